```python
import math
import jax, jax.numpy as jnp
from jax import lax
import numpy as np

D_MODEL = 1024
BATCH = 8
SEQ = 4096
DEPTH = 4

N_A_LAYERS = DEPTH // 2
N_B_LAYERS = DEPTH - N_A_LAYERS

HEAD_DIM = 64
SB_HEADS = D_MODEL // HEAD_DIM
SWA_Q_HEADS = D_MODEL // HEAD_DIM
SWA_KV_HEADS = SWA_Q_HEADS // 8
SWA_GROUP = SWA_Q_HEADS // SWA_KV_HEADS
WINDOW = 128
BLOCK = 128
D_FF = 4 * D_MODEL
NUM_BUCKETS = 32
MAX_EXACT = NUM_BUCKETS // 2
MAX_DISTANCE = 128
EPS = 1e-5
NEG_INF = -1e30

kernel_name = "yoco_stickbreaking_swa_sinks_trunk"


def rmsnorm(x, g):
    xf = x.astype(jnp.float32)
    y = xf * lax.rsqrt(jnp.mean(xf * xf, axis=-1, keepdims=True) + EPS)
    return (y * g.astype(jnp.float32)).astype(x.dtype)


def sq_relu_mlp(x, w_up, w_down):
    u = jax.nn.relu(x @ w_up)
    return (u * u) @ w_down


def stick_breaking_attention(x, w_qkv, w_o):
    b, s_len, _ = x.shape
    nb = s_len // BLOCK
    scale = 1.0 / math.sqrt(HEAD_DIM)
    q, k, v = jnp.split(x @ w_qkv, 3, axis=-1)
    q = q.reshape(b, nb, BLOCK, SB_HEADS, HEAD_DIM).transpose(1, 0, 3, 2, 4)
    k = k.reshape(b, s_len, SB_HEADS, HEAD_DIM).transpose(0, 2, 1, 3)
    v = v.reshape(b, s_len, SB_HEADS, HEAD_DIM).transpose(0, 2, 1, 3)
    key_pos = jnp.arange(s_len)

    def one_block(args):
        q_blk, blk = args
        z = jnp.einsum('bhqd,bhkd->bhqk', q_blk, k, preferred_element_type=jnp.float32) * scale
        q_pos = blk * BLOCK + jnp.arange(BLOCK)
        causal = key_pos[None, :] < q_pos[:, None]
        log_1m = jnp.where(causal, jax.nn.log_sigmoid(-z), 0.0)
        after = lax.cumsum(log_1m, axis=3, reverse=True) - log_1m
        w = jnp.where(causal, jnp.exp(jax.nn.log_sigmoid(z) + after), 0.0)
        return jnp.einsum('bhqk,bhkd->bhqd', w.astype(v.dtype), v)

    o = lax.map(one_block, (q, jnp.arange(nb)))
    o = o.transpose(1, 0, 3, 2, 4).reshape(b, s_len, SB_HEADS * HEAD_DIM)
    return o @ w_o


def t5_bucket(n):
    nf = jnp.maximum(n, 1).astype(jnp.float32)
    large = MAX_EXACT + (jnp.log(nf / MAX_EXACT) / math.log(MAX_DISTANCE / MAX_EXACT)
                         * (NUM_BUCKETS - MAX_EXACT)).astype(jnp.int32)
    large = jnp.minimum(large, NUM_BUCKETS - 1)
    return jnp.where(n < MAX_EXACT, n, large)


def band_distance():
    qi = jnp.arange(BLOCK)[:, None]
    kj = jnp.arange(2 * BLOCK)[None, :]
    return qi + BLOCK - kj


def relative_bias_band(rel_bias):
    bucket = t5_bucket(jnp.maximum(band_distance(), 0))
    bias = rel_bias.astype(jnp.float32)[bucket]
    return bias.transpose(2, 0, 1).reshape(SWA_KV_HEADS, SWA_GROUP, BLOCK, 2 * BLOCK)


def band_valid(nb):
    dist = band_distance()
    key_pos = jnp.arange(nb)[:, None, None] * BLOCK - BLOCK + jnp.arange(2 * BLOCK)[None, None, :]
    return (dist >= 0) & (dist < WINDOW) & (key_pos >= 0)


def to_band(t):
    b, s_len, g, dh = t.shape
    nb = s_len // BLOCK
    prev = jnp.pad(t, ((0, 0), (BLOCK, 0), (0, 0), (0, 0)))[:, :s_len].reshape(b, nb, BLOCK, g, dh)
    cur = t.reshape(b, nb, BLOCK, g, dh)
    return jnp.concatenate([prev, cur], axis=2)


def shared_kv(h, kv_norm, w_kv, b_kv):
    b, s_len, _ = h.shape
    kv = rmsnorm(h, kv_norm) @ w_kv + b_kv
    k, v = jnp.split(kv, 2, axis=-1)
    k = k.reshape(b, s_len, SWA_KV_HEADS, HEAD_DIM)
    v = v.reshape(b, s_len, SWA_KV_HEADS, HEAD_DIM)
    return to_band(k), to_band(v)


def swa_sink_attention(x, k_band, v_band, w_q, b_q, sinks, w_o, b_o, bias_band, valid):
    b, s_len, _ = x.shape
    nb = s_len // BLOCK
    scale = 1.0 / math.sqrt(HEAD_DIM)
    q = (x @ w_q + b_q).reshape(b, nb, BLOCK, SWA_KV_HEADS, SWA_GROUP, HEAD_DIM)
    s = jnp.einsum('bnqgrd,bnkgd->bngrqk', q, k_band, preferred_element_type=jnp.float32) * scale
    s = s + bias_band[None, None]
    s = jnp.where(valid[None, :, None, None], s, NEG_INF)
    sink = jnp.broadcast_to(sinks.astype(jnp.float32).reshape(SWA_KV_HEADS, SWA_GROUP, 1, 1),
                            s.shape[:-1] + (1,))
    p = jax.nn.softmax(jnp.concatenate([s, sink], axis=-1), axis=-1)[..., :-1]
    o = jnp.einsum('bngrqk,bnkgd->bnqgrd', p.astype(v_band.dtype), v_band)
    return o.reshape(b, s_len, SWA_Q_HEADS * HEAD_DIM) @ w_o + b_o


def setup_inputs(seed: int = 0) -> dict:
    key = jax.random.key(seed)
    ks = jax.random.split(key, 20)
    f32 = jnp.float32

    def nrm(k, shape, fan_in):
        return jax.random.normal(k, shape, f32) * fan_in ** -0.5

    def gain(k, shape):
        return 1.0 + 0.02 * jax.random.normal(k, shape, f32)

    kv_width = 2 * SWA_KV_HEADS * HEAD_DIM
    return {
        "x": jax.random.normal(ks[0], (BATCH, SEQ, D_MODEL), f32),
        "a_norm": gain(ks[1], (N_A_LAYERS, D_MODEL)),
        "a_wqkv": nrm(ks[2], (N_A_LAYERS, D_MODEL, 3 * SB_HEADS * HEAD_DIM), D_MODEL),
        "a_wo": nrm(ks[3], (N_A_LAYERS, SB_HEADS * HEAD_DIM, D_MODEL), SB_HEADS * HEAD_DIM),
        "kv_norm": gain(ks[4], (D_MODEL,)),
        "w_kv": nrm(ks[5], (D_MODEL, kv_width), D_MODEL),
        "b_kv": 0.02 * jax.random.normal(ks[6], (kv_width,), f32),
        "b_norm": gain(ks[7], (N_B_LAYERS, D_MODEL)),
        "b_wq": nrm(ks[8], (N_B_LAYERS, D_MODEL, SWA_Q_HEADS * HEAD_DIM), D_MODEL),
        "b_bq": 0.02 * jax.random.normal(ks[9], (N_B_LAYERS, SWA_Q_HEADS * HEAD_DIM), f32),
        "b_sinks": 0.5 * jax.random.normal(ks[10], (N_B_LAYERS, SWA_Q_HEADS), f32),
        "b_wo": nrm(ks[11], (N_B_LAYERS, SWA_Q_HEADS * HEAD_DIM, D_MODEL), SWA_Q_HEADS * HEAD_DIM),
        "b_bo": 0.02 * jax.random.normal(ks[12], (N_B_LAYERS, D_MODEL), f32),
        "rel_bias": 0.5 * jax.random.normal(ks[13], (NUM_BUCKETS, SWA_Q_HEADS), f32),
        "mlp_norm": gain(ks[14], (DEPTH, D_MODEL)),
        "mlp_up": nrm(ks[15], (DEPTH, D_MODEL, D_FF), D_MODEL),
        "mlp_down": nrm(ks[16], (DEPTH, D_FF, D_MODEL), D_FF),
        "final_norm": gain(ks[17], (D_MODEL,)),
    }


def reference(x, a_norm, a_wqkv, a_wo, kv_norm, w_kv, b_kv, b_norm, b_wq, b_bq, b_sinks,
              b_wo, b_bo, rel_bias, mlp_norm, mlp_up, mlp_down, final_norm):
    nb = x.shape[1] // BLOCK
    bias_band = relative_bias_band(rel_bias)
    valid = band_valid(nb)
    h = x
    k_band = v_band = None
    for layer in range(DEPTH):
        if layer < N_A_LAYERS:
            h = h + stick_breaking_attention(rmsnorm(h, a_norm[layer]), a_wqkv[layer], a_wo[layer])
        else:
            j = layer - N_A_LAYERS
            if j == 0:
                k_band, v_band = shared_kv(h, kv_norm, w_kv, b_kv)
            h = h + swa_sink_attention(rmsnorm(h, b_norm[j]), k_band, v_band, b_wq[j], b_bq[j],
                                       b_sinks[j], b_wo[j], b_bo[j], bias_band, valid)
        h = h + sq_relu_mlp(rmsnorm(h, mlp_norm[layer]), mlp_up[layer], mlp_down[layer])
    return rmsnorm(h, final_norm)
```

```python
import functools
import math

import jax
import jax.numpy as jnp
from jax import lax
from jax.experimental import pallas as pl
from jax.experimental.pallas import tpu as pltpu

HEAD_DIM = 64
PAIR = 2 * HEAD_DIM
BLOCK = 128
WINDOW = 128
SWA_GROUP = 8
NUM_BUCKETS = 32
MAX_EXACT = NUM_BUCKETS // 2
MAX_DISTANCE = 128
EPS = 1e-5
NEG_INF = -1e30
SCALE = 1.0 / math.sqrt(HEAD_DIM)

SB_UNDERFLOW = -104.0
SB_STATIC_BLOCKS = 3

ROW_TILE = 512
FF_CHUNK = 1024
VMEM_LIMIT_BYTES = 56 * 1024 * 1024

_F32 = jnp.float32
_BF16 = jnp.bfloat16


def _dot(a, b):
    return jnp.dot(a, b, preferred_element_type=_F32)


def _dot_nt(a, b):
    return lax.dot_general(a, b, (((1,), (1,)), ((), ())), preferred_element_type=_F32)


def _rms_scale(x, g):
    ms = jnp.mean(x * x, axis=-1, keepdims=True)
    return x * lax.rsqrt(ms + EPS) * g


def _resident(shape):
    return pl.BlockSpec(shape, lambda *_: (0,) * len(shape), pipeline_mode=pl.Buffered(1))


def _norm_matmul_kernel(x_ref, g_ref, w_ref, b_ref, o_ref, *, col_chunk):
    xn = _rms_scale(x_ref[...], g_ref[...]).astype(_BF16)
    n = o_ref.shape[-1]
    for c in range(0, n, col_chunk):
        y = _dot(xn, w_ref[:, c:c + col_chunk]) + b_ref[:, c:c + col_chunk]
        o_ref[:, c:c + col_chunk] = y.astype(o_ref.dtype)


def _norm_matmul(x, g, w, b):
    m, d = x.shape
    n = w.shape[1]
    tm = min(ROW_TILE, m)
    return pl.pallas_call(
        functools.partial(_norm_matmul_kernel, col_chunk=min(n, 512)),
        grid=(m // tm,),
        in_specs=[
            pl.BlockSpec((tm, d), lambda i: (i, 0)),
            _resident((1, d)),
            _resident((d, n)),
            _resident((1, n)),
        ],
        out_specs=pl.BlockSpec((tm, n), lambda i: (i, 0)),
        out_shape=jax.ShapeDtypeStruct((m, n), _BF16),
        compiler_params=pltpu.CompilerParams(
            dimension_semantics=("arbitrary",), vmem_limit_bytes=VMEM_LIMIT_BYTES),
        name="norm_matmul",
    )(x, g.reshape(1, d), w, b.reshape(1, n))


def _proj_mlp_kernel(h_ref, a_ref, wo_ref, bo_ref, g_ref, wup_ref, wdn_ref, fg_ref, o_ref,
                     *, final_norm):
    h1 = h_ref[...] + _dot(a_ref[...], wo_ref[...]) + bo_ref[...]
    xn = _rms_scale(h1, g_ref[...]).astype(_BF16)
    acc = h1
    d_ff = wup_ref.shape[1]
    for c in range(0, d_ff, FF_CHUNK):
        u = jnp.maximum(_dot(xn, wup_ref[:, c:c + FF_CHUNK]), 0.0)
        acc = acc + _dot((u * u).astype(_BF16), wdn_ref[c:c + FF_CHUNK, :])
    if final_norm:
        acc = _rms_scale(acc, fg_ref[...])
    o_ref[...] = acc


def _proj_mlp(h, a, wo, bo, g, wup, wdn, fg, final_norm):
    m, d = h.shape
    d_ff = wup.shape[1]
    tm = min(ROW_TILE, m)
    row = lambda i: (i, 0)
    return pl.pallas_call(
        functools.partial(_proj_mlp_kernel, final_norm=final_norm),
        grid=(m // tm,),
        in_specs=[
            pl.BlockSpec((tm, d), row),
            pl.BlockSpec((tm, d), row),
            _resident((d, d)),
            _resident((1, d)),
            _resident((1, d)),
            _resident((d, d_ff)),
            _resident((d_ff, d)),
            _resident((1, d)),
        ],
        out_specs=pl.BlockSpec((tm, d), row),
        out_shape=jax.ShapeDtypeStruct((m, d), _F32),
        compiler_params=pltpu.CompilerParams(
            dimension_semantics=("arbitrary",), vmem_limit_bytes=VMEM_LIMIT_BYTES),
        name="proj_mlp",
    )(h, a, wo, bo.reshape(1, d), g.reshape(1, d), wup, wdn, fg.reshape(1, d))


def _sb_attn_kernel(q_ref, k_ref, v_ref, o_ref, acc_ref, carry_ref):
    n_q = q_ref.shape[0] // BLOCK
    row = lax.broadcasted_iota(jnp.int32, (BLOCK, BLOCK), 0)
    col = lax.broadcasted_iota(jnp.int32, (BLOCK, BLOCK), 1)
    head_lanes = (col < HEAD_DIM, col >= HEAD_DIM)
    strictly_causal = col < row
    suffix_and_ones = jnp.concatenate(
        [(row >= col).astype(_BF16), jnp.ones((BLOCK, BLOCK), _BF16)], axis=1)

    def one_block(qh, k, vh, carry, mask):
        z = _dot_nt(qh, k)
        sp = jnp.maximum(z, 0.0) + jnp.log(1.0 + jnp.exp(-jnp.abs(z)))
        if mask is not None:
            sp = jnp.where(mask, sp, 0.0)
        hi = sp.astype(_BF16)
        lo = (sp - hi.astype(_F32)).astype(_BF16)
        cs = _dot(hi, suffix_and_ones) + _dot(lo, suffix_and_ones)
        w = jnp.exp(z - cs[:, :BLOCK] - carry)
        if mask is not None:
            w = jnp.where(mask, w, 0.0)
        return _dot(w.astype(_BF16), vh), carry + cs[:, BLOCK:]

    def block_rows(j):
        start = j * BLOCK
        return pl.ds(start if isinstance(j, int) else pl.multiple_of(start, BLOCK), BLOCK)

    def load_heads(i):
        q = q_ref[block_rows(i), :]
        return [jnp.where(m, q, 0) * jnp.asarray(SCALE, q.dtype) for m in head_lanes]

    def load_kv(j):
        rows = block_rows(j)
        k = k_ref[rows, :]
        v = v_ref[rows, :]
        return k, [jnp.where(m, v, 0) for m in head_lanes]

    def q_tile(i, n_static):
        qs = load_heads(i)
        acc = jnp.zeros((BLOCK, BLOCK), _F32)
        carry = [jnp.zeros((BLOCK, BLOCK), _F32) for _ in qs]
        for c in range(n_static):
            k, vs = load_kv(i - c)
            for h in range(2):
                out, carry[h] = one_block(qs[h], k, vs[h], carry[h],
                                          strictly_causal if c == 0 else None)
                acc = acc + out
        acc_ref[...] = acc
        carry_ref[0] = carry[0]
        carry_ref[1] = carry[1]

        if isinstance(i, int) and i - n_static < 0:
            pass
        else:
            def more(state):
                j, least = state
                return jnp.logical_and(j >= 0, least < -SB_UNDERFLOW)

            def older_block(state):
                j, _ = state
                k, vs = load_kv(j)
                total = acc_ref[...]
                new_carry = []
                for h in range(2):
                    out, ch = one_block(qs[h], k, vs[h], carry_ref[h], None)
                    total = total + out
                    carry_ref[h] = ch
                    new_carry.append(ch)
                acc_ref[...] = total
                return j - 1, jnp.min(jnp.minimum(new_carry[0], new_carry[1]))

            lax.while_loop(more, older_block,
                           (i - n_static, jnp.min(jnp.minimum(carry[0], carry[1]))))

        o_ref[block_rows(i), :] = acc_ref[...].astype(o_ref.dtype)

    n_peel = min(n_q, SB_STATIC_BLOCKS - 1)
    for i in range(n_peel):
        q_tile(i, i + 1)
    if n_q > n_peel:
        def body(i, _):
            q_tile(i, SB_STATIC_BLOCKS)
            return 0
        lax.fori_loop(n_peel, n_q, body, 0)


def _sb_attention(qkv, batch, seq, n_heads):
    n_pairs = n_heads // 2
    blk = (None, seq, PAIR)
    return pl.pallas_call(
        _sb_attn_kernel,
        grid=(batch, n_pairs),
        in_specs=[
            pl.BlockSpec(blk, lambda b, p: (b, 0, p)),
            pl.BlockSpec(blk, lambda b, p: (b, 0, n_pairs + p)),
            pl.BlockSpec(blk, lambda b, p: (b, 0, 2 * n_pairs + p)),
        ],
        out_specs=pl.BlockSpec(blk, lambda b, p: (b, 0, p)),
        out_shape=jax.ShapeDtypeStruct((batch, seq, n_heads * HEAD_DIM), _BF16),
        scratch_shapes=[pltpu.VMEM((BLOCK, BLOCK), _F32), pltpu.VMEM((2, BLOCK, BLOCK), _F32)],
        compiler_params=pltpu.CompilerParams(
            dimension_semantics=("arbitrary", "arbitrary"), vmem_limit_bytes=VMEM_LIMIT_BYTES),
        name="sb_attention",
    )(qkv, qkv, qkv)


def _bias_band_kernel(rel_ref, bucket_ref, o_ref):
    first = pl.program_id(0) == 0
    head = pl.program_id(1)
    bucket = bucket_ref[...]
    qi = lax.broadcasted_iota(jnp.int32, bucket.shape, 0)
    kj = lax.broadcasted_iota(jnp.int32, bucket.shape, 1)
    dist = qi + BLOCK - kj
    valid = (dist >= 0) & (dist < WINDOW) & jnp.logical_or(jnp.logical_not(first), kj >= BLOCK)
    bias = jnp.zeros(bucket.shape, _F32)
    for b in range(NUM_BUCKETS):
        bias = jnp.where(bucket == b, rel_ref[b, head], bias)
    o_ref[...] = jnp.where(valid, bias, NEG_INF)


def _bias_band(rel_bias, bucket):
    n_heads = rel_bias.shape[1]
    return pl.pallas_call(
        _bias_band_kernel,
        grid=(2, n_heads),
        in_specs=[
            pl.BlockSpec(memory_space=pltpu.SMEM),
            pl.BlockSpec(bucket.shape, lambda t, h: (0, 0)),
        ],
        out_specs=pl.BlockSpec((None, None) + bucket.shape, lambda t, h: (t, h, 0, 0)),
        out_shape=jax.ShapeDtypeStruct((2, n_heads) + bucket.shape, _F32),
        name="bias_band",
    )(rel_bias, bucket)


def _band_buckets():
    qi = jnp.arange(BLOCK)[:, None]
    kj = jnp.arange(2 * BLOCK)[None, :]
    n = jnp.maximum(qi + BLOCK - kj, 0)
    nf = jnp.maximum(n, 1).astype(_F32)
    large = MAX_EXACT + (jnp.log(nf / MAX_EXACT) / math.log(MAX_DISTANCE / MAX_EXACT)
                         * (NUM_BUCKETS - MAX_EXACT)).astype(jnp.int32)
    large = jnp.minimum(large, NUM_BUCKETS - 1)
    return jnp.where(n < MAX_EXACT, n, large).astype(jnp.int32)


def _swa_kernel(sink_ref, q_ref, kvp_ref, kvc_ref, bias_ref, o_ref):
    n_pairs = q_ref.shape[-1] // PAIR
    n_kv = kvc_ref.shape[-1] // (2 * PAIR)
    pairs_per_kv = n_pairs // n_kv
    col = lax.broadcasted_iota(jnp.int32, (BLOCK, PAIR), 1)
    col2 = lax.broadcasted_iota(jnp.int32, (2 * BLOCK, PAIR), 1)
    head_lanes = (col < HEAD_DIM, col >= HEAD_DIM)
    head_lanes2 = (col2 < HEAD_DIM, col2 >= HEAD_DIM)

    for g in range(n_kv):
        kc = slice(g * PAIR, (g + 1) * PAIR)
        vc = slice((n_kv + g) * PAIR, (n_kv + g + 1) * PAIR)
        k = jnp.concatenate([kvp_ref[:, kc], kvc_ref[:, kc]], axis=0)
        v = jnp.concatenate([kvp_ref[:, vc], kvc_ref[:, vc]], axis=0)
        vs = [jnp.where(m, v, 0) for m in head_lanes2]
        for j in range(pairs_per_kv):
            p = g * pairs_per_kv + j
            q = q_ref[:, p * PAIR:(p + 1) * PAIR]
            out = jnp.zeros((BLOCK, PAIR), _F32)
            for h in range(2):
                head = 2 * p + h
                qh = jnp.where(head_lanes[h], q, 0) * jnp.asarray(SCALE, q.dtype)
                s = _dot_nt(qh, k) + bias_ref[head]
                sink = sink_ref[head]
                m = jnp.maximum(jnp.max(s, axis=-1, keepdims=True), sink)
                e = jnp.exp(s - m)
                denom = jnp.sum(e, axis=-1, keepdims=True) + jnp.exp(sink - m)
                out = out + _dot(e.astype(_BF16), vs[h]) / denom
            o_ref[:, p * PAIR:(p + 1) * PAIR] = out.astype(o_ref.dtype)


def _swa_attention(q, kv, bias_tab, sinks):
    batch, seq, d = q.shape
    n_q = seq // BLOCK
    kvw = kv.shape[-1]
    return pl.pallas_call(
        _swa_kernel,
        grid=(batch, n_q),
        in_specs=[
            pl.BlockSpec(memory_space=pltpu.SMEM),
            pl.BlockSpec((None, BLOCK, d), lambda b, n: (b, n, 0)),
            pl.BlockSpec((None, BLOCK, kvw), lambda b, n: (b, jnp.maximum(n - 1, 0), 0)),
            pl.BlockSpec((None, BLOCK, kvw), lambda b, n: (b, n, 0)),
            pl.BlockSpec((None,) + bias_tab.shape[1:], lambda b, n: (jnp.minimum(n, 1), 0, 0, 0)),
        ],
        out_specs=pl.BlockSpec((None, BLOCK, d), lambda b, n: (b, n, 0)),
        out_shape=jax.ShapeDtypeStruct((batch, seq, d), _BF16),
        compiler_params=pltpu.CompilerParams(
            dimension_semantics=("arbitrary", "arbitrary"), vmem_limit_bytes=VMEM_LIMIT_BYTES),
        name="swa_attention",
    )(sinks, q, kv, kv, bias_tab)


def _duplicate_heads(w, n_heads):
    lead = w.shape[:-1]
    w = w.reshape(lead + (n_heads, 1, HEAD_DIM))
    w = jnp.broadcast_to(w, lead + (n_heads, 2, HEAD_DIM))
    return w.reshape(lead + (n_heads * PAIR,))


def kernel(x, a_norm, a_wqkv, a_wo, kv_norm, w_kv, b_kv, b_norm, b_wq, b_bq, b_sinks,
           b_wo, b_bo, rel_bias, mlp_norm, mlp_up, mlp_down, final_norm):
    batch, seq, d = x.shape
    m = batch * seq
    n_a = a_wqkv.shape[0]
    n_b = b_wq.shape[0]
    depth = n_a + n_b
    n_heads = d // HEAD_DIM
    n_kv = w_kv.shape[1] // (2 * HEAD_DIM)
    assert seq % BLOCK == 0 and m % min(ROW_TILE, m) == 0 and n_heads % (2 * n_kv) == 0

    bf = lambda w: w.astype(_BF16)
    zeros_d = jnp.zeros((d,), _F32)
    h = x.reshape(m, d)

    bias_tab = _bias_band(rel_bias, _band_buckets())
    kv = None
    for layer in range(depth):
        last = layer == depth - 1
        if layer < n_a:
            qkv = _norm_matmul(h, a_norm[layer], bf(a_wqkv[layer]),
                               jnp.zeros((a_wqkv.shape[2],), _F32))
            attn = _sb_attention(qkv.reshape(batch, seq, -1), batch, seq, n_heads)
            wo, bo = a_wo[layer], zeros_d
        else:
            j = layer - n_a
            if j == 0:
                wk, wv = jnp.split(w_kv, 2, axis=-1)
                bk, bv = jnp.split(b_kv, 2, axis=-1)
                w_dup = jnp.concatenate([_duplicate_heads(wk, n_kv), _duplicate_heads(wv, n_kv)], -1)
                b_dup = jnp.concatenate([_duplicate_heads(bk, n_kv), _duplicate_heads(bv, n_kv)], -1)
                kv = _norm_matmul(h, kv_norm, bf(w_dup), b_dup).reshape(batch, seq, -1)
            q = _norm_matmul(h, b_norm[j], bf(b_wq[j]), b_bq[j])
            attn = _swa_attention(q.reshape(batch, seq, d), kv, bias_tab, b_sinks[j])
            wo, bo = b_wo[j], b_bo[j]
        h = _proj_mlp(h, attn.reshape(m, d), bf(wo), bo, mlp_norm[layer],
                      bf(mlp_up[layer]), bf(mlp_down[layer]), final_norm, last)
    return h.reshape(batch, seq, d)
```

```python
import functools
import math

import jax
import jax.numpy as jnp
from jax import lax
from jax.experimental import pallas as pl
from jax.experimental.pallas import tpu as pltpu

HEAD_DIM = 64
PAIR = 2 * HEAD_DIM
BLOCK = 128
WINDOW = 128
SWA_GROUP = 8
NUM_BUCKETS = 32
MAX_EXACT = NUM_BUCKETS // 2
MAX_DISTANCE = 128
EPS = 1e-5
NEG_INF = -1e30
SCALE = 1.0 / math.sqrt(HEAD_DIM)

LOG2_E = math.log2(math.e)
SB_UNDERFLOW_LOG2 = -151.0
SB_STATIC_BLOCKS = 3
SB_TILES_PER_STEP = 3

ROW_TILE = 512
FF_CHUNK = 1024
VMEM_LIMIT_BYTES = 56 * 1024 * 1024

_F32 = jnp.float32
_BF16 = jnp.bfloat16


def _dot(a, b):
    return jnp.dot(a, b, preferred_element_type=_F32)


def _dot_nt(a, b):
    return lax.dot_general(a, b, (((1,), (1,)), ((), ())), preferred_element_type=_F32)


def _rms_scale(x, g):
    ms = jnp.mean(x * x, axis=-1, keepdims=True)
    return x * lax.rsqrt(ms + EPS) * g


def _resident(shape):
    return pl.BlockSpec(shape, lambda *_: (0,) * len(shape), pipeline_mode=pl.Buffered(1))


def _norm_matmul_kernel(x_ref, g_ref, w_ref, b_ref, o_ref, *, col_chunk):
    xn = _rms_scale(x_ref[...], g_ref[...]).astype(_BF16)
    n = o_ref.shape[-1]
    for c in range(0, n, col_chunk):
        y = _dot(xn, w_ref[:, c:c + col_chunk]) + b_ref[:, c:c + col_chunk]
        o_ref[:, c:c + col_chunk] = y.astype(o_ref.dtype)


def _norm_matmul(x, g, w, b):
    m, d = x.shape
    n = w.shape[1]
    tm = min(ROW_TILE, m)
    return pl.pallas_call(
        functools.partial(_norm_matmul_kernel, col_chunk=min(n, 512)),
        grid=(m // tm,),
        in_specs=[
            pl.BlockSpec((tm, d), lambda i: (i, 0)),
            _resident((1, d)),
            _resident((d, n)),
            _resident((1, n)),
        ],
        out_specs=pl.BlockSpec((tm, n), lambda i: (i, 0)),
        out_shape=jax.ShapeDtypeStruct((m, n), _BF16),
        compiler_params=pltpu.CompilerParams(
            dimension_semantics=("arbitrary",), vmem_limit_bytes=VMEM_LIMIT_BYTES),
        name="norm_matmul",
    )(x, g.reshape(1, d), w, b.reshape(1, n))


def _proj_mlp_kernel(h_ref, a_ref, wo_ref, bo_ref, g_ref, wup_ref, wdn_ref, fg_ref, o_ref,
                     *, final_norm):
    h1 = h_ref[...] + _dot(a_ref[...], wo_ref[...]) + bo_ref[...]
    xn = _rms_scale(h1, g_ref[...]).astype(_BF16)
    acc = h1
    d_ff = wup_ref.shape[1]
    for c in range(0, d_ff, FF_CHUNK):
        u = jnp.maximum(_dot(xn, wup_ref[:, c:c + FF_CHUNK]), 0.0)
        acc = acc + _dot((u * u).astype(_BF16), wdn_ref[c:c + FF_CHUNK, :])
    if final_norm:
        acc = _rms_scale(acc, fg_ref[...])
    o_ref[...] = acc


def _proj_mlp(h, a, wo, bo, g, wup, wdn, fg, final_norm):
    m, d = h.shape
    d_ff = wup.shape[1]
    tm = min(ROW_TILE, m)
    row = lambda i: (i, 0)
    return pl.pallas_call(
        functools.partial(_proj_mlp_kernel, final_norm=final_norm),
        grid=(m // tm,),
        in_specs=[
            pl.BlockSpec((tm, d), row),
            pl.BlockSpec((tm, d), row),
            _resident((d, d)),
            _resident((1, d)),
            _resident((1, d)),
            _resident((d, d_ff)),
            _resident((d_ff, d)),
            _resident((1, d)),
        ],
        out_specs=pl.BlockSpec((tm, d), row),
        out_shape=jax.ShapeDtypeStruct((m, d), _F32),
        compiler_params=pltpu.CompilerParams(
            dimension_semantics=("arbitrary",), vmem_limit_bytes=VMEM_LIMIT_BYTES),
        name="proj_mlp",
    )(h, a, wo, bo.reshape(1, d), g.reshape(1, d), wup, wdn, fg.reshape(1, d))


def _sb_attn_kernel(q_ref, k_ref, v_ref, o_ref, acc_ref, carry_ref):
    n_q = q_ref.shape[0] // BLOCK
    row = lax.broadcasted_iota(jnp.int32, (BLOCK, BLOCK), 0)
    col = lax.broadcasted_iota(jnp.int32, (BLOCK, BLOCK), 1)
    first_head = col < HEAD_DIM
    strictly_causal = col < row
    suffix_and_ones = jnp.concatenate(
        [(row >= col).astype(_BF16), jnp.ones((BLOCK, BLOCK), _BF16)], axis=1)
    suffix_and_ones = jnp.concatenate([suffix_and_ones, suffix_and_ones], axis=0)

    def block_rows(j, n=1):
        start = j * BLOCK
        return pl.ds(start if isinstance(j, int) else pl.multiple_of(start, BLOCK), n * BLOCK)

    def load_q(i):
        q = q_ref[block_rows(i), :]
        return jnp.concatenate([jnp.where(first_head, q, 0), jnp.where(first_head, 0, q)], axis=0)

    def scores(q2, k):
        return _dot_nt(q2, k)

    def suffix_sums(z, diag):
        n_b = z.shape[1] // BLOCK
        neg_abs = lax.bitcast_convert_type(
            lax.bitcast_convert_type(z, jnp.uint32) | jnp.uint32(0x80000000), _F32)
        sp = jnp.maximum(z, 0.0) + jnp.log2(1.0 + jnp.exp2(neg_abs))
        split = []
        for h in range(2):
            for b in range(n_b):
                s = sp[h * BLOCK:(h + 1) * BLOCK, b * BLOCK:(b + 1) * BLOCK]
                if diag and b == n_b - 1:
                    s = jnp.where(strictly_causal, s, 0.0)
                hi = s.astype(_BF16)
                lo = (s - hi.astype(_F32)).astype(_BF16)
                split.append(jnp.concatenate([hi, lo], axis=1))
        return _dot(jnp.concatenate(split, axis=0), suffix_and_ones)

    def weighted_values(z, cs, v, carry_in, diag):
        n_b = z.shape[1] // BLOCK
        ws, carries = [], []
        for h in range(2):
            carry = None if carry_in is None else carry_in[h]
            wb = [None] * n_b
            for b in reversed(range(n_b)):
                r = cs[(h * n_b + b) * BLOCK:(h * n_b + b + 1) * BLOCK]
                arg = z[h * BLOCK:(h + 1) * BLOCK, b * BLOCK:(b + 1) * BLOCK] - r[:, :BLOCK]
                if carry is not None:
                    arg = arg - carry
                w = jnp.exp2(arg)
                if diag and b == n_b - 1:
                    w = jnp.where(strictly_causal, w, 0.0)
                wb[b] = w.astype(_BF16)
                carry = r[:, BLOCK:] if carry is None else carry + r[:, BLOCK:]
            ws.append(jnp.concatenate(wb, axis=1))
            carries.append(carry)
        out = _dot(jnp.concatenate(ws, axis=0), v)
        return jnp.where(first_head, out[:BLOCK], out[BLOCK:]), carries

    def attend(jobs, diag):
        zs = [scores(q2, k) for q2, k, _, _ in jobs]
        css = [suffix_sums(z, diag) for z in zs]
        return [weighted_values(z, cs, v, carry_in, diag)
                for z, cs, (_, _, v, carry_in) in zip(zs, css, jobs)]

    def q_tiles(tiles, n_static):
        q2s = [load_q(i) for i in tiles]
        jobs = []
        for i, q2 in zip(tiles, q2s):
            window = block_rows(i - (n_static - 1), n_static)
            jobs.append((q2, k_ref[window, :], v_ref[window, :], None))
        results = attend(jobs, True)

        if all(isinstance(i, int) and i - n_static < 0 for i in tiles):
            for i, (out, _) in zip(tiles, results):
                o_ref[block_rows(i), :] = out.astype(o_ref.dtype)
            return

        least_mass = []
        for t, (out, carry) in enumerate(results):
            acc_ref[t] = out
            carry_ref[t, 0] = carry[0]
            carry_ref[t, 1] = carry[1]
            least_mass.append(jnp.min(jnp.minimum(carry[0], carry[1])))

        for t, (i, q2) in enumerate(zip(tiles, q2s)):
            def more(state):
                j, least = state
                return jnp.logical_and(j >= 0, least < -SB_UNDERFLOW_LOG2)

            def older_block(state, t=t, q2=q2):
                j, _ = state
                rows = block_rows(j)
                (out, carry), = attend(
                    [(q2, k_ref[rows, :], v_ref[rows, :], [carry_ref[t, 0], carry_ref[t, 1]])], False)
                acc_ref[t] += out
                carry_ref[t, 0] = carry[0]
                carry_ref[t, 1] = carry[1]
                return j - 1, jnp.min(jnp.minimum(carry[0], carry[1]))

            lax.while_loop(more, older_block, (i - n_static, least_mass[t]))
            o_ref[block_rows(i), :] = acc_ref[t].astype(o_ref.dtype)

    n_peel = min(n_q, SB_STATIC_BLOCKS - 1)
    for i in range(n_peel):
        q_tiles([i], i + 1)
    n_groups = (n_q - n_peel) // SB_TILES_PER_STEP
    if n_groups > 0:
        def body(g, _):
            first = n_peel + g * SB_TILES_PER_STEP
            q_tiles([first + t for t in range(SB_TILES_PER_STEP)], SB_STATIC_BLOCKS)
            return 0
        lax.fori_loop(0, n_groups, body, 0)
    for i in range(n_peel + n_groups * SB_TILES_PER_STEP, n_q):
        q_tiles([i], SB_STATIC_BLOCKS)


def _sb_attention(qkv, batch, seq, n_heads):
    n_pairs = n_heads // 2
    blk = (None, seq, PAIR)
    return pl.pallas_call(
        _sb_attn_kernel,
        grid=(batch, n_pairs),
        in_specs=[
            pl.BlockSpec(blk, lambda b, p: (b, 0, p)),
            pl.BlockSpec(blk, lambda b, p: (b, 0, n_pairs + p)),
            pl.BlockSpec(blk, lambda b, p: (b, 0, 2 * n_pairs + p)),
        ],
        out_specs=pl.BlockSpec(blk, lambda b, p: (b, 0, p)),
        out_shape=jax.ShapeDtypeStruct((batch, seq, n_heads * HEAD_DIM), _BF16),
        scratch_shapes=[pltpu.VMEM((SB_TILES_PER_STEP, BLOCK, BLOCK), _F32),
                        pltpu.VMEM((SB_TILES_PER_STEP, 2, BLOCK, BLOCK), _F32)],
        compiler_params=pltpu.CompilerParams(
            dimension_semantics=("arbitrary", "arbitrary"), vmem_limit_bytes=VMEM_LIMIT_BYTES),
        name="sb_attention",
    )(qkv, qkv, qkv)


def _bias_band_kernel(rel_ref, bucket_ref, o_ref):
    first = pl.program_id(0) == 0
    head = pl.program_id(1)
    bucket = bucket_ref[...]
    qi = lax.broadcasted_iota(jnp.int32, bucket.shape, 0)
    kj = lax.broadcasted_iota(jnp.int32, bucket.shape, 1)
    dist = qi + BLOCK - kj
    valid = (dist >= 0) & (dist < WINDOW) & jnp.logical_or(jnp.logical_not(first), kj >= BLOCK)
    bias = jnp.zeros(bucket.shape, _F32)
    for b in range(NUM_BUCKETS):
        bias = jnp.where(bucket == b, rel_ref[b, head], bias)
    o_ref[...] = jnp.where(valid, bias, NEG_INF)


def _bias_band(rel_bias, bucket):
    n_heads = rel_bias.shape[1]
    return pl.pallas_call(
        _bias_band_kernel,
        grid=(2, n_heads),
        in_specs=[
            pl.BlockSpec(memory_space=pltpu.SMEM),
            pl.BlockSpec(bucket.shape, lambda t, h: (0, 0)),
        ],
        out_specs=pl.BlockSpec((None, None) + bucket.shape, lambda t, h: (t, h, 0, 0)),
        out_shape=jax.ShapeDtypeStruct((2, n_heads) + bucket.shape, _F32),
        name="bias_band",
    )(rel_bias, bucket)


def _band_buckets():
    qi = jnp.arange(BLOCK)[:, None]
    kj = jnp.arange(2 * BLOCK)[None, :]
    n = jnp.maximum(qi + BLOCK - kj, 0)
    nf = jnp.maximum(n, 1).astype(_F32)
    large = MAX_EXACT + (jnp.log(nf / MAX_EXACT) / math.log(MAX_DISTANCE / MAX_EXACT)
                         * (NUM_BUCKETS - MAX_EXACT)).astype(jnp.int32)
    large = jnp.minimum(large, NUM_BUCKETS - 1)
    return jnp.where(n < MAX_EXACT, n, large).astype(jnp.int32)


def _swa_kernel(sink_ref, q_ref, kvp_ref, kvc_ref, bias_ref, o_ref):
    n_pairs = q_ref.shape[-1] // PAIR
    n_kv = kvc_ref.shape[-1] // (2 * PAIR)
    pairs_per_kv = n_pairs // n_kv
    col = lax.broadcasted_iota(jnp.int32, (BLOCK, PAIR), 1)
    col2 = lax.broadcasted_iota(jnp.int32, (2 * BLOCK, PAIR), 1)
    head_lanes = (col < HEAD_DIM, col >= HEAD_DIM)
    head_lanes2 = (col2 < HEAD_DIM, col2 >= HEAD_DIM)

    for g in range(n_kv):
        kc = slice(g * PAIR, (g + 1) * PAIR)
        vc = slice((n_kv + g) * PAIR, (n_kv + g + 1) * PAIR)
        k = jnp.concatenate([kvp_ref[:, kc], kvc_ref[:, kc]], axis=0)
        v = jnp.concatenate([kvp_ref[:, vc], kvc_ref[:, vc]], axis=0)
        vs = [jnp.where(m, v, 0) for m in head_lanes2]
        for j in range(pairs_per_kv):
            p = g * pairs_per_kv + j
            q = q_ref[:, p * PAIR:(p + 1) * PAIR]
            out = jnp.zeros((BLOCK, PAIR), _F32)
            for h in range(2):
                head = 2 * p + h
                qh = jnp.where(head_lanes[h], q, 0) * jnp.asarray(SCALE, q.dtype)
                s = _dot_nt(qh, k) + bias_ref[head]
                sink = sink_ref[head]
                m = jnp.maximum(jnp.max(s, axis=-1, keepdims=True), sink)
                e = jnp.exp(s - m)
                denom = jnp.sum(e, axis=-1, keepdims=True) + jnp.exp(sink - m)
                out = out + _dot(e.astype(_BF16), vs[h]) / denom
            o_ref[:, p * PAIR:(p + 1) * PAIR] = out.astype(o_ref.dtype)


def _swa_attention(q, kv, bias_tab, sinks):
    batch, seq, d = q.shape
    n_q = seq // BLOCK
    kvw = kv.shape[-1]
    return pl.pallas_call(
        _swa_kernel,
        grid=(batch, n_q),
        in_specs=[
            pl.BlockSpec(memory_space=pltpu.SMEM),
            pl.BlockSpec((None, BLOCK, d), lambda b, n: (b, n, 0)),
            pl.BlockSpec((None, BLOCK, kvw), lambda b, n: (b, jnp.maximum(n - 1, 0), 0)),
            pl.BlockSpec((None, BLOCK, kvw), lambda b, n: (b, n, 0)),
            pl.BlockSpec((None,) + bias_tab.shape[1:], lambda b, n: (jnp.minimum(n, 1), 0, 0, 0)),
        ],
        out_specs=pl.BlockSpec((None, BLOCK, d), lambda b, n: (b, n, 0)),
        out_shape=jax.ShapeDtypeStruct((batch, seq, d), _BF16),
        compiler_params=pltpu.CompilerParams(
            dimension_semantics=("arbitrary", "arbitrary"), vmem_limit_bytes=VMEM_LIMIT_BYTES),
        name="swa_attention",
    )(sinks, q, kv, kv, bias_tab)


def _duplicate_heads(w, n_heads):
    lead = w.shape[:-1]
    w = w.reshape(lead + (n_heads, 1, HEAD_DIM))
    w = jnp.broadcast_to(w, lead + (n_heads, 2, HEAD_DIM))
    return w.reshape(lead + (n_heads * PAIR,))


def kernel(x, a_norm, a_wqkv, a_wo, kv_norm, w_kv, b_kv, b_norm, b_wq, b_bq, b_sinks,
           b_wo, b_bo, rel_bias, mlp_norm, mlp_up, mlp_down, final_norm):
    batch, seq, d = x.shape
    m = batch * seq
    n_a = a_wqkv.shape[0]
    n_b = b_wq.shape[0]
    depth = n_a + n_b
    n_heads = d // HEAD_DIM
    n_kv = w_kv.shape[1] // (2 * HEAD_DIM)
    assert seq % BLOCK == 0 and m % min(ROW_TILE, m) == 0 and n_heads % (2 * n_kv) == 0

    bf = lambda w: w.astype(_BF16)
    zeros_d = jnp.zeros((d,), _F32)
    h = x.reshape(m, d)

    bias_tab = _bias_band(rel_bias, _band_buckets())
    kv = None
    for layer in range(depth):
        last = layer == depth - 1
        if layer < n_a:
            col_scale = jnp.where(jnp.arange(a_wqkv.shape[2]) < d, SCALE * LOG2_E, 1.0)
            qkv = _norm_matmul(h, a_norm[layer], bf(a_wqkv[layer] * col_scale),
                               jnp.zeros((a_wqkv.shape[2],), _F32))
            attn = _sb_attention(qkv.reshape(batch, seq, -1), batch, seq, n_heads)
            wo, bo = a_wo[layer], zeros_d
        else:
            j = layer - n_a
            if j == 0:
                wk, wv = jnp.split(w_kv, 2, axis=-1)
                bk, bv = jnp.split(b_kv, 2, axis=-1)
                w_dup = jnp.concatenate([_duplicate_heads(wk, n_kv), _duplicate_heads(wv, n_kv)], -1)
                b_dup = jnp.concatenate([_duplicate_heads(bk, n_kv), _duplicate_heads(bv, n_kv)], -1)
                kv = _norm_matmul(h, kv_norm, bf(w_dup), b_dup).reshape(batch, seq, -1)
            q = _norm_matmul(h, b_norm[j], bf(b_wq[j]), b_bq[j])
            attn = _swa_attention(q.reshape(batch, seq, d), kv, bias_tab, b_sinks[j])
            wo, bo = b_wo[j], b_bo[j]
        h = _proj_mlp(h, attn.reshape(m, d), bf(wo), bo, mlp_norm[layer],
                      bf(mlp_up[layer]), bf(mlp_down[layer]), final_norm, last)
    return h.reshape(batch, seq, d)
```

```python
import functools
import math

import jax
import jax.numpy as jnp
from jax import lax
from jax.experimental import pallas as pl
from jax.experimental.pallas import tpu as pltpu

HEAD_DIM = 64
PAIR = 2 * HEAD_DIM
BLOCK = 128
WINDOW = 128
SWA_GROUP = 8
NUM_BUCKETS = 32
MAX_EXACT = NUM_BUCKETS // 2
MAX_DISTANCE = 128
EPS = 1e-5
NEG_INF = -1e30
SCALE = 1.0 / math.sqrt(HEAD_DIM)

LOG2_E = math.log2(math.e)
SB_UNDERFLOW_LOG2 = -151.0
SB_STATIC_BLOCKS = 3
SB_TILES_PER_STEP = 3

SWA_BLOCKS_PER_STEP = 8

ROW_TILE = 512
FF_CHUNK = 1024
PROJ_CHUNK = 512
VMEM_LIMIT_BYTES = 56 * 1024 * 1024

_F32 = jnp.float32
_BF16 = jnp.bfloat16


def _dot(a, b):
    return jnp.dot(a, b, preferred_element_type=_F32)


def _dot_nt(a, b):
    return lax.dot_general(a, b, (((1,), (1,)), ((), ())), preferred_element_type=_F32)


def _rms_unit(x):
    ms = jnp.mean(x * x, axis=-1, keepdims=True)
    return x * lax.rsqrt(ms + EPS)


def _resident(shape):
    return pl.BlockSpec(shape, lambda *_: (0,) * len(shape), pipeline_mode=pl.Buffered(1))


def _project(x_unit, w_ref, b_ref, y_ref):
    xb = x_unit.astype(_BF16)
    n = y_ref.shape[-1]
    for c in range(0, n, PROJ_CHUNK):
        cols = slice(c, min(c + PROJ_CHUNK, n))
        y_ref[:, cols] = (_dot(xb, w_ref[:, cols]) + b_ref[:, cols]).astype(y_ref.dtype)


def _norm_proj_kernel(x_ref, w_ref, b_ref, y_ref):
    _project(_rms_unit(x_ref[...]), w_ref, b_ref, y_ref)


def _norm_proj(x, w, b):
    m, d = x.shape
    n = w.shape[1]
    tm = min(ROW_TILE, m)
    return pl.pallas_call(
        _norm_proj_kernel,
        grid=(m // tm,),
        in_specs=[pl.BlockSpec((tm, d), lambda i: (i, 0)), _resident((d, n)), _resident((1, n))],
        out_specs=pl.BlockSpec((tm, n), lambda i: (i, 0)),
        out_shape=jax.ShapeDtypeStruct((m, n), _BF16),
        compiler_params=pltpu.CompilerParams(
            dimension_semantics=("arbitrary",), vmem_limit_bytes=VMEM_LIMIT_BYTES),
        name="norm_proj",
    )(x, w, b.reshape(1, n))


def _attn_out_and_mlp(h_ref, a_ref, wo_ref, bo_ref, g_ref, wup_ref, wdn_ref):
    h1 = h_ref[...] + _dot(a_ref[...], wo_ref[...]) + bo_ref[...]
    xn = (_rms_unit(h1) * g_ref[...]).astype(_BF16)
    acc = h1
    d_ff = wup_ref.shape[1]
    for c in range(0, d_ff, FF_CHUNK):
        u = jnp.maximum(_dot(xn, wup_ref[:, c:c + FF_CHUNK]), 0.0)
        acc = acc + _dot((u * u).astype(_BF16), wdn_ref[c:c + FF_CHUNK, :])
    return acc


def _mid_layer_kernel(h_ref, a_ref, wo_ref, bo_ref, g_ref, wup_ref, wdn_ref, wn_ref, bn_ref,
                      o_ref, y_ref):
    h2 = _attn_out_and_mlp(h_ref, a_ref, wo_ref, bo_ref, g_ref, wup_ref, wdn_ref)
    o_ref[...] = h2
    _project(_rms_unit(h2), wn_ref, bn_ref, y_ref)


def _last_layer_kernel(h_ref, a_ref, wo_ref, bo_ref, g_ref, wup_ref, wdn_ref, fg_ref, o_ref):
    h2 = _attn_out_and_mlp(h_ref, a_ref, wo_ref, bo_ref, g_ref, wup_ref, wdn_ref)
    o_ref[...] = _rms_unit(h2) * fg_ref[...]


def _layer_tail(h, a, wo, bo, g, wup, wdn, *, next_proj=None, final_gain=None):
    m, d = h.shape
    d_ff = wup.shape[1]
    tm = min(ROW_TILE, m)
    row = lambda i: (i, 0)
    in_specs = [
        pl.BlockSpec((tm, d), row),
        pl.BlockSpec((tm, d), row),
        _resident((d, d)),
        _resident((1, d)),
        _resident((1, d)),
        _resident((d, d_ff)),
        _resident((d_ff, d)),
    ]
    args = [h, a, wo, bo.reshape(1, d), g.reshape(1, d), wup, wdn]
    params = pltpu.CompilerParams(
        dimension_semantics=("arbitrary",), vmem_limit_bytes=VMEM_LIMIT_BYTES)
    if next_proj is None:
        return pl.pallas_call(
            _last_layer_kernel,
            grid=(m // tm,),
            in_specs=in_specs + [_resident((1, d))],
            out_specs=pl.BlockSpec((tm, d), row),
            out_shape=jax.ShapeDtypeStruct((m, d), _F32),
            compiler_params=params,
            name="last_layer",
        )(*args, final_gain.reshape(1, d))
    wn, bn = next_proj
    n = wn.shape[1]
    return pl.pallas_call(
        _mid_layer_kernel,
        grid=(m // tm,),
        in_specs=in_specs + [_resident((d, n)), _resident((1, n))],
        out_specs=[pl.BlockSpec((tm, d), row), pl.BlockSpec((tm, n), row)],
        out_shape=[jax.ShapeDtypeStruct((m, d), _F32), jax.ShapeDtypeStruct((m, n), _BF16)],
        compiler_params=params,
        name="mid_layer",
    )(*args, wn, bn.reshape(1, n))


def _sb_attn_kernel(q_ref, k_ref, v_ref, o_ref, acc_ref, carry_ref):
    n_q = q_ref.shape[0] // BLOCK
    row = lax.broadcasted_iota(jnp.int32, (BLOCK, BLOCK), 0)
    col = lax.broadcasted_iota(jnp.int32, (BLOCK, BLOCK), 1)
    first_head = col < HEAD_DIM
    strictly_causal = col < row
    suffix_and_ones = jnp.concatenate(
        [(row >= col).astype(_BF16), jnp.ones((BLOCK, BLOCK), _BF16)], axis=1)
    suffix_and_ones = jnp.concatenate([suffix_and_ones, suffix_and_ones], axis=0)

    def block_rows(j, n=1):
        start = j * BLOCK
        return pl.ds(start if isinstance(j, int) else pl.multiple_of(start, BLOCK), n * BLOCK)

    def load_q(i):
        q = q_ref[block_rows(i), :]
        return jnp.concatenate([jnp.where(first_head, q, 0), jnp.where(first_head, 0, q)], axis=0)

    def scores(q2, k):
        return _dot_nt(q2, k)

    def suffix_sums(z, diag):
        n_b = z.shape[1] // BLOCK
        neg_abs = lax.bitcast_convert_type(
            lax.bitcast_convert_type(z, jnp.uint32) | jnp.uint32(0x80000000), _F32)
        sp = jnp.maximum(z, 0.0) + jnp.log2(1.0 + jnp.exp2(neg_abs))
        split = []
        for h in range(2):
            for b in range(n_b):
                s = sp[h * BLOCK:(h + 1) * BLOCK, b * BLOCK:(b + 1) * BLOCK]
                if diag and b == n_b - 1:
                    s = jnp.where(strictly_causal, s, 0.0)
                hi = s.astype(_BF16)
                lo = (s - hi.astype(_F32)).astype(_BF16)
                split.append(jnp.concatenate([hi, lo], axis=1))
        return _dot(jnp.concatenate(split, axis=0), suffix_and_ones)

    def weighted_values(z, cs, v, carry_in, diag):
        n_b = z.shape[1] // BLOCK
        ws, carries = [], []
        for h in range(2):
            carry = None if carry_in is None else carry_in[h]
            wb = [None] * n_b
            for b in reversed(range(n_b)):
                r = cs[(h * n_b + b) * BLOCK:(h * n_b + b + 1) * BLOCK]
                arg = z[h * BLOCK:(h + 1) * BLOCK, b * BLOCK:(b + 1) * BLOCK] - r[:, :BLOCK]
                if carry is not None:
                    arg = arg - carry
                w = jnp.exp2(arg)
                if diag and b == n_b - 1:
                    w = jnp.where(strictly_causal, w, 0.0)
                wb[b] = w.astype(_BF16)
                carry = r[:, BLOCK:] if carry is None else carry + r[:, BLOCK:]
            ws.append(jnp.concatenate(wb, axis=1))
            carries.append(carry)
        out = _dot(jnp.concatenate(ws, axis=0), v)
        return jnp.where(first_head, out[:BLOCK], out[BLOCK:]), carries

    def attend(jobs, diag):
        zs = [scores(q2, k) for q2, k, _, _ in jobs]
        css = [suffix_sums(z, diag) for z in zs]
        return [weighted_values(z, cs, v, carry_in, diag)
                for z, cs, (_, _, v, carry_in) in zip(zs, css, jobs)]

    def q_tiles(tiles, n_static):
        q2s = [load_q(i) for i in tiles]
        jobs = []
        for i, q2 in zip(tiles, q2s):
            window = block_rows(i - (n_static - 1), n_static)
            jobs.append((q2, k_ref[window, :], v_ref[window, :], None))
        results = attend(jobs, True)

        if all(isinstance(i, int) and i - n_static < 0 for i in tiles):
            for i, (out, _) in zip(tiles, results):
                o_ref[block_rows(i), :] = out.astype(o_ref.dtype)
            return

        least_mass = []
        for t, (out, carry) in enumerate(results):
            acc_ref[t] = out
            carry_ref[t, 0] = carry[0]
            carry_ref[t, 1] = carry[1]
            least_mass.append(jnp.min(jnp.minimum(carry[0], carry[1])))

        for t, (i, q2) in enumerate(zip(tiles, q2s)):
            def more(state):
                j, least = state
                return jnp.logical_and(j >= 0, least < -SB_UNDERFLOW_LOG2)

            def older_block(state, t=t, q2=q2):
                j, _ = state
                rows = block_rows(j)
                (out, carry), = attend(
                    [(q2, k_ref[rows, :], v_ref[rows, :], [carry_ref[t, 0], carry_ref[t, 1]])], False)
                acc_ref[t] += out
                carry_ref[t, 0] = carry[0]
                carry_ref[t, 1] = carry[1]
                return j - 1, jnp.min(jnp.minimum(carry[0], carry[1]))

            lax.while_loop(more, older_block, (i - n_static, least_mass[t]))
            o_ref[block_rows(i), :] = acc_ref[t].astype(o_ref.dtype)

    n_peel = min(n_q, SB_STATIC_BLOCKS - 1)
    for i in range(n_peel):
        q_tiles([i], i + 1)
    n_groups = (n_q - n_peel) // SB_TILES_PER_STEP
    if n_groups > 0:
        def body(g, _):
            first = n_peel + g * SB_TILES_PER_STEP
            q_tiles([first + t for t in range(SB_TILES_PER_STEP)], SB_STATIC_BLOCKS)
            return 0
        lax.fori_loop(0, n_groups, body, 0)
    for i in range(n_peel + n_groups * SB_TILES_PER_STEP, n_q):
        q_tiles([i], SB_STATIC_BLOCKS)


def _sb_attention(qkv, batch, seq, n_heads):
    n_pairs = n_heads // 2
    blk = (None, seq, PAIR)
    return pl.pallas_call(
        _sb_attn_kernel,
        grid=(batch, n_pairs),
        in_specs=[
            pl.BlockSpec(blk, lambda b, p: (b, 0, p)),
            pl.BlockSpec(blk, lambda b, p: (b, 0, n_pairs + p)),
            pl.BlockSpec(blk, lambda b, p: (b, 0, 2 * n_pairs + p)),
        ],
        out_specs=pl.BlockSpec(blk, lambda b, p: (b, 0, p)),
        out_shape=jax.ShapeDtypeStruct((batch, seq, n_heads * HEAD_DIM), _BF16),
        scratch_shapes=[pltpu.VMEM((SB_TILES_PER_STEP, BLOCK, BLOCK), _F32),
                        pltpu.VMEM((SB_TILES_PER_STEP, 2, BLOCK, BLOCK), _F32)],
        compiler_params=pltpu.CompilerParams(
            dimension_semantics=("arbitrary", "arbitrary"), vmem_limit_bytes=VMEM_LIMIT_BYTES),
        name="sb_attention",
    )(qkv, qkv, qkv)


def _bias_band_kernel(rel_ref, bucket_ref, o_ref):
    first = pl.program_id(0) == 0
    head = pl.program_id(1)
    bucket = bucket_ref[...]
    qi = lax.broadcasted_iota(jnp.int32, bucket.shape, 0)
    kj = lax.broadcasted_iota(jnp.int32, bucket.shape, 1)
    dist = qi + BLOCK - kj
    valid = (dist >= 0) & (dist < WINDOW) & jnp.logical_or(jnp.logical_not(first), kj >= BLOCK)
    bias = jnp.zeros(bucket.shape, _F32)
    for b in range(NUM_BUCKETS):
        bias = jnp.where(bucket == b, rel_ref[b, head] * LOG2_E, bias)
    o_ref[...] = jnp.where(valid, bias, NEG_INF)


def _bias_band(rel_bias, bucket):
    n_heads = rel_bias.shape[1]
    return pl.pallas_call(
        _bias_band_kernel,
        grid=(2, n_heads),
        in_specs=[
            pl.BlockSpec(memory_space=pltpu.SMEM),
            pl.BlockSpec(bucket.shape, lambda t, h: (0, 0)),
        ],
        out_specs=pl.BlockSpec((None, None) + bucket.shape, lambda t, h: (t, h, 0, 0)),
        out_shape=jax.ShapeDtypeStruct((2, n_heads) + bucket.shape, _F32),
        name="bias_band",
    )(rel_bias, bucket)


def _band_buckets():
    qi = jnp.arange(BLOCK)[:, None]
    kj = jnp.arange(2 * BLOCK)[None, :]
    n = jnp.maximum(qi + BLOCK - kj, 0)
    nf = jnp.maximum(n, 1).astype(_F32)
    large = MAX_EXACT + (jnp.log(nf / MAX_EXACT) / math.log(MAX_DISTANCE / MAX_EXACT)
                         * (NUM_BUCKETS - MAX_EXACT)).astype(jnp.int32)
    large = jnp.minimum(large, NUM_BUCKETS - 1)
    return jnp.where(n < MAX_EXACT, n, large).astype(jnp.int32)


def _swa_kernel(sink_ref, q_ref, kvp_ref, kvc_ref, bias_ref, o_ref):
    n_pairs = q_ref.shape[-1] // PAIR
    n_kv = kvc_ref.shape[-1] // (2 * PAIR)
    pairs_per_kv = n_pairs // n_kv
    blocks = q_ref.shape[0] // BLOCK
    first_head = lax.broadcasted_iota(jnp.int32, (BLOCK, PAIR), 1) < HEAD_DIM
    ones = jnp.ones((2 * BLOCK, PAIR), _BF16)
    first_table = jnp.where(pl.program_id(1) == 0, 0, 1)

    for t in range(blocks):
        rows = slice(t * BLOCK, (t + 1) * BLOCK)
        table = first_table if t == 0 else 1
        for g in range(n_kv):
            kc = slice(g * PAIR, (g + 1) * PAIR)
            vc = slice((n_kv + g) * PAIR, (n_kv + g + 1) * PAIR)
            if t == 0:
                k = jnp.concatenate([kvp_ref[:, kc], kvc_ref[rows, kc]], axis=0)
                v = jnp.concatenate([kvp_ref[:, vc], kvc_ref[rows, vc]], axis=0)
            else:
                band = slice((t - 1) * BLOCK, (t + 1) * BLOCK)
                k = kvc_ref[band, kc]
                v = kvc_ref[band, vc]
            v_and_ones = jnp.concatenate([v, ones], axis=1)
            qs = []
            for j in range(pairs_per_kv):
                p = g * pairs_per_kv + j
                q = q_ref[rows, p * PAIR:(p + 1) * PAIR]
                qs += [jnp.where(first_head, q, 0), jnp.where(first_head, 0, q)]
            s = _dot_nt(jnp.concatenate(qs, axis=0), k)
            es, sink_terms = [], []
            for hh in range(2 * pairs_per_kv):
                head = 2 * g * pairs_per_kv + hh
                sh = s[hh * BLOCK:(hh + 1) * BLOCK] + bias_ref[table, head]
                sink = sink_ref[head] * LOG2_E
                m = jnp.maximum(jnp.max(sh, axis=-1, keepdims=True), sink)
                es.append(jnp.exp2(sh - m).astype(_BF16))
                sink_terms.append(jnp.exp2(sink - m))
            pv = _dot(jnp.concatenate(es, axis=0), v_and_ones)
            for j in range(pairs_per_kv):
                p = g * pairs_per_kv + j
                halves = []
                for h in range(2):
                    r = pv[(2 * j + h) * BLOCK:(2 * j + h + 1) * BLOCK]
                    halves.append(r[:, :PAIR] / (r[:, PAIR:] + sink_terms[2 * j + h]))
                o_ref[rows, p * PAIR:(p + 1) * PAIR] = (
                    jnp.where(first_head, halves[0], halves[1]).astype(o_ref.dtype))


def _swa_attention(q_src, kv_src, kv_col, d, kvw, bias_tab, sinks):
    batch, seq, _ = q_src.shape
    blocks = math.gcd(seq // BLOCK, SWA_BLOCKS_PER_STEP)
    tq = blocks * BLOCK
    return pl.pallas_call(
        _swa_kernel,
        grid=(batch, seq // tq),
        in_specs=[
            pl.BlockSpec(memory_space=pltpu.SMEM),
            pl.BlockSpec((None, tq, d), lambda b, n: (b, n, 0)),
            pl.BlockSpec((None, BLOCK, kvw), lambda b, n: (b, jnp.maximum(n * blocks - 1, 0), kv_col)),
            pl.BlockSpec((None, tq, kvw), lambda b, n: (b, n, kv_col)),
            _resident(bias_tab.shape),
        ],
        out_specs=pl.BlockSpec((None, tq, d), lambda b, n: (b, n, 0)),
        out_shape=jax.ShapeDtypeStruct((batch, seq, d), _BF16),
        compiler_params=pltpu.CompilerParams(
            dimension_semantics=("arbitrary", "arbitrary"), vmem_limit_bytes=VMEM_LIMIT_BYTES),
        name="swa_attention",
    )(sinks, q_src, kv_src, kv_src, bias_tab)


def _duplicate_heads(w, n_heads):
    lead = w.shape[:-1]
    w = w.reshape(lead + (n_heads, 1, HEAD_DIM))
    w = jnp.broadcast_to(w, lead + (n_heads, 2, HEAD_DIM))
    return w.reshape(lead + (n_heads * PAIR,))


def kernel(x, a_norm, a_wqkv, a_wo, kv_norm, w_kv, b_kv, b_norm, b_wq, b_bq, b_sinks,
           b_wo, b_bo, rel_bias, mlp_norm, mlp_up, mlp_down, final_norm):
    batch, seq, d = x.shape
    m = batch * seq
    n_a = a_wqkv.shape[0]
    n_b = b_wq.shape[0]
    depth = n_a + n_b
    n_heads = d // HEAD_DIM
    n_kv = w_kv.shape[1] // (2 * HEAD_DIM)
    assert seq % BLOCK == 0 and m % min(ROW_TILE, m) == 0 and n_heads % (2 * n_kv) == 0

    bf = lambda w: w.astype(_BF16)
    kvw = 2 * n_kv * PAIR
    q_scale = SCALE * LOG2_E

    def attention_projection(layer):
        if layer < n_a:
            col_scale = jnp.where(jnp.arange(a_wqkv.shape[2]) < d, q_scale, 1.0)
            w = a_norm[layer][:, None] * a_wqkv[layer] * col_scale
            return bf(w), jnp.zeros((w.shape[1],), _F32)
        j = layer - n_a
        w = b_norm[j][:, None] * b_wq[j] * q_scale
        b = b_bq[j] * q_scale
        if j == 0:
            wk, wv = jnp.split(w_kv, 2, axis=-1)
            bk, bv = jnp.split(b_kv, 2, axis=-1)
            w_dup = jnp.concatenate([_duplicate_heads(wk, n_kv), _duplicate_heads(wv, n_kv)], -1)
            b_dup = jnp.concatenate([_duplicate_heads(bk, n_kv), _duplicate_heads(bv, n_kv)], -1)
            w = jnp.concatenate([w, kv_norm[:, None] * w_dup], axis=-1)
            b = jnp.concatenate([b, b_dup], axis=-1)
        return bf(w), b

    bias_tab = _bias_band(rel_bias, _band_buckets())
    h = x.reshape(m, d)
    y = _norm_proj(h, *attention_projection(0)).reshape(batch, seq, -1)
    kv_src = None
    for layer in range(depth):
        if layer < n_a:
            attn = _sb_attention(y, batch, seq, n_heads)
            wo, bo = a_wo[layer], jnp.zeros((d,), _F32)
        else:
            j = layer - n_a
            if j == 0:
                kv_src = y
            attn = _swa_attention(y, kv_src, d // kvw, d, kvw, bias_tab, b_sinks[j])
            wo, bo = b_wo[j], b_bo[j]
        tail = functools.partial(_layer_tail, h, attn.reshape(m, d), bf(wo), bo, mlp_norm[layer],
                                 bf(mlp_up[layer]), bf(mlp_down[layer]))
        if layer == depth - 1:
            return tail(final_gain=final_norm).reshape(batch, seq, d)
        h, y = tail(next_proj=attention_projection(layer + 1))
        y = y.reshape(batch, seq, -1)
```

```python
import functools
import math

import jax
import jax.numpy as jnp
from jax import lax
from jax.experimental import pallas as pl
from jax.experimental.pallas import tpu as pltpu

HEAD_DIM = 64
PAIR = 2 * HEAD_DIM
BLOCK = 128
WINDOW = 128
SWA_GROUP = 8
NUM_BUCKETS = 32
MAX_EXACT = NUM_BUCKETS // 2
MAX_DISTANCE = 128
EPS = 1e-5
NEG_INF = -1e30
SCALE = 1.0 / math.sqrt(HEAD_DIM)

LOG2_E = math.log2(math.e)
SB_UNDERFLOW_LOG2 = -151.0
SB_STATIC_BLOCKS = 3
SB_TILES_PER_STEP = 2

SWA_BLOCKS_PER_STEP = 8

ROW_TILE = 512
FF_CHUNK = 1024
PROJ_CHUNK = 512
VMEM_LIMIT_BYTES = 56 * 1024 * 1024

_F32 = jnp.float32
_BF16 = jnp.bfloat16


def _dot(a, b):
    return jnp.dot(a, b, preferred_element_type=_F32)


def _dot_nt(a, b):
    return lax.dot_general(a, b, (((1,), (1,)), ((), ())), preferred_element_type=_F32)


def _rms_unit(x):
    ms = jnp.mean(x * x, axis=-1, keepdims=True)
    return x * lax.rsqrt(ms + EPS)


def _resident(shape):
    return pl.BlockSpec(shape, lambda *_: (0,) * len(shape), pipeline_mode=pl.Buffered(1))


def _project(x_unit, w_ref, b_ref, y_ref):
    xb = x_unit.astype(_BF16)
    n = y_ref.shape[-1]
    for c in range(0, n, PROJ_CHUNK):
        cols = slice(c, min(c + PROJ_CHUNK, n))
        y_ref[:, cols] = (_dot(xb, w_ref[:, cols]) + b_ref[:, cols]).astype(y_ref.dtype)


def _norm_proj_kernel(x_ref, w_ref, b_ref, y_ref):
    _project(_rms_unit(x_ref[...]), w_ref, b_ref, y_ref)


def _norm_proj(x, w, b):
    m, d = x.shape
    n = w.shape[1]
    tm = min(ROW_TILE, m)
    return pl.pallas_call(
        _norm_proj_kernel,
        grid=(m // tm,),
        in_specs=[pl.BlockSpec((tm, d), lambda i: (i, 0)), _resident((d, n)), _resident((1, n))],
        out_specs=pl.BlockSpec((tm, n), lambda i: (i, 0)),
        out_shape=jax.ShapeDtypeStruct((m, n), _BF16),
        compiler_params=pltpu.CompilerParams(
            dimension_semantics=("arbitrary",), vmem_limit_bytes=VMEM_LIMIT_BYTES),
        name="norm_proj",
    )(x, w, b.reshape(1, n))


def _attn_out_and_mlp(h_ref, a_ref, wo_ref, bo_ref, g_ref, wup_ref, wdn_ref):
    h1 = h_ref[...] + _dot(a_ref[...], wo_ref[...]) + bo_ref[...]
    xn = (_rms_unit(h1) * g_ref[...]).astype(_BF16)
    acc = h1
    d_ff = wup_ref.shape[1]
    for c in range(0, d_ff, FF_CHUNK):
        u = jnp.maximum(_dot(xn, wup_ref[:, c:c + FF_CHUNK]), 0.0)
        acc = acc + _dot((u * u).astype(_BF16), wdn_ref[c:c + FF_CHUNK, :])
    return acc


def _mid_layer_kernel(h_ref, a_ref, wo_ref, bo_ref, g_ref, wup_ref, wdn_ref, wn_ref, bn_ref,
                      o_ref, y_ref):
    h2 = _attn_out_and_mlp(h_ref, a_ref, wo_ref, bo_ref, g_ref, wup_ref, wdn_ref)
    o_ref[...] = h2
    _project(_rms_unit(h2), wn_ref, bn_ref, y_ref)


def _last_layer_kernel(h_ref, a_ref, wo_ref, bo_ref, g_ref, wup_ref, wdn_ref, fg_ref, o_ref):
    h2 = _attn_out_and_mlp(h_ref, a_ref, wo_ref, bo_ref, g_ref, wup_ref, wdn_ref)
    o_ref[...] = _rms_unit(h2) * fg_ref[...]


def _layer_tail(h, a, wo, bo, g, wup, wdn, *, next_proj=None, final_gain=None):
    m, d = h.shape
    d_ff = wup.shape[1]
    tm = min(ROW_TILE, m)
    row = lambda i: (i, 0)
    in_specs = [
        pl.BlockSpec((tm, d), row),
        pl.BlockSpec((tm, d), row),
        _resident((d, d)),
        _resident((1, d)),
        _resident((1, d)),
        _resident((d, d_ff)),
        _resident((d_ff, d)),
    ]
    args = [h, a, wo, bo.reshape(1, d), g.reshape(1, d), wup, wdn]
    params = pltpu.CompilerParams(
        dimension_semantics=("arbitrary",), vmem_limit_bytes=VMEM_LIMIT_BYTES)
    if next_proj is None:
        return pl.pallas_call(
            _last_layer_kernel,
            grid=(m // tm,),
            in_specs=in_specs + [_resident((1, d))],
            out_specs=pl.BlockSpec((tm, d), row),
            out_shape=jax.ShapeDtypeStruct((m, d), _F32),
            compiler_params=params,
            name="last_layer",
        )(*args, final_gain.reshape(1, d))
    wn, bn = next_proj
    n = wn.shape[1]
    return pl.pallas_call(
        _mid_layer_kernel,
        grid=(m // tm,),
        in_specs=in_specs + [_resident((d, n)), _resident((1, n))],
        out_specs=[pl.BlockSpec((tm, d), row), pl.BlockSpec((tm, n), row)],
        out_shape=[jax.ShapeDtypeStruct((m, d), _F32), jax.ShapeDtypeStruct((m, n), _BF16)],
        compiler_params=params,
        name="mid_layer",
    )(*args, wn, bn.reshape(1, n))


def _sb_attn_kernel(q_ref, k_ref, v_ref, o_ref, acc_ref, carry_ref, least_ref):
    n_q = q_ref.shape[0] // BLOCK
    row = lax.broadcasted_iota(jnp.int32, (BLOCK, BLOCK), 0)
    col = lax.broadcasted_iota(jnp.int32, (BLOCK, BLOCK), 1)
    first_head = col < HEAD_DIM
    strictly_causal = col < row
    suffix_and_ones = jnp.concatenate(
        [(row >= col).astype(_BF16), jnp.ones((BLOCK, BLOCK), _BF16)], axis=1)
    suffix_and_ones = jnp.concatenate([suffix_and_ones, suffix_and_ones], axis=0)

    def block_rows(j, n=1):
        start = j * BLOCK
        return pl.ds(start if isinstance(j, int) else pl.multiple_of(start, BLOCK), n * BLOCK)

    def load_q(i):
        q = q_ref[block_rows(i), :]
        return jnp.concatenate([jnp.where(first_head, q, 0), jnp.where(first_head, 0, q)], axis=0)

    def scores(q2, k):
        return _dot_nt(q2, k)

    def suffix_sums(z, diag):
        n_b = z.shape[1] // BLOCK
        neg_abs = lax.bitcast_convert_type(
            lax.bitcast_convert_type(z, jnp.uint32) | jnp.uint32(0x80000000), _F32)
        sp = jnp.maximum(z, 0.0) + jnp.log2(1.0 + jnp.exp2(neg_abs))
        split = []
        for h in range(2):
            for b in range(n_b):
                s = sp[h * BLOCK:(h + 1) * BLOCK, b * BLOCK:(b + 1) * BLOCK]
                if diag and b == n_b - 1:
                    s = jnp.where(strictly_causal, s, 0.0)
                hi = s.astype(_BF16)
                lo = (s - hi.astype(_F32)).astype(_BF16)
                split.append(jnp.concatenate([hi, lo], axis=1))
        return _dot(jnp.concatenate(split, axis=0), suffix_and_ones)

    def weighted_values(z, cs, v, carry_in, diag):
        n_b = z.shape[1] // BLOCK
        ws, carries = [], []
        for h in range(2):
            carry = None if carry_in is None else carry_in[h]
            wb = [None] * n_b
            for b in reversed(range(n_b)):
                r = cs[(h * n_b + b) * BLOCK:(h * n_b + b + 1) * BLOCK]
                arg = z[h * BLOCK:(h + 1) * BLOCK, b * BLOCK:(b + 1) * BLOCK] - r[:, :BLOCK]
                if carry is not None:
                    arg = arg - carry
                w = jnp.exp2(arg)
                if diag and b == n_b - 1:
                    w = jnp.where(strictly_causal, w, 0.0)
                wb[b] = w.astype(_BF16)
                carry = r[:, BLOCK:] if carry is None else carry + r[:, BLOCK:]
            ws.append(jnp.concatenate(wb, axis=1))
            carries.append(carry)
        out = _dot(jnp.concatenate(ws, axis=0), v)
        return jnp.where(first_head, out[:BLOCK], out[BLOCK:]), carries

    def window_scores(i):
        n_static = min(i + 1, SB_STATIC_BLOCKS)
        return scores(load_q(i), k_ref[block_rows(i - (n_static - 1), n_static), :])

    groups = [list(range(s, min(s + SB_TILES_PER_STEP, n_q))) for s in range(0, n_q, SB_TILES_PER_STEP)]
    zs = [window_scores(i) for i in groups[0]]
    for g, tiles in enumerate(groups):
        zs_next = [window_scores(i) for i in groups[g + 1]] if g + 1 < len(groups) else None
        css = [suffix_sums(z, True) for z in zs]
        for i, z, cs in zip(tiles, zs, css):
            n_static = z.shape[1] // BLOCK
            v = v_ref[block_rows(i - (n_static - 1), n_static), :]
            out, carry = weighted_values(z, cs, v, None, True)
            o_ref[block_rows(i), :] = out.astype(o_ref.dtype)
            if i - n_static >= 0:
                acc_ref[i] = out
                carry_ref[i, 0] = carry[0]
                carry_ref[i, 1] = carry[1]
                least_ref[i] = jnp.min(jnp.minimum(carry[0], carry[1]))
        zs = zs_next

    def finish_tile(i, _):
        def more(state):
            j, least = state
            return jnp.logical_and(j >= 0, least < -SB_UNDERFLOW_LOG2)

        def older_block(state):
            j, _ = state
            rows = block_rows(j)
            z = scores(load_q(i), k_ref[rows, :])
            out, carry = weighted_values(z, suffix_sums(z, False), v_ref[rows, :],
                                         [carry_ref[i, 0], carry_ref[i, 1]], False)
            acc_ref[i] += out
            carry_ref[i, 0] = carry[0]
            carry_ref[i, 1] = carry[1]
            return j - 1, jnp.min(jnp.minimum(carry[0], carry[1]))

        @pl.when(least_ref[i] < -SB_UNDERFLOW_LOG2)
        def _():
            lax.while_loop(more, older_block, (i - SB_STATIC_BLOCKS, least_ref[i]))
            o_ref[block_rows(i), :] = acc_ref[i].astype(o_ref.dtype)
        return 0

    if n_q > SB_STATIC_BLOCKS:
        lax.fori_loop(SB_STATIC_BLOCKS, n_q, finish_tile, 0)


def _sb_attention(qkv, batch, seq, n_heads):
    n_pairs = n_heads // 2
    blk = (None, seq, PAIR)
    return pl.pallas_call(
        _sb_attn_kernel,
        grid=(batch, n_pairs),
        in_specs=[
            pl.BlockSpec(blk, lambda b, p: (b, 0, p)),
            pl.BlockSpec(blk, lambda b, p: (b, 0, n_pairs + p)),
            pl.BlockSpec(blk, lambda b, p: (b, 0, 2 * n_pairs + p)),
        ],
        out_specs=pl.BlockSpec(blk, lambda b, p: (b, 0, p)),
        out_shape=jax.ShapeDtypeStruct((batch, seq, n_heads * HEAD_DIM), _BF16),
        scratch_shapes=[pltpu.VMEM((seq // BLOCK, BLOCK, BLOCK), _F32),
                        pltpu.VMEM((seq // BLOCK, 2, BLOCK, BLOCK), _F32),
                        pltpu.SMEM((seq // BLOCK,), _F32)],
        compiler_params=pltpu.CompilerParams(
            dimension_semantics=("arbitrary", "arbitrary"), vmem_limit_bytes=VMEM_LIMIT_BYTES),
        name="sb_attention",
    )(qkv, qkv, qkv)


def _bias_band_kernel(rel_ref, bucket_ref, o_ref):
    first = pl.program_id(0) == 0
    head = pl.program_id(1)
    bucket = bucket_ref[...]
    qi = lax.broadcasted_iota(jnp.int32, bucket.shape, 0)
    kj = lax.broadcasted_iota(jnp.int32, bucket.shape, 1)
    dist = qi + BLOCK - kj
    valid = (dist >= 0) & (dist < WINDOW) & jnp.logical_or(jnp.logical_not(first), kj >= BLOCK)
    bias = jnp.zeros(bucket.shape, _F32)
    for b in range(NUM_BUCKETS):
        bias = jnp.where(bucket == b, rel_ref[b, head] * LOG2_E, bias)
    o_ref[...] = jnp.where(valid, bias, NEG_INF)


def _bias_band(rel_bias, bucket):
    n_heads = rel_bias.shape[1]
    return pl.pallas_call(
        _bias_band_kernel,
        grid=(2, n_heads),
        in_specs=[
            pl.BlockSpec(memory_space=pltpu.SMEM),
            pl.BlockSpec(bucket.shape, lambda t, h: (0, 0)),
        ],
        out_specs=pl.BlockSpec((None, None) + bucket.shape, lambda t, h: (t, h, 0, 0)),
        out_shape=jax.ShapeDtypeStruct((2, n_heads) + bucket.shape, _F32),
        name="bias_band",
    )(rel_bias, bucket)


def _band_buckets():
    qi = jnp.arange(BLOCK)[:, None]
    kj = jnp.arange(2 * BLOCK)[None, :]
    n = jnp.maximum(qi + BLOCK - kj, 0)
    nf = jnp.maximum(n, 1).astype(_F32)
    large = MAX_EXACT + (jnp.log(nf / MAX_EXACT) / math.log(MAX_DISTANCE / MAX_EXACT)
                         * (NUM_BUCKETS - MAX_EXACT)).astype(jnp.int32)
    large = jnp.minimum(large, NUM_BUCKETS - 1)
    return jnp.where(n < MAX_EXACT, n, large).astype(jnp.int32)


def _swa_kernel(sink_ref, q_ref, kvp_ref, kvc_ref, bias_ref, o_ref):
    n_pairs = q_ref.shape[-1] // PAIR
    n_kv = kvc_ref.shape[-1] // (2 * PAIR)
    pairs_per_kv = n_pairs // n_kv
    blocks = q_ref.shape[0] // BLOCK
    first_head = lax.broadcasted_iota(jnp.int32, (BLOCK, PAIR), 1) < HEAD_DIM
    ones = jnp.ones((2 * BLOCK, PAIR), _BF16)
    first_table = jnp.where(pl.program_id(1) == 0, 0, 1)

    for t in range(blocks):
        rows = slice(t * BLOCK, (t + 1) * BLOCK)
        table = first_table if t == 0 else 1
        for g in range(n_kv):
            kc = slice(g * PAIR, (g + 1) * PAIR)
            vc = slice((n_kv + g) * PAIR, (n_kv + g + 1) * PAIR)
            if t == 0:
                k = jnp.concatenate([kvp_ref[:, kc], kvc_ref[rows, kc]], axis=0)
                v = jnp.concatenate([kvp_ref[:, vc], kvc_ref[rows, vc]], axis=0)
            else:
                band = slice((t - 1) * BLOCK, (t + 1) * BLOCK)
                k = kvc_ref[band, kc]
                v = kvc_ref[band, vc]
            v_and_ones = jnp.concatenate([v, ones], axis=1)
            qs = []
            for j in range(pairs_per_kv):
                p = g * pairs_per_kv + j
                q = q_ref[rows, p * PAIR:(p + 1) * PAIR]
                qs += [jnp.where(first_head, q, 0), jnp.where(first_head, 0, q)]
            s = _dot_nt(jnp.concatenate(qs, axis=0), k)
            es, sink_terms = [], []
            for hh in range(2 * pairs_per_kv):
                head = 2 * g * pairs_per_kv + hh
                sh = s[hh * BLOCK:(hh + 1) * BLOCK] + bias_ref[table, head]
                sink = sink_ref[head] * LOG2_E
                m = jnp.maximum(jnp.max(sh, axis=-1, keepdims=True), sink)
                es.append(jnp.exp2(sh - m).astype(_BF16))
                sink_terms.append(jnp.exp2(sink - m))
            pv = _dot(jnp.concatenate(es, axis=0), v_and_ones)
            for j in range(pairs_per_kv):
                p = g * pairs_per_kv + j
                halves = []
                for h in range(2):
                    r = pv[(2 * j + h) * BLOCK:(2 * j + h + 1) * BLOCK]
                    halves.append(r[:, :PAIR] / (r[:, PAIR:] + sink_terms[2 * j + h]))
                o_ref[rows, p * PAIR:(p + 1) * PAIR] = (
                    jnp.where(first_head, halves[0], halves[1]).astype(o_ref.dtype))


def _swa_attention(q_src, kv_src, kv_col, d, kvw, bias_tab, sinks):
    batch, seq, _ = q_src.shape
    blocks = math.gcd(seq // BLOCK, SWA_BLOCKS_PER_STEP)
    tq = blocks * BLOCK
    return pl.pallas_call(
        _swa_kernel,
        grid=(batch, seq // tq),
        in_specs=[
            pl.BlockSpec(memory_space=pltpu.SMEM),
            pl.BlockSpec((None, tq, d), lambda b, n: (b, n, 0)),
            pl.BlockSpec((None, BLOCK, kvw), lambda b, n: (b, jnp.maximum(n * blocks - 1, 0), kv_col)),
            pl.BlockSpec((None, tq, kvw), lambda b, n: (b, n, kv_col)),
            _resident(bias_tab.shape),
        ],
        out_specs=pl.BlockSpec((None, tq, d), lambda b, n: (b, n, 0)),
        out_shape=jax.ShapeDtypeStruct((batch, seq, d), _BF16),
        compiler_params=pltpu.CompilerParams(
            dimension_semantics=("arbitrary", "arbitrary"), vmem_limit_bytes=VMEM_LIMIT_BYTES),
        name="swa_attention",
    )(sinks, q_src, kv_src, kv_src, bias_tab)


def _duplicate_heads(w, n_heads):
    lead = w.shape[:-1]
    w = w.reshape(lead + (n_heads, 1, HEAD_DIM))
    w = jnp.broadcast_to(w, lead + (n_heads, 2, HEAD_DIM))
    return w.reshape(lead + (n_heads * PAIR,))


def kernel(x, a_norm, a_wqkv, a_wo, kv_norm, w_kv, b_kv, b_norm, b_wq, b_bq, b_sinks,
           b_wo, b_bo, rel_bias, mlp_norm, mlp_up, mlp_down, final_norm):
    batch, seq, d = x.shape
    m = batch * seq
    n_a = a_wqkv.shape[0]
    n_b = b_wq.shape[0]
    depth = n_a + n_b
    n_heads = d // HEAD_DIM
    n_kv = w_kv.shape[1] // (2 * HEAD_DIM)
    assert seq % BLOCK == 0 and m % min(ROW_TILE, m) == 0 and n_heads % (2 * n_kv) == 0

    bf = lambda w: w.astype(_BF16)
    kvw = 2 * n_kv * PAIR
    q_scale = SCALE * LOG2_E

    def attention_projection(layer):
        if layer < n_a:
            col_scale = jnp.where(jnp.arange(a_wqkv.shape[2]) < d, q_scale, 1.0)
            w = a_norm[layer][:, None] * a_wqkv[layer] * col_scale
            return bf(w), jnp.zeros((w.shape[1],), _F32)
        j = layer - n_a
        w = b_norm[j][:, None] * b_wq[j] * q_scale
        b = b_bq[j] * q_scale
        if j == 0:
            wk, wv = jnp.split(w_kv, 2, axis=-1)
            bk, bv = jnp.split(b_kv, 2, axis=-1)
            w_dup = jnp.concatenate([_duplicate_heads(wk, n_kv), _duplicate_heads(wv, n_kv)], -1)
            b_dup = jnp.concatenate([_duplicate_heads(bk, n_kv), _duplicate_heads(bv, n_kv)], -1)
            w = jnp.concatenate([w, kv_norm[:, None] * w_dup], axis=-1)
            b = jnp.concatenate([b, b_dup], axis=-1)
        return bf(w), b

    bias_tab = _bias_band(rel_bias, _band_buckets())
    h = x.reshape(m, d)
    y = _norm_proj(h, *attention_projection(0)).reshape(batch, seq, -1)
    kv_src = None
    for layer in range(depth):
        if layer < n_a:
            attn = _sb_attention(y, batch, seq, n_heads)
            wo, bo = a_wo[layer], jnp.zeros((d,), _F32)
        else:
            j = layer - n_a
            if j == 0:
                kv_src = y
            attn = _swa_attention(y, kv_src, d // kvw, d, kvw, bias_tab, b_sinks[j])
            wo, bo = b_wo[j], b_bo[j]
        tail = functools.partial(_layer_tail, h, attn.reshape(m, d), bf(wo), bo, mlp_norm[layer],
                                 bf(mlp_up[layer]), bf(mlp_down[layer]))
        if layer == depth - 1:
            return tail(final_gain=final_norm).reshape(batch, seq, d)
        h, y = tail(next_proj=attention_projection(layer + 1))
        y = y.reshape(batch, seq, -1)
```

```python
import functools
import math

import jax
import jax.numpy as jnp
from jax import lax
from jax.experimental import pallas as pl
from jax.experimental.pallas import tpu as pltpu

HEAD_DIM = 64
PAIR = 2 * HEAD_DIM
BLOCK = 128
WINDOW = 128
SWA_GROUP = 8
NUM_BUCKETS = 32
MAX_EXACT = NUM_BUCKETS // 2
MAX_DISTANCE = 128
EPS = 1e-5
NEG_INF = -1e30
SCALE = 1.0 / math.sqrt(HEAD_DIM)

LOG2_E = math.log2(math.e)
SB_UNDERFLOW_LOG2 = -151.0
SB_TILE = 64
SB_WINDOW = 256
SB_TILES_PER_STEP = 8

SWA_BLOCKS_PER_STEP = 8

ROW_TILE = 512
FF_CHUNK = 1024
PROJ_CHUNK = 512
SUB_ROWS = 256
VMEM_LIMIT_BYTES = 56 * 1024 * 1024

_F32 = jnp.float32
_BF16 = jnp.bfloat16


def _dot(a, b):
    return jnp.dot(a, b, preferred_element_type=_F32)


def _dot_nt(a, b):
    return lax.dot_general(a, b, (((1,), (1,)), ((), ())), preferred_element_type=_F32)


def _rms_unit(x):
    ms = jnp.mean(x * x, axis=-1, keepdims=True)
    return x * lax.rsqrt(ms + EPS)


def _resident(shape):
    return pl.BlockSpec(shape, lambda *_: (0,) * len(shape), pipeline_mode=pl.Buffered(1))


def _project(x_unit, w_ref, b_ref, y_ref):
    xb = x_unit.astype(_BF16)
    n = y_ref.shape[-1]
    for c in range(0, n, PROJ_CHUNK):
        cols = slice(c, min(c + PROJ_CHUNK, n))
        y_ref[:, cols] = (_dot(xb, w_ref[:, cols]) + b_ref[:, cols]).astype(y_ref.dtype)


def _norm_proj_kernel(x_ref, w_ref, b_ref, y_ref):
    _project(_rms_unit(x_ref[...]), w_ref, b_ref, y_ref)


def _norm_proj(x, w, b):
    m, d = x.shape
    n = w.shape[1]
    tm = min(ROW_TILE, m)
    return pl.pallas_call(
        _norm_proj_kernel,
        grid=(m // tm,),
        in_specs=[pl.BlockSpec((tm, d), lambda i: (i, 0)), _resident((d, n)), _resident((1, n))],
        out_specs=pl.BlockSpec((tm, n), lambda i: (i, 0)),
        out_shape=jax.ShapeDtypeStruct((m, n), _BF16),
        compiler_params=pltpu.CompilerParams(
            dimension_semantics=("arbitrary",), vmem_limit_bytes=VMEM_LIMIT_BYTES),
        name="norm_proj",
    )(x, w, b.reshape(1, n))


def _attn_out_and_mlp(h_ref, a_ref, wo_ref, bo_ref, g_ref, wup_ref, wdn_ref):
    tm = h_ref.shape[0]
    subs = [slice(r, r + min(SUB_ROWS, tm)) for r in range(0, tm, min(SUB_ROWS, tm))]
    h1 = [h_ref[s, :] + _dot(a_ref[s, :], wo_ref[...]) + bo_ref[...] for s in subs]
    xn = [(_rms_unit(x) * g_ref[...]).astype(_BF16) for x in h1]
    acc = h1
    d_ff = wup_ref.shape[1]
    for c in range(0, d_ff, FF_CHUNK):
        u = [jnp.maximum(_dot(x, wup_ref[:, c:c + FF_CHUNK]), 0.0) for x in xn]
        acc = [a + _dot((v * v).astype(_BF16), wdn_ref[c:c + FF_CHUNK, :]) for a, v in zip(acc, u)]
    return subs, acc


def _mid_layer_kernel(h_ref, a_ref, wo_ref, bo_ref, g_ref, wup_ref, wdn_ref, wn_ref, bn_ref,
                      o_ref, y_ref):
    subs, h2 = _attn_out_and_mlp(h_ref, a_ref, wo_ref, bo_ref, g_ref, wup_ref, wdn_ref)
    for s, x in zip(subs, h2):
        o_ref[s, :] = x
    xb = [_rms_unit(x).astype(_BF16) for x in h2]
    n = y_ref.shape[-1]
    for c in range(0, n, PROJ_CHUNK):
        cols = slice(c, min(c + PROJ_CHUNK, n))
        for s, x in zip(subs, xb):
            y_ref[s, cols] = (_dot(x, wn_ref[:, cols]) + bn_ref[:, cols]).astype(y_ref.dtype)


def _last_layer_kernel(h_ref, a_ref, wo_ref, bo_ref, g_ref, wup_ref, wdn_ref, fg_ref, o_ref):
    subs, h2 = _attn_out_and_mlp(h_ref, a_ref, wo_ref, bo_ref, g_ref, wup_ref, wdn_ref)
    for s, x in zip(subs, h2):
        o_ref[s, :] = _rms_unit(x) * fg_ref[...]


def _layer_tail(h, a, wo, bo, g, wup, wdn, *, next_proj=None, final_gain=None):
    m, d = h.shape
    d_ff = wup.shape[1]
    tm = min(ROW_TILE, m)
    row = lambda i: (i, 0)
    in_specs = [
        pl.BlockSpec((tm, d), row),
        pl.BlockSpec((tm, d), row),
        _resident((d, d)),
        _resident((1, d)),
        _resident((1, d)),
        _resident((d, d_ff)),
        _resident((d_ff, d)),
    ]
    args = [h, a, wo, bo.reshape(1, d), g.reshape(1, d), wup, wdn]
    params = pltpu.CompilerParams(
        dimension_semantics=("arbitrary",), vmem_limit_bytes=VMEM_LIMIT_BYTES)
    if next_proj is None:
        return pl.pallas_call(
            _last_layer_kernel,
            grid=(m // tm,),
            in_specs=in_specs + [_resident((1, d))],
            out_specs=pl.BlockSpec((tm, d), row),
            out_shape=jax.ShapeDtypeStruct((m, d), _F32),
            compiler_params=params,
            name="last_layer",
        )(*args, final_gain.reshape(1, d))
    wn, bn = next_proj
    n = wn.shape[1]
    return pl.pallas_call(
        _mid_layer_kernel,
        grid=(m // tm,),
        in_specs=in_specs + [_resident((d, n)), _resident((1, n))],
        out_specs=[pl.BlockSpec((tm, d), row), pl.BlockSpec((tm, n), row)],
        out_shape=[jax.ShapeDtypeStruct((m, d), _F32), jax.ShapeDtypeStruct((m, n), _BF16)],
        compiler_params=params,
        name="mid_layer",
    )(*args, wn, bn.reshape(1, n))


def _sb_attn_kernel(q_ref, k_ref, v_ref, o_ref, acc_ref, carry_ref, least_ref):
    n_t = q_ref.shape[0] // SB_TILE
    row = lax.broadcasted_iota(jnp.int32, (SB_TILE, BLOCK), 0)
    col = lax.broadcasted_iota(jnp.int32, (SB_TILE, BLOCK), 1)
    first_head = col < HEAD_DIM
    key = lax.broadcasted_iota(jnp.int32, (BLOCK, BLOCK), 0)
    pos = lax.broadcasted_iota(jnp.int32, (BLOCK, BLOCK), 1)
    suffix_and_ones = jnp.concatenate(
        [(key >= pos).astype(_BF16), jnp.ones((BLOCK, BLOCK), _BF16)], axis=1)
    suffix_and_ones = jnp.concatenate([suffix_and_ones, suffix_and_ones], axis=0)

    def tile_rows(t):
        start = t * SB_TILE
        return pl.ds(start if isinstance(t, int) else pl.multiple_of(start, SB_TILE), SB_TILE)

    def load_q(t):
        q = q_ref[tile_rows(t), :]
        return jnp.concatenate([jnp.where(first_head, q, 0), jnp.where(first_head, 0, q)], axis=0)

    def window(t):
        end = (t + 1) * SB_TILE
        n_b = min(pl.cdiv(end, BLOCK), SB_WINDOW // BLOCK)
        start = max(end - n_b * BLOCK, 0)
        offsets = []
        for b in range(n_b):
            off = t * SB_TILE - start - b * BLOCK
            assert off > -SB_TILE
            offsets.append(None if off >= BLOCK else off)
        return start, n_b, offsets

    def scores(q2, k):
        return _dot_nt(q2, k)

    def suffix_sums(z, masks):
        neg_abs = lax.bitcast_convert_type(
            lax.bitcast_convert_type(z, jnp.uint32) | jnp.uint32(0x80000000), _F32)
        sp = jnp.maximum(z, 0.0) + jnp.log2(1.0 + jnp.exp2(neg_abs))
        split = []
        for h in range(2):
            for b, mask in enumerate(masks):
                s = sp[h * SB_TILE:(h + 1) * SB_TILE, b * BLOCK:(b + 1) * BLOCK]
                if mask is not None:
                    s = jnp.where(mask, s, 0.0)
                hi = s.astype(_BF16)
                lo = (s - hi.astype(_F32)).astype(_BF16)
                split.append(jnp.concatenate([hi, lo], axis=1))
        return _dot(jnp.concatenate(split, axis=0), suffix_and_ones)

    def weighted_values(z, cs, v, carry_in, masks):
        n_b = len(masks)
        ws, carries = [], []
        for h in range(2):
            carry = None if carry_in is None else carry_in[h]
            wb = [None] * n_b
            for b in reversed(range(n_b)):
                r = cs[(h * n_b + b) * SB_TILE:(h * n_b + b + 1) * SB_TILE]
                arg = z[h * SB_TILE:(h + 1) * SB_TILE, b * BLOCK:(b + 1) * BLOCK] - r[:, :BLOCK]
                if carry is not None:
                    arg = arg - carry
                w = jnp.exp2(arg)
                if masks[b] is not None:
                    w = jnp.where(masks[b], w, 0.0)
                wb[b] = w.astype(_BF16)
                carry = r[:, BLOCK:] if carry is None else carry + r[:, BLOCK:]
            ws.append(jnp.concatenate(wb, axis=1))
            carries.append(carry)
        out = _dot(jnp.concatenate(ws, axis=0), v)
        return jnp.where(first_head, out[:SB_TILE], out[SB_TILE:]), carries

    def window_scores(t):
        start, n_b, _ = window(t)
        return scores(load_q(t), k_ref[pl.ds(start, n_b * BLOCK), :])

    groups = [list(range(s, min(s + SB_TILES_PER_STEP, n_t))) for s in range(0, n_t, SB_TILES_PER_STEP)]
    zs = [window_scores(t) for t in groups[0]]
    for g, tiles in enumerate(groups):
        zs_next = [window_scores(t) for t in groups[g + 1]] if g + 1 < len(groups) else None
        wins = [window(t) for t in tiles]
        masks = [[None if off is None else col < row + off for off in offsets] for _, _, offsets in wins]
        css = [suffix_sums(z, m) for z, m in zip(zs, masks)]
        for t, z, cs, m, (start, n_b, _) in zip(tiles, zs, css, masks, wins):
            out, carry = weighted_values(z, cs, v_ref[pl.ds(start, n_b * BLOCK), :], None, m)
            o_ref[tile_rows(t), :] = out.astype(o_ref.dtype)
            if start > 0:
                acc_ref[t] = out
                carry_ref[t, 0] = carry[0]
                carry_ref[t, 1] = carry[1]
                least_ref[t] = jnp.min(jnp.minimum(carry[0], carry[1]))
        zs = zs_next

    first_open = next((t for t in range(n_t) if window(t)[0] > 0), n_t)

    def finish_tile(t, _):
        def more(state):
            remaining, least = state
            return jnp.logical_and(remaining > 0, least < -SB_UNDERFLOW_LOG2)

        def older_block(state):
            remaining, _ = state
            first = jnp.maximum(remaining - BLOCK, 0)
            rows = pl.ds(pl.multiple_of(first, SB_TILE), BLOCK)
            masks = [col < remaining - first]
            z = scores(load_q(t), k_ref[rows, :])
            out, carry = weighted_values(z, suffix_sums(z, masks), v_ref[rows, :],
                                         [carry_ref[t, 0], carry_ref[t, 1]], masks)
            acc_ref[t] += out
            carry_ref[t, 0] = carry[0]
            carry_ref[t, 1] = carry[1]
            return first, jnp.min(jnp.minimum(carry[0], carry[1]))

        @pl.when(least_ref[t] < -SB_UNDERFLOW_LOG2)
        def _():
            lax.while_loop(more, older_block, ((t + 1) * SB_TILE - SB_WINDOW, least_ref[t]))
            o_ref[tile_rows(t), :] = acc_ref[t].astype(o_ref.dtype)
        return 0

    if n_t > first_open:
        lax.fori_loop(first_open, n_t, finish_tile, 0)


def _sb_attention(qkv, batch, seq, n_heads):
    n_pairs = n_heads // 2
    n_t = seq // SB_TILE
    blk = (None, seq, PAIR)
    return pl.pallas_call(
        _sb_attn_kernel,
        grid=(batch, n_pairs),
        in_specs=[
            pl.BlockSpec(blk, lambda b, p: (b, 0, p)),
            pl.BlockSpec(blk, lambda b, p: (b, 0, n_pairs + p)),
            pl.BlockSpec(blk, lambda b, p: (b, 0, 2 * n_pairs + p)),
        ],
        out_specs=pl.BlockSpec(blk, lambda b, p: (b, 0, p)),
        out_shape=jax.ShapeDtypeStruct((batch, seq, n_heads * HEAD_DIM), _BF16),
        scratch_shapes=[pltpu.VMEM((n_t, SB_TILE, PAIR), _F32),
                        pltpu.VMEM((n_t, 2, SB_TILE, BLOCK), _F32),
                        pltpu.SMEM((n_t,), _F32)],
        compiler_params=pltpu.CompilerParams(
            dimension_semantics=("arbitrary", "arbitrary"), vmem_limit_bytes=VMEM_LIMIT_BYTES),
        name="sb_attention",
    )(qkv, qkv, qkv)


def _bias_band_kernel(rel_ref, bucket_ref, o_ref):
    first = pl.program_id(0) == 0
    head = pl.program_id(1)
    bucket = bucket_ref[...]
    qi = lax.broadcasted_iota(jnp.int32, bucket.shape, 0)
    kj = lax.broadcasted_iota(jnp.int32, bucket.shape, 1)
    dist = qi + BLOCK - kj
    valid = (dist >= 0) & (dist < WINDOW) & jnp.logical_or(jnp.logical_not(first), kj >= BLOCK)
    bias = jnp.zeros(bucket.shape, _F32)
    for b in range(NUM_BUCKETS):
        bias = jnp.where(bucket == b, rel_ref[b, head] * LOG2_E, bias)
    o_ref[...] = jnp.where(valid, bias, NEG_INF)


def _bias_band(rel_bias, bucket):
    n_heads = rel_bias.shape[1]
    return pl.pallas_call(
        _bias_band_kernel,
        grid=(2, n_heads),
        in_specs=[
            pl.BlockSpec(memory_space=pltpu.SMEM),
            pl.BlockSpec(bucket.shape, lambda t, h: (0, 0)),
        ],
        out_specs=pl.BlockSpec((None, None) + bucket.shape, lambda t, h: (t, h, 0, 0)),
        out_shape=jax.ShapeDtypeStruct((2, n_heads) + bucket.shape, _F32),
        name="bias_band",
    )(rel_bias, bucket)


def _band_buckets():
    qi = jnp.arange(BLOCK)[:, None]
    kj = jnp.arange(2 * BLOCK)[None, :]
    n = jnp.maximum(qi + BLOCK - kj, 0)
    nf = jnp.maximum(n, 1).astype(_F32)
    large = MAX_EXACT + (jnp.log(nf / MAX_EXACT) / math.log(MAX_DISTANCE / MAX_EXACT)
                         * (NUM_BUCKETS - MAX_EXACT)).astype(jnp.int32)
    large = jnp.minimum(large, NUM_BUCKETS - 1)
    return jnp.where(n < MAX_EXACT, n, large).astype(jnp.int32)


def _swa_kernel(sink_ref, q_ref, kvp_ref, kvc_ref, bias_ref, o_ref):
    n_pairs = q_ref.shape[-1] // PAIR
    n_kv = kvc_ref.shape[-1] // (2 * PAIR)
    pairs_per_kv = n_pairs // n_kv
    blocks = q_ref.shape[0] // BLOCK
    first_head = lax.broadcasted_iota(jnp.int32, (BLOCK, PAIR), 1) < HEAD_DIM
    ones = jnp.ones((2 * BLOCK, PAIR), _BF16)
    first_table = jnp.where(pl.program_id(1) == 0, 0, 1)

    for t in range(blocks):
        rows = slice(t * BLOCK, (t + 1) * BLOCK)
        table = first_table if t == 0 else 1
        for g in range(n_kv):
            kc = slice(g * PAIR, (g + 1) * PAIR)
            vc = slice((n_kv + g) * PAIR, (n_kv + g + 1) * PAIR)
            if t == 0:
                k = jnp.concatenate([kvp_ref[:, kc], kvc_ref[rows, kc]], axis=0)
                v = jnp.concatenate([kvp_ref[:, vc], kvc_ref[rows, vc]], axis=0)
            else:
                band = slice((t - 1) * BLOCK, (t + 1) * BLOCK)
                k = kvc_ref[band, kc]
                v = kvc_ref[band, vc]
            v_and_ones = jnp.concatenate([v, ones], axis=1)
            qs = []
            for j in range(pairs_per_kv):
                p = g * pairs_per_kv + j
                q = q_ref[rows, p * PAIR:(p + 1) * PAIR]
                qs += [jnp.where(first_head, q, 0), jnp.where(first_head, 0, q)]
            s = _dot_nt(jnp.concatenate(qs, axis=0), k)
            es, sink_terms = [], []
            for hh in range(2 * pairs_per_kv):
                head = 2 * g * pairs_per_kv + hh
                sh = s[hh * BLOCK:(hh + 1) * BLOCK] + bias_ref[table, head]
                sink = sink_ref[head] * LOG2_E
                m = jnp.maximum(jnp.max(sh, axis=-1, keepdims=True), sink)
                es.append(jnp.exp2(sh - m).astype(_BF16))
                sink_terms.append(jnp.exp2(sink - m))
            pv = _dot(jnp.concatenate(es, axis=0), v_and_ones)
            for j in range(pairs_per_kv):
                p = g * pairs_per_kv + j
                halves = []
                for h in range(2):
                    r = pv[(2 * j + h) * BLOCK:(2 * j + h + 1) * BLOCK]
                    halves.append(r[:, :PAIR] / (r[:, PAIR:] + sink_terms[2 * j + h]))
                o_ref[rows, p * PAIR:(p + 1) * PAIR] = (
                    jnp.where(first_head, halves[0], halves[1]).astype(o_ref.dtype))


def _swa_attention(q_src, kv_src, kv_col, d, kvw, bias_tab, sinks):
    batch, seq, _ = q_src.shape
    blocks = math.gcd(seq // BLOCK, SWA_BLOCKS_PER_STEP)
    tq = blocks * BLOCK
    return pl.pallas_call(
        _swa_kernel,
        grid=(batch, seq // tq),
        in_specs=[
            pl.BlockSpec(memory_space=pltpu.SMEM),
            pl.BlockSpec((None, tq, d), lambda b, n: (b, n, 0)),
            pl.BlockSpec((None, BLOCK, kvw), lambda b, n: (b, jnp.maximum(n * blocks - 1, 0), kv_col)),
            pl.BlockSpec((None, tq, kvw), lambda b, n: (b, n, kv_col)),
            _resident(bias_tab.shape),
        ],
        out_specs=pl.BlockSpec((None, tq, d), lambda b, n: (b, n, 0)),
        out_shape=jax.ShapeDtypeStruct((batch, seq, d), _BF16),
        compiler_params=pltpu.CompilerParams(
            dimension_semantics=("arbitrary", "arbitrary"), vmem_limit_bytes=VMEM_LIMIT_BYTES),
        name="swa_attention",
    )(sinks, q_src, kv_src, kv_src, bias_tab)


def _duplicate_heads(w, n_heads):
    lead = w.shape[:-1]
    w = w.reshape(lead + (n_heads, 1, HEAD_DIM))
    w = jnp.broadcast_to(w, lead + (n_heads, 2, HEAD_DIM))
    return w.reshape(lead + (n_heads * PAIR,))


def kernel(x, a_norm, a_wqkv, a_wo, kv_norm, w_kv, b_kv, b_norm, b_wq, b_bq, b_sinks,
           b_wo, b_bo, rel_bias, mlp_norm, mlp_up, mlp_down, final_norm):
    batch, seq, d = x.shape
    m = batch * seq
    n_a = a_wqkv.shape[0]
    n_b = b_wq.shape[0]
    depth = n_a + n_b
    n_heads = d // HEAD_DIM
    n_kv = w_kv.shape[1] // (2 * HEAD_DIM)
    assert seq % BLOCK == 0 and m % min(ROW_TILE, m) == 0 and n_heads % (2 * n_kv) == 0

    bf = lambda w: w.astype(_BF16)
    kvw = 2 * n_kv * PAIR
    q_scale = SCALE * LOG2_E

    def attention_projection(layer):
        if layer < n_a:
            col_scale = jnp.where(jnp.arange(a_wqkv.shape[2]) < d, q_scale, 1.0)
            w = a_norm[layer][:, None] * a_wqkv[layer] * col_scale
            return bf(w), jnp.zeros((w.shape[1],), _F32)
        j = layer - n_a
        w = b_norm[j][:, None] * b_wq[j] * q_scale
        b = b_bq[j] * q_scale
        if j == 0:
            wk, wv = jnp.split(w_kv, 2, axis=-1)
            bk, bv = jnp.split(b_kv, 2, axis=-1)
            w_dup = jnp.concatenate([_duplicate_heads(wk, n_kv), _duplicate_heads(wv, n_kv)], -1)
            b_dup = jnp.concatenate([_duplicate_heads(bk, n_kv), _duplicate_heads(bv, n_kv)], -1)
            w = jnp.concatenate([w, kv_norm[:, None] * w_dup], axis=-1)
            b = jnp.concatenate([b, b_dup], axis=-1)
        return bf(w), b

    bias_tab = _bias_band(rel_bias, _band_buckets())
    h = x.reshape(m, d)
    y = _norm_proj(h, *attention_projection(0)).reshape(batch, seq, -1)
    kv_src = None
    for layer in range(depth):
        if layer < n_a:
            attn = _sb_attention(y, batch, seq, n_heads)
            wo, bo = a_wo[layer], jnp.zeros((d,), _F32)
        else:
            j = layer - n_a
            if j == 0:
                kv_src = y
            attn = _swa_attention(y, kv_src, d // kvw, d, kvw, bias_tab, b_sinks[j])
            wo, bo = b_wo[j], b_bo[j]
        tail = functools.partial(_layer_tail, h, attn.reshape(m, d), bf(wo), bo, mlp_norm[layer],
                                 bf(mlp_up[layer]), bf(mlp_down[layer]))
        if layer == depth - 1:
            return tail(final_gain=final_norm).reshape(batch, seq, d)
        h, y = tail(next_proj=attention_projection(layer + 1))
        y = y.reshape(batch, seq, -1)
```

```python
import functools
import math

import jax
import jax.numpy as jnp
from jax import lax
from jax.experimental import pallas as pl
from jax.experimental.pallas import tpu as pltpu

HEAD_DIM = 64
PAIR = 2 * HEAD_DIM
BLOCK = 128
WINDOW = 128
SWA_GROUP = 8
NUM_BUCKETS = 32
MAX_EXACT = NUM_BUCKETS // 2
MAX_DISTANCE = 128
EPS = 1e-5
NEG_INF = -1e30
SCALE = 1.0 / math.sqrt(HEAD_DIM)

LOG2_E = math.log2(math.e)
SB_UNDERFLOW_LOG2 = -151.0
SB_TILE = 64
SB_WINDOW_FIRST_LAYER = 256
SB_WINDOW_LATER_LAYERS = 384
SB_TILES_PER_STEP = 8

SWA_BLOCKS_PER_STEP = 8

ROW_TILE = 512
FF_CHUNK = 1024
PROJ_CHUNK = 512
SUB_ROWS = 256
VMEM_LIMIT_BYTES = 56 * 1024 * 1024

_F32 = jnp.float32
_BF16 = jnp.bfloat16


def _dot(a, b):
    return jnp.dot(a, b, preferred_element_type=_F32)


def _dot_nt(a, b):
    return lax.dot_general(a, b, (((1,), (1,)), ((), ())), preferred_element_type=_F32)


def _rms_unit(x):
    ms = jnp.mean(x * x, axis=-1, keepdims=True)
    return x * lax.rsqrt(ms + EPS)


def _resident(shape):
    return pl.BlockSpec(shape, lambda *_: (0,) * len(shape), pipeline_mode=pl.Buffered(1))


def _project(x_unit, w_ref, b_ref, y_ref):
    xb = x_unit.astype(_BF16)
    n = y_ref.shape[-1]
    for c in range(0, n, PROJ_CHUNK):
        cols = slice(c, min(c + PROJ_CHUNK, n))
        y_ref[:, cols] = (_dot(xb, w_ref[:, cols]) + b_ref[:, cols]).astype(y_ref.dtype)


def _norm_proj_kernel(x_ref, w_ref, b_ref, y_ref):
    _project(_rms_unit(x_ref[...]), w_ref, b_ref, y_ref)


def _norm_proj(x, w, b):
    m, d = x.shape
    n = w.shape[1]
    tm = min(ROW_TILE, m)
    return pl.pallas_call(
        _norm_proj_kernel,
        grid=(m // tm,),
        in_specs=[pl.BlockSpec((tm, d), lambda i: (i, 0)), _resident((d, n)), _resident((1, n))],
        out_specs=pl.BlockSpec((tm, n), lambda i: (i, 0)),
        out_shape=jax.ShapeDtypeStruct((m, n), _BF16),
        compiler_params=pltpu.CompilerParams(
            dimension_semantics=("arbitrary",), vmem_limit_bytes=VMEM_LIMIT_BYTES),
        name="norm_proj",
    )(x, w, b.reshape(1, n))


def _attn_out_and_mlp(h_ref, a_ref, wo_ref, bo_ref, g_ref, wup_ref, wdn_ref):
    tm = h_ref.shape[0]
    subs = [slice(r, r + min(SUB_ROWS, tm)) for r in range(0, tm, min(SUB_ROWS, tm))]
    h1 = [h_ref[s, :] + _dot(a_ref[s, :], wo_ref[...]) + bo_ref[...] for s in subs]
    xn = [(_rms_unit(x) * g_ref[...]).astype(_BF16) for x in h1]
    acc = h1
    d_ff = wup_ref.shape[1]
    for c in range(0, d_ff, FF_CHUNK):
        u = [jnp.maximum(_dot(x, wup_ref[:, c:c + FF_CHUNK]), 0.0) for x in xn]
        acc = [a + _dot((v * v).astype(_BF16), wdn_ref[c:c + FF_CHUNK, :]) for a, v in zip(acc, u)]
    return subs, acc


def _mid_layer_kernel(h_ref, a_ref, wo_ref, bo_ref, g_ref, wup_ref, wdn_ref, wn_ref, bn_ref,
                      o_ref, y_ref):
    subs, h2 = _attn_out_and_mlp(h_ref, a_ref, wo_ref, bo_ref, g_ref, wup_ref, wdn_ref)
    for s, x in zip(subs, h2):
        o_ref[s, :] = x
    xb = [_rms_unit(x).astype(_BF16) for x in h2]
    n = y_ref.shape[-1]
    for c in range(0, n, PROJ_CHUNK):
        cols = slice(c, min(c + PROJ_CHUNK, n))
        for s, x in zip(subs, xb):
            y_ref[s, cols] = (_dot(x, wn_ref[:, cols]) + bn_ref[:, cols]).astype(y_ref.dtype)


def _last_layer_kernel(h_ref, a_ref, wo_ref, bo_ref, g_ref, wup_ref, wdn_ref, fg_ref, o_ref):
    subs, h2 = _attn_out_and_mlp(h_ref, a_ref, wo_ref, bo_ref, g_ref, wup_ref, wdn_ref)
    for s, x in zip(subs, h2):
        o_ref[s, :] = _rms_unit(x) * fg_ref[...]


def _layer_tail(h, a, wo, bo, g, wup, wdn, *, next_proj=None, final_gain=None):
    m, d = h.shape
    d_ff = wup.shape[1]
    tm = min(ROW_TILE, m)
    row = lambda i: (i, 0)
    in_specs = [
        pl.BlockSpec((tm, d), row),
        pl.BlockSpec((tm, d), row),
        _resident((d, d)),
        _resident((1, d)),
        _resident((1, d)),
        _resident((d, d_ff)),
        _resident((d_ff, d)),
    ]
    args = [h, a, wo, bo.reshape(1, d), g.reshape(1, d), wup, wdn]
    params = pltpu.CompilerParams(
        dimension_semantics=("arbitrary",), vmem_limit_bytes=VMEM_LIMIT_BYTES)
    if next_proj is None:
        return pl.pallas_call(
            _last_layer_kernel,
            grid=(m // tm,),
            in_specs=in_specs + [_resident((1, d))],
            out_specs=pl.BlockSpec((tm, d), row),
            out_shape=jax.ShapeDtypeStruct((m, d), _F32),
            compiler_params=params,
            name="last_layer",
        )(*args, final_gain.reshape(1, d))
    wn, bn = next_proj
    n = wn.shape[1]
    return pl.pallas_call(
        _mid_layer_kernel,
        grid=(m // tm,),
        in_specs=in_specs + [_resident((d, n)), _resident((1, n))],
        out_specs=[pl.BlockSpec((tm, d), row), pl.BlockSpec((tm, n), row)],
        out_shape=[jax.ShapeDtypeStruct((m, d), _F32), jax.ShapeDtypeStruct((m, n), _BF16)],
        compiler_params=params,
        name="mid_layer",
    )(*args, wn, bn.reshape(1, n))


def _sb_attn_kernel(q_ref, k_ref, v_ref, o_ref, acc_ref, carry_ref, least_ref, *, window_keys):
    n_t = q_ref.shape[0] // SB_TILE
    row = lax.broadcasted_iota(jnp.int32, (SB_TILE, BLOCK), 0)
    col = lax.broadcasted_iota(jnp.int32, (SB_TILE, BLOCK), 1)
    first_head = col < HEAD_DIM
    key = lax.broadcasted_iota(jnp.int32, (BLOCK, BLOCK), 0)
    pos = lax.broadcasted_iota(jnp.int32, (BLOCK, BLOCK), 1)
    suffix_and_ones = jnp.concatenate(
        [(key >= pos).astype(_BF16), jnp.ones((BLOCK, BLOCK), _BF16)], axis=1)
    suffix_and_ones = jnp.concatenate([suffix_and_ones, suffix_and_ones], axis=0)

    def tile_rows(t):
        start = t * SB_TILE
        return pl.ds(start if isinstance(t, int) else pl.multiple_of(start, SB_TILE), SB_TILE)

    def load_q(t):
        q = q_ref[tile_rows(t), :]
        return jnp.concatenate([jnp.where(first_head, q, 0), jnp.where(first_head, 0, q)], axis=0)

    def window(t):
        end = (t + 1) * SB_TILE
        n_b = min(pl.cdiv(end, BLOCK), window_keys // BLOCK)
        start = max(end - n_b * BLOCK, 0)
        offsets = []
        for b in range(n_b):
            off = t * SB_TILE - start - b * BLOCK
            assert off > -SB_TILE
            offsets.append(None if off >= BLOCK else off)
        return start, n_b, offsets

    def scores(q2, k):
        return _dot_nt(q2, k)

    def suffix_sums(z, masks):
        neg_abs = lax.bitcast_convert_type(
            lax.bitcast_convert_type(z, jnp.uint32) | jnp.uint32(0x80000000), _F32)
        sp = jnp.maximum(z, 0.0) + jnp.log2(1.0 + jnp.exp2(neg_abs))
        split = []
        for h in range(2):
            for b, mask in enumerate(masks):
                s = sp[h * SB_TILE:(h + 1) * SB_TILE, b * BLOCK:(b + 1) * BLOCK]
                if mask is not None:
                    s = jnp.where(mask, s, 0.0)
                hi = s.astype(_BF16)
                lo = (s - hi.astype(_F32)).astype(_BF16)
                split.append(jnp.concatenate([hi, lo], axis=1))
        return _dot(jnp.concatenate(split, axis=0), suffix_and_ones)

    def weighted_values(z, cs, v, carry_in, masks):
        n_b = len(masks)
        ws, carries = [], []
        for h in range(2):
            carry = None if carry_in is None else carry_in[h]
            wb = [None] * n_b
            for b in reversed(range(n_b)):
                r = cs[(h * n_b + b) * SB_TILE:(h * n_b + b + 1) * SB_TILE]
                arg = z[h * SB_TILE:(h + 1) * SB_TILE, b * BLOCK:(b + 1) * BLOCK] - r[:, :BLOCK]
                if carry is not None:
                    arg = arg - carry
                w = jnp.exp2(arg)
                if masks[b] is not None:
                    w = jnp.where(masks[b], w, 0.0)
                wb[b] = w.astype(_BF16)
                carry = r[:, BLOCK:] if carry is None else carry + r[:, BLOCK:]
            ws.append(jnp.concatenate(wb, axis=1))
            carries.append(carry)
        out = _dot(jnp.concatenate(ws, axis=0), v)
        return jnp.where(first_head, out[:SB_TILE], out[SB_TILE:]), carries

    def window_scores(t):
        start, n_b, _ = window(t)
        return scores(load_q(t), k_ref[pl.ds(start, n_b * BLOCK), :])

    groups = [list(range(s, min(s + SB_TILES_PER_STEP, n_t))) for s in range(0, n_t, SB_TILES_PER_STEP)]
    zs = [window_scores(t) for t in groups[0]]
    for g, tiles in enumerate(groups):
        zs_next = [window_scores(t) for t in groups[g + 1]] if g + 1 < len(groups) else None
        wins = [window(t) for t in tiles]
        masks = [[None if off is None else col < row + off for off in offsets] for _, _, offsets in wins]
        css = [suffix_sums(z, m) for z, m in zip(zs, masks)]
        for t, z, cs, m, (start, n_b, _) in zip(tiles, zs, css, masks, wins):
            out, carry = weighted_values(z, cs, v_ref[pl.ds(start, n_b * BLOCK), :], None, m)
            o_ref[tile_rows(t), :] = out.astype(o_ref.dtype)
            if start > 0:
                acc_ref[t] = out
                carry_ref[t, 0] = carry[0]
                carry_ref[t, 1] = carry[1]
                least_ref[t] = jnp.min(jnp.minimum(carry[0], carry[1]))
        zs = zs_next

    first_open = next((t for t in range(n_t) if window(t)[0] > 0), n_t)

    def finish_tile(t, _):
        def more(state):
            remaining, least = state
            return jnp.logical_and(remaining > 0, least < -SB_UNDERFLOW_LOG2)

        def older_block(state):
            remaining, _ = state
            first = jnp.maximum(remaining - BLOCK, 0)
            rows = pl.ds(pl.multiple_of(first, SB_TILE), BLOCK)
            masks = [col < remaining - first]
            z = scores(load_q(t), k_ref[rows, :])
            out, carry = weighted_values(z, suffix_sums(z, masks), v_ref[rows, :],
                                         [carry_ref[t, 0], carry_ref[t, 1]], masks)
            acc_ref[t] += out
            carry_ref[t, 0] = carry[0]
            carry_ref[t, 1] = carry[1]
            return first, jnp.min(jnp.minimum(carry[0], carry[1]))

        @pl.when(least_ref[t] < -SB_UNDERFLOW_LOG2)
        def _():
            lax.while_loop(more, older_block, ((t + 1) * SB_TILE - window_keys, least_ref[t]))
            o_ref[tile_rows(t), :] = acc_ref[t].astype(o_ref.dtype)
        return 0

    if n_t > first_open:
        lax.fori_loop(first_open, n_t, finish_tile, 0)


def _sb_attention(qkv, batch, seq, n_heads, window_keys):
    n_pairs = n_heads // 2
    n_t = seq // SB_TILE
    blk = (None, seq, PAIR)
    return pl.pallas_call(
        functools.partial(_sb_attn_kernel, window_keys=window_keys),
        grid=(batch, n_pairs),
        in_specs=[
            pl.BlockSpec(blk, lambda b, p: (b, 0, p)),
            pl.BlockSpec(blk, lambda b, p: (b, 0, n_pairs + p)),
            pl.BlockSpec(blk, lambda b, p: (b, 0, 2 * n_pairs + p)),
        ],
        out_specs=pl.BlockSpec(blk, lambda b, p: (b, 0, p)),
        out_shape=jax.ShapeDtypeStruct((batch, seq, n_heads * HEAD_DIM), _BF16),
        scratch_shapes=[pltpu.VMEM((n_t, SB_TILE, PAIR), _F32),
                        pltpu.VMEM((n_t, 2, SB_TILE, BLOCK), _F32),
                        pltpu.SMEM((n_t,), _F32)],
        compiler_params=pltpu.CompilerParams(
            dimension_semantics=("arbitrary", "arbitrary"), vmem_limit_bytes=VMEM_LIMIT_BYTES),
        name="sb_attention",
    )(qkv, qkv, qkv)


def _bias_band_kernel(rel_ref, bucket_ref, o_ref):
    first = pl.program_id(0) == 0
    head = pl.program_id(1)
    bucket = bucket_ref[...]
    qi = lax.broadcasted_iota(jnp.int32, bucket.shape, 0)
    kj = lax.broadcasted_iota(jnp.int32, bucket.shape, 1)
    dist = qi + BLOCK - kj
    valid = (dist >= 0) & (dist < WINDOW) & jnp.logical_or(jnp.logical_not(first), kj >= BLOCK)
    bias = jnp.zeros(bucket.shape, _F32)
    for b in range(NUM_BUCKETS):
        bias = jnp.where(bucket == b, rel_ref[b, head] * LOG2_E, bias)
    o_ref[...] = jnp.where(valid, bias, NEG_INF)


def _bias_band(rel_bias, bucket):
    n_heads = rel_bias.shape[1]
    return pl.pallas_call(
        _bias_band_kernel,
        grid=(2, n_heads),
        in_specs=[
            pl.BlockSpec(memory_space=pltpu.SMEM),
            pl.BlockSpec(bucket.shape, lambda t, h: (0, 0)),
        ],
        out_specs=pl.BlockSpec((None, None) + bucket.shape, lambda t, h: (t, h, 0, 0)),
        out_shape=jax.ShapeDtypeStruct((2, n_heads) + bucket.shape, _F32),
        name="bias_band",
    )(rel_bias, bucket)


def _band_buckets():
    qi = jnp.arange(BLOCK)[:, None]
    kj = jnp.arange(2 * BLOCK)[None, :]
    n = jnp.maximum(qi + BLOCK - kj, 0)
    nf = jnp.maximum(n, 1).astype(_F32)
    large = MAX_EXACT + (jnp.log(nf / MAX_EXACT) / math.log(MAX_DISTANCE / MAX_EXACT)
                         * (NUM_BUCKETS - MAX_EXACT)).astype(jnp.int32)
    large = jnp.minimum(large, NUM_BUCKETS - 1)
    return jnp.where(n < MAX_EXACT, n, large).astype(jnp.int32)


def _swa_kernel(sink_ref, q_ref, kvp_ref, kvc_ref, bias_ref, o_ref):
    n_pairs = q_ref.shape[-1] // PAIR
    n_kv = kvc_ref.shape[-1] // (2 * PAIR)
    pairs_per_kv = n_pairs // n_kv
    blocks = q_ref.shape[0] // BLOCK
    first_head = lax.broadcasted_iota(jnp.int32, (BLOCK, PAIR), 1) < HEAD_DIM
    ones = jnp.ones((2 * BLOCK, PAIR), _BF16)
    first_table = jnp.where(pl.program_id(1) == 0, 0, 1)

    for t in range(blocks):
        rows = slice(t * BLOCK, (t + 1) * BLOCK)
        table = first_table if t == 0 else 1
        for g in range(n_kv):
            kc = slice(g * PAIR, (g + 1) * PAIR)
            vc = slice((n_kv + g) * PAIR, (n_kv + g + 1) * PAIR)
            if t == 0:
                k = jnp.concatenate([kvp_ref[:, kc], kvc_ref[rows, kc]], axis=0)
                v = jnp.concatenate([kvp_ref[:, vc], kvc_ref[rows, vc]], axis=0)
            else:
                band = slice((t - 1) * BLOCK, (t + 1) * BLOCK)
                k = kvc_ref[band, kc]
                v = kvc_ref[band, vc]
            v_and_ones = jnp.concatenate([v, ones], axis=1)
            qs = []
            for j in range(pairs_per_kv):
                p = g * pairs_per_kv + j
                q = q_ref[rows, p * PAIR:(p + 1) * PAIR]
                qs += [jnp.where(first_head, q, 0), jnp.where(first_head, 0, q)]
            s = _dot_nt(jnp.concatenate(qs, axis=0), k)
            es, sink_terms = [], []
            for hh in range(2 * pairs_per_kv):
                head = 2 * g * pairs_per_kv + hh
                sh = s[hh * BLOCK:(hh + 1) * BLOCK] + bias_ref[table, head]
                sink = sink_ref[head] * LOG2_E
                m = jnp.maximum(jnp.max(sh, axis=-1, keepdims=True), sink)
                es.append(jnp.exp2(sh - m).astype(_BF16))
                sink_terms.append(jnp.exp2(sink - m))
            pv = _dot(jnp.concatenate(es, axis=0), v_and_ones)
            for j in range(pairs_per_kv):
                p = g * pairs_per_kv + j
                halves = []
                for h in range(2):
                    r = pv[(2 * j + h) * BLOCK:(2 * j + h + 1) * BLOCK]
                    halves.append(r[:, :PAIR] / (r[:, PAIR:] + sink_terms[2 * j + h]))
                o_ref[rows, p * PAIR:(p + 1) * PAIR] = (
                    jnp.where(first_head, halves[0], halves[1]).astype(o_ref.dtype))


def _swa_attention(q_src, kv_src, kv_col, d, kvw, bias_tab, sinks):
    batch, seq, _ = q_src.shape
    blocks = math.gcd(seq // BLOCK, SWA_BLOCKS_PER_STEP)
    tq = blocks * BLOCK
    return pl.pallas_call(
        _swa_kernel,
        grid=(batch, seq // tq),
        in_specs=[
            pl.BlockSpec(memory_space=pltpu.SMEM),
            pl.BlockSpec((None, tq, d), lambda b, n: (b, n, 0)),
            pl.BlockSpec((None, BLOCK, kvw), lambda b, n: (b, jnp.maximum(n * blocks - 1, 0), kv_col)),
            pl.BlockSpec((None, tq, kvw), lambda b, n: (b, n, kv_col)),
            _resident(bias_tab.shape),
        ],
        out_specs=pl.BlockSpec((None, tq, d), lambda b, n: (b, n, 0)),
        out_shape=jax.ShapeDtypeStruct((batch, seq, d), _BF16),
        compiler_params=pltpu.CompilerParams(
            dimension_semantics=("arbitrary", "arbitrary"), vmem_limit_bytes=VMEM_LIMIT_BYTES),
        name="swa_attention",
    )(sinks, q_src, kv_src, kv_src, bias_tab)


def _duplicate_heads(w, n_heads):
    lead = w.shape[:-1]
    w = w.reshape(lead + (n_heads, 1, HEAD_DIM))
    w = jnp.broadcast_to(w, lead + (n_heads, 2, HEAD_DIM))
    return w.reshape(lead + (n_heads * PAIR,))


def kernel(x, a_norm, a_wqkv, a_wo, kv_norm, w_kv, b_kv, b_norm, b_wq, b_bq, b_sinks,
           b_wo, b_bo, rel_bias, mlp_norm, mlp_up, mlp_down, final_norm):
    batch, seq, d = x.shape
    m = batch * seq
    n_a = a_wqkv.shape[0]
    n_b = b_wq.shape[0]
    depth = n_a + n_b
    n_heads = d // HEAD_DIM
    n_kv = w_kv.shape[1] // (2 * HEAD_DIM)
    assert seq % BLOCK == 0 and m % min(ROW_TILE, m) == 0 and n_heads % (2 * n_kv) == 0

    bf = lambda w: w.astype(_BF16)
    kvw = 2 * n_kv * PAIR
    q_scale = SCALE * LOG2_E

    def attention_projection(layer):
        if layer < n_a:
            col_scale = jnp.where(jnp.arange(a_wqkv.shape[2]) < d, q_scale, 1.0)
            w = a_norm[layer][:, None] * a_wqkv[layer] * col_scale
            return bf(w), jnp.zeros((w.shape[1],), _F32)
        j = layer - n_a
        w = b_norm[j][:, None] * b_wq[j] * q_scale
        b = b_bq[j] * q_scale
        if j == 0:
            wk, wv = jnp.split(w_kv, 2, axis=-1)
            bk, bv = jnp.split(b_kv, 2, axis=-1)
            w_dup = jnp.concatenate([_duplicate_heads(wk, n_kv), _duplicate_heads(wv, n_kv)], -1)
            b_dup = jnp.concatenate([_duplicate_heads(bk, n_kv), _duplicate_heads(bv, n_kv)], -1)
            w = jnp.concatenate([w, kv_norm[:, None] * w_dup], axis=-1)
            b = jnp.concatenate([b, b_dup], axis=-1)
        return bf(w), b

    bias_tab = _bias_band(rel_bias, _band_buckets())
    h = x.reshape(m, d)
    y = _norm_proj(h, *attention_projection(0)).reshape(batch, seq, -1)
    kv_src = None
    for layer in range(depth):
        if layer < n_a:
            attn = _sb_attention(y, batch, seq, n_heads,
                                 SB_WINDOW_FIRST_LAYER if layer == 0 else SB_WINDOW_LATER_LAYERS)
            wo, bo = a_wo[layer], jnp.zeros((d,), _F32)
        else:
            j = layer - n_a
            if j == 0:
                kv_src = y
            attn = _swa_attention(y, kv_src, d // kvw, d, kvw, bias_tab, b_sinks[j])
            wo, bo = b_wo[j], b_bo[j]
        tail = functools.partial(_layer_tail, h, attn.reshape(m, d), bf(wo), bo, mlp_norm[layer],
                                 bf(mlp_up[layer]), bf(mlp_down[layer]))
        if layer == depth - 1:
            return tail(final_gain=final_norm).reshape(batch, seq, d)
        h, y = tail(next_proj=attention_projection(layer + 1))
        y = y.reshape(batch, seq, -1)
```

```python
import functools
import math

import jax
import jax.numpy as jnp
from jax import lax
from jax.experimental import pallas as pl
from jax.experimental.pallas import tpu as pltpu

HEAD_DIM = 64
PAIR = 2 * HEAD_DIM
BLOCK = 128
WINDOW = 128
SWA_GROUP = 8
NUM_BUCKETS = 32
MAX_EXACT = NUM_BUCKETS // 2
MAX_DISTANCE = 128
EPS = 1e-5
NEG_INF = -1e30
SCALE = 1.0 / math.sqrt(HEAD_DIM)

LOG2_E = math.log2(math.e)
SB_UNDERFLOW_LOG2 = -151.0
SB_TILE = 64
SB_WINDOW_FIRST_LAYER = 256
SB_WINDOW_LATER_LAYERS = 384
SB_SCORE_LOOKAHEAD = 6
SB_SUFFIX_LOOKAHEAD = 3

SWA_BLOCKS_PER_STEP = 8

ROW_TILE = 512
FF_CHUNK = 1024
PROJ_CHUNK = 512
SUB_ROWS = 256
VMEM_LIMIT_BYTES = 56 * 1024 * 1024

_F32 = jnp.float32
_BF16 = jnp.bfloat16


def _dot(a, b):
    return jnp.dot(a, b, preferred_element_type=_F32)


def _dot_nt(a, b):
    return lax.dot_general(a, b, (((1,), (1,)), ((), ())), preferred_element_type=_F32)


def _rms_unit(x):
    ms = jnp.mean(x * x, axis=-1, keepdims=True)
    return x * lax.rsqrt(ms + EPS)


def _resident(shape):
    return pl.BlockSpec(shape, lambda *_: (0,) * len(shape), pipeline_mode=pl.Buffered(1))


def _project(x_unit, w_ref, b_ref, y_ref):
    xb = x_unit.astype(_BF16)
    n = y_ref.shape[-1]
    for c in range(0, n, PROJ_CHUNK):
        cols = slice(c, min(c + PROJ_CHUNK, n))
        y_ref[:, cols] = (_dot(xb, w_ref[:, cols]) + b_ref[:, cols]).astype(y_ref.dtype)


def _norm_proj_kernel(x_ref, w_ref, b_ref, y_ref):
    _project(_rms_unit(x_ref[...]), w_ref, b_ref, y_ref)


def _norm_proj(x, w, b):
    m, d = x.shape
    n = w.shape[1]
    tm = min(ROW_TILE, m)
    return pl.pallas_call(
        _norm_proj_kernel,
        grid=(m // tm,),
        in_specs=[pl.BlockSpec((tm, d), lambda i: (i, 0)), _resident((d, n)), _resident((1, n))],
        out_specs=pl.BlockSpec((tm, n), lambda i: (i, 0)),
        out_shape=jax.ShapeDtypeStruct((m, n), _BF16),
        compiler_params=pltpu.CompilerParams(
            dimension_semantics=("arbitrary",), vmem_limit_bytes=VMEM_LIMIT_BYTES),
        name="norm_proj",
    )(x, w, b.reshape(1, n))


def _attn_out_and_mlp(h_ref, a_ref, wo_ref, bo_ref, g_ref, wup_ref, wdn_ref):
    tm = h_ref.shape[0]
    subs = [slice(r, r + min(SUB_ROWS, tm)) for r in range(0, tm, min(SUB_ROWS, tm))]
    h1 = [h_ref[s, :] + _dot(a_ref[s, :], wo_ref[...]) + bo_ref[...] for s in subs]
    xn = [(_rms_unit(x) * g_ref[...]).astype(_BF16) for x in h1]
    acc = h1
    d_ff = wup_ref.shape[1]
    for c in range(0, d_ff, FF_CHUNK):
        u = [jnp.maximum(_dot(x, wup_ref[:, c:c + FF_CHUNK]), 0.0) for x in xn]
        acc = [a + _dot((v * v).astype(_BF16), wdn_ref[c:c + FF_CHUNK, :]) for a, v in zip(acc, u)]
    return subs, acc


def _mid_layer_kernel(h_ref, a_ref, wo_ref, bo_ref, g_ref, wup_ref, wdn_ref, wn_ref, bn_ref,
                      o_ref, y_ref):
    subs, h2 = _attn_out_and_mlp(h_ref, a_ref, wo_ref, bo_ref, g_ref, wup_ref, wdn_ref)
    for s, x in zip(subs, h2):
        o_ref[s, :] = x
    xb = [_rms_unit(x).astype(_BF16) for x in h2]
    n = y_ref.shape[-1]
    for c in range(0, n, PROJ_CHUNK):
        cols = slice(c, min(c + PROJ_CHUNK, n))
        for s, x in zip(subs, xb):
            y_ref[s, cols] = (_dot(x, wn_ref[:, cols]) + bn_ref[:, cols]).astype(y_ref.dtype)


def _last_layer_kernel(h_ref, a_ref, wo_ref, bo_ref, g_ref, wup_ref, wdn_ref, fg_ref, o_ref):
    subs, h2 = _attn_out_and_mlp(h_ref, a_ref, wo_ref, bo_ref, g_ref, wup_ref, wdn_ref)
    for s, x in zip(subs, h2):
        o_ref[s, :] = _rms_unit(x) * fg_ref[...]


def _layer_tail(h, a, wo, bo, g, wup, wdn, *, next_proj=None, final_gain=None):
    m, d = h.shape
    d_ff = wup.shape[1]
    tm = min(ROW_TILE, m)
    row = lambda i: (i, 0)
    in_specs = [
        pl.BlockSpec((tm, d), row),
        pl.BlockSpec((tm, d), row),
        _resident((d, d)),
        _resident((1, d)),
        _resident((1, d)),
        _resident((d, d_ff)),
        _resident((d_ff, d)),
    ]
    args = [h, a, wo, bo.reshape(1, d), g.reshape(1, d), wup, wdn]
    params = pltpu.CompilerParams(
        dimension_semantics=("arbitrary",), vmem_limit_bytes=VMEM_LIMIT_BYTES)
    if next_proj is None:
        return pl.pallas_call(
            _last_layer_kernel,
            grid=(m // tm,),
            in_specs=in_specs + [_resident((1, d))],
            out_specs=pl.BlockSpec((tm, d), row),
            out_shape=jax.ShapeDtypeStruct((m, d), _F32),
            compiler_params=params,
            name="last_layer",
        )(*args, final_gain.reshape(1, d))
    wn, bn = next_proj
    n = wn.shape[1]
    return pl.pallas_call(
        _mid_layer_kernel,
        grid=(m // tm,),
        in_specs=in_specs + [_resident((d, n)), _resident((1, n))],
        out_specs=[pl.BlockSpec((tm, d), row), pl.BlockSpec((tm, n), row)],
        out_shape=[jax.ShapeDtypeStruct((m, d), _F32), jax.ShapeDtypeStruct((m, n), _BF16)],
        compiler_params=params,
        name="mid_layer",
    )(*args, wn, bn.reshape(1, n))


def _sb_attn_kernel(q_ref, k_ref, v_ref, o_ref, acc_ref, carry_ref, least_ref, *, window_keys):
    n_t = q_ref.shape[0] // SB_TILE
    row = lax.broadcasted_iota(jnp.int32, (SB_TILE, BLOCK), 0)
    col = lax.broadcasted_iota(jnp.int32, (SB_TILE, BLOCK), 1)
    first_head = col < HEAD_DIM
    key = lax.broadcasted_iota(jnp.int32, (BLOCK, BLOCK), 0)
    pos = lax.broadcasted_iota(jnp.int32, (BLOCK, BLOCK), 1)
    suffix_and_ones = jnp.concatenate(
        [(key >= pos).astype(_BF16), jnp.ones((BLOCK, BLOCK), _BF16)], axis=1)
    suffix_and_ones = jnp.concatenate([suffix_and_ones, suffix_and_ones], axis=0)

    def tile_rows(t):
        start = t * SB_TILE
        return pl.ds(start if isinstance(t, int) else pl.multiple_of(start, SB_TILE), SB_TILE)

    def load_q(t):
        q = q_ref[tile_rows(t), :]
        return jnp.concatenate([jnp.where(first_head, q, 0), jnp.where(first_head, 0, q)], axis=0)

    def window(t):
        end = (t + 1) * SB_TILE
        n_b = min(pl.cdiv(end, BLOCK), window_keys // BLOCK)
        start = max(end - n_b * BLOCK, 0)
        offsets = []
        for b in range(n_b):
            off = t * SB_TILE - start - b * BLOCK
            assert off > -SB_TILE
            offsets.append(None if off >= BLOCK else off)
        return start, n_b, offsets

    def scores(q2, k):
        return _dot_nt(q2, k)

    def suffix_sums(z, masks):
        neg_abs = lax.bitcast_convert_type(
            lax.bitcast_convert_type(z, jnp.uint32) | jnp.uint32(0x80000000), _F32)
        sp = jnp.maximum(z, 0.0) + jnp.log2(1.0 + jnp.exp2(neg_abs))
        split = []
        for h in range(2):
            for b, mask in enumerate(masks):
                s = sp[h * SB_TILE:(h + 1) * SB_TILE, b * BLOCK:(b + 1) * BLOCK]
                if mask is not None:
                    s = jnp.where(mask, s, 0.0)
                hi = s.astype(_BF16)
                lo = (s - hi.astype(_F32)).astype(_BF16)
                split.append(jnp.concatenate([hi, lo], axis=1))
        return _dot(jnp.concatenate(split, axis=0), suffix_and_ones)

    def weighted_values(z, cs, v, carry_in, masks):
        n_b = len(masks)
        ws, carries = [], []
        for h in range(2):
            carry = None if carry_in is None else carry_in[h]
            wb = [None] * n_b
            for b in reversed(range(n_b)):
                r = cs[(h * n_b + b) * SB_TILE:(h * n_b + b + 1) * SB_TILE]
                arg = z[h * SB_TILE:(h + 1) * SB_TILE, b * BLOCK:(b + 1) * BLOCK] - r[:, :BLOCK]
                if carry is not None:
                    arg = arg - carry
                w = jnp.exp2(arg)
                if masks[b] is not None:
                    w = jnp.where(masks[b], w, 0.0)
                wb[b] = w.astype(_BF16)
                carry = r[:, BLOCK:] if carry is None else carry + r[:, BLOCK:]
            ws.append(jnp.concatenate(wb, axis=1))
            carries.append(carry)
        out = _dot(jnp.concatenate(ws, axis=0), v)
        return jnp.where(first_head, out[:SB_TILE], out[SB_TILE:]), carries

    def window_scores(t):
        start, n_b, _ = window(t)
        return scores(load_q(t), k_ref[pl.ds(start, n_b * BLOCK), :])

    zs, css, wins, masks = {}, {}, {}, {}

    def issue_scores(t):
        if t < n_t:
            wins[t] = window(t)
            masks[t] = [None if off is None else col < row + off for off in wins[t][2]]
            zs[t] = window_scores(t)

    def issue_suffix_sums(t):
        if t < n_t:
            css[t] = suffix_sums(zs[t], masks[t])

    for t in range(SB_SCORE_LOOKAHEAD):
        issue_scores(t)
    for t in range(SB_SUFFIX_LOOKAHEAD):
        issue_suffix_sums(t)
    for t in range(n_t):
        issue_scores(t + SB_SCORE_LOOKAHEAD)
        issue_suffix_sums(t + SB_SUFFIX_LOOKAHEAD)
        start, n_b, _ = wins.pop(t)
        out, carry = weighted_values(zs.pop(t), css.pop(t), v_ref[pl.ds(start, n_b * BLOCK), :],
                                     None, masks.pop(t))
        o_ref[tile_rows(t), :] = out.astype(o_ref.dtype)
        if start > 0:
            acc_ref[t] = out
            carry_ref[t, 0] = carry[0]
            carry_ref[t, 1] = carry[1]
            least_ref[t] = jnp.min(jnp.minimum(carry[0], carry[1]))

    first_open = next((t for t in range(n_t) if window(t)[0] > 0), n_t)

    def finish_tile(t, _):
        def more(state):
            remaining, least = state
            return jnp.logical_and(remaining > 0, least < -SB_UNDERFLOW_LOG2)

        def older_block(state):
            remaining, _ = state
            first = jnp.maximum(remaining - BLOCK, 0)
            rows = pl.ds(pl.multiple_of(first, SB_TILE), BLOCK)
            masks = [col < remaining - first]
            z = scores(load_q(t), k_ref[rows, :])
            out, carry = weighted_values(z, suffix_sums(z, masks), v_ref[rows, :],
                                         [carry_ref[t, 0], carry_ref[t, 1]], masks)
            acc_ref[t] += out
            carry_ref[t, 0] = carry[0]
            carry_ref[t, 1] = carry[1]
            return first, jnp.min(jnp.minimum(carry[0], carry[1]))

        @pl.when(least_ref[t] < -SB_UNDERFLOW_LOG2)
        def _():
            lax.while_loop(more, older_block, ((t + 1) * SB_TILE - window_keys, least_ref[t]))
            o_ref[tile_rows(t), :] = acc_ref[t].astype(o_ref.dtype)
        return 0

    if n_t > first_open:
        lax.fori_loop(first_open, n_t, finish_tile, 0)


def _sb_attention(qkv, batch, seq, n_heads, window_keys):
    n_pairs = n_heads // 2
    n_t = seq // SB_TILE
    blk = (None, seq, PAIR)
    return pl.pallas_call(
        functools.partial(_sb_attn_kernel, window_keys=window_keys),
        grid=(batch, n_pairs),
        in_specs=[
            pl.BlockSpec(blk, lambda b, p: (b, 0, p)),
            pl.BlockSpec(blk, lambda b, p: (b, 0, n_pairs + p)),
            pl.BlockSpec(blk, lambda b, p: (b, 0, 2 * n_pairs + p)),
        ],
        out_specs=pl.BlockSpec(blk, lambda b, p: (b, 0, p)),
        out_shape=jax.ShapeDtypeStruct((batch, seq, n_heads * HEAD_DIM), _BF16),
        scratch_shapes=[pltpu.VMEM((n_t, SB_TILE, PAIR), _F32),
                        pltpu.VMEM((n_t, 2, SB_TILE, BLOCK), _F32),
                        pltpu.SMEM((n_t,), _F32)],
        compiler_params=pltpu.CompilerParams(
            dimension_semantics=("arbitrary", "arbitrary"), vmem_limit_bytes=VMEM_LIMIT_BYTES),
        name="sb_attention",
    )(qkv, qkv, qkv)


def _bias_band_kernel(rel_ref, bucket_ref, o_ref):
    first = pl.program_id(0) == 0
    head = pl.program_id(1)
    bucket = bucket_ref[...]
    qi = lax.broadcasted_iota(jnp.int32, bucket.shape, 0)
    kj = lax.broadcasted_iota(jnp.int32, bucket.shape, 1)
    dist = qi + BLOCK - kj
    valid = (dist >= 0) & (dist < WINDOW) & jnp.logical_or(jnp.logical_not(first), kj >= BLOCK)
    bias = jnp.zeros(bucket.shape, _F32)
    for b in range(NUM_BUCKETS):
        bias = jnp.where(bucket == b, rel_ref[b, head] * LOG2_E, bias)
    o_ref[...] = jnp.where(valid, bias, NEG_INF)


def _bias_band(rel_bias, bucket):
    n_heads = rel_bias.shape[1]
    return pl.pallas_call(
        _bias_band_kernel,
        grid=(2, n_heads),
        in_specs=[
            pl.BlockSpec(memory_space=pltpu.SMEM),
            pl.BlockSpec(bucket.shape, lambda t, h: (0, 0)),
        ],
        out_specs=pl.BlockSpec((None, None) + bucket.shape, lambda t, h: (t, h, 0, 0)),
        out_shape=jax.ShapeDtypeStruct((2, n_heads) + bucket.shape, _F32),
        name="bias_band",
    )(rel_bias, bucket)


def _band_buckets():
    qi = jnp.arange(BLOCK)[:, None]
    kj = jnp.arange(2 * BLOCK)[None, :]
    n = jnp.maximum(qi + BLOCK - kj, 0)
    nf = jnp.maximum(n, 1).astype(_F32)
    large = MAX_EXACT + (jnp.log(nf / MAX_EXACT) / math.log(MAX_DISTANCE / MAX_EXACT)
                         * (NUM_BUCKETS - MAX_EXACT)).astype(jnp.int32)
    large = jnp.minimum(large, NUM_BUCKETS - 1)
    return jnp.where(n < MAX_EXACT, n, large).astype(jnp.int32)


def _swa_kernel(sink_ref, q_ref, kvp_ref, kvc_ref, bias_ref, o_ref):
    n_pairs = q_ref.shape[-1] // PAIR
    n_kv = kvc_ref.shape[-1] // (2 * PAIR)
    pairs_per_kv = n_pairs // n_kv
    blocks = q_ref.shape[0] // BLOCK
    first_head = lax.broadcasted_iota(jnp.int32, (BLOCK, PAIR), 1) < HEAD_DIM
    ones = jnp.ones((2 * BLOCK, PAIR), _BF16)
    first_table = jnp.where(pl.program_id(1) == 0, 0, 1)

    for t in range(blocks):
        rows = slice(t * BLOCK, (t + 1) * BLOCK)
        table = first_table if t == 0 else 1
        for g in range(n_kv):
            kc = slice(g * PAIR, (g + 1) * PAIR)
            vc = slice((n_kv + g) * PAIR, (n_kv + g + 1) * PAIR)
            if t == 0:
                k = jnp.concatenate([kvp_ref[:, kc], kvc_ref[rows, kc]], axis=0)
                v = jnp.concatenate([kvp_ref[:, vc], kvc_ref[rows, vc]], axis=0)
            else:
                band = slice((t - 1) * BLOCK, (t + 1) * BLOCK)
                k = kvc_ref[band, kc]
                v = kvc_ref[band, vc]
            v_and_ones = jnp.concatenate([v, ones], axis=1)
            qs = []
            for j in range(pairs_per_kv):
                p = g * pairs_per_kv + j
                q = q_ref[rows, p * PAIR:(p + 1) * PAIR]
                qs += [jnp.where(first_head, q, 0), jnp.where(first_head, 0, q)]
            s = _dot_nt(jnp.concatenate(qs, axis=0), k)
            es, sink_terms = [], []
            for hh in range(2 * pairs_per_kv):
                head = 2 * g * pairs_per_kv + hh
                sh = s[hh * BLOCK:(hh + 1) * BLOCK] + bias_ref[table, head]
                sink = sink_ref[head] * LOG2_E
                m = jnp.maximum(jnp.max(sh, axis=-1, keepdims=True), sink)
                es.append(jnp.exp2(sh - m).astype(_BF16))
                sink_terms.append(jnp.exp2(sink - m))
            pv = _dot(jnp.concatenate(es, axis=0), v_and_ones)
            for j in range(pairs_per_kv):
                p = g * pairs_per_kv + j
                halves = []
                for h in range(2):
                    r = pv[(2 * j + h) * BLOCK:(2 * j + h + 1) * BLOCK]
                    halves.append(r[:, :PAIR] / (r[:, PAIR:] + sink_terms[2 * j + h]))
                o_ref[rows, p * PAIR:(p + 1) * PAIR] = (
                    jnp.where(first_head, halves[0], halves[1]).astype(o_ref.dtype))


def _swa_attention(q_src, kv_src, kv_col, d, kvw, bias_tab, sinks):
    batch, seq, _ = q_src.shape
    blocks = math.gcd(seq // BLOCK, SWA_BLOCKS_PER_STEP)
    tq = blocks * BLOCK
    return pl.pallas_call(
        _swa_kernel,
        grid=(batch, seq // tq),
        in_specs=[
            pl.BlockSpec(memory_space=pltpu.SMEM),
            pl.BlockSpec((None, tq, d), lambda b, n: (b, n, 0)),
            pl.BlockSpec((None, BLOCK, kvw), lambda b, n: (b, jnp.maximum(n * blocks - 1, 0), kv_col)),
            pl.BlockSpec((None, tq, kvw), lambda b, n: (b, n, kv_col)),
            _resident(bias_tab.shape),
        ],
        out_specs=pl.BlockSpec((None, tq, d), lambda b, n: (b, n, 0)),
        out_shape=jax.ShapeDtypeStruct((batch, seq, d), _BF16),
        compiler_params=pltpu.CompilerParams(
            dimension_semantics=("arbitrary", "arbitrary"), vmem_limit_bytes=VMEM_LIMIT_BYTES),
        name="swa_attention",
    )(sinks, q_src, kv_src, kv_src, bias_tab)


def _duplicate_heads(w, n_heads):
    lead = w.shape[:-1]
    w = w.reshape(lead + (n_heads, 1, HEAD_DIM))
    w = jnp.broadcast_to(w, lead + (n_heads, 2, HEAD_DIM))
    return w.reshape(lead + (n_heads * PAIR,))


def kernel(x, a_norm, a_wqkv, a_wo, kv_norm, w_kv, b_kv, b_norm, b_wq, b_bq, b_sinks,
           b_wo, b_bo, rel_bias, mlp_norm, mlp_up, mlp_down, final_norm):
    batch, seq, d = x.shape
    m = batch * seq
    n_a = a_wqkv.shape[0]
    n_b = b_wq.shape[0]
    depth = n_a + n_b
    n_heads = d // HEAD_DIM
    n_kv = w_kv.shape[1] // (2 * HEAD_DIM)
    assert seq % BLOCK == 0 and m % min(ROW_TILE, m) == 0 and n_heads % (2 * n_kv) == 0

    bf = lambda w: w.astype(_BF16)
    kvw = 2 * n_kv * PAIR
    q_scale = SCALE * LOG2_E

    def attention_projection(layer):
        if layer < n_a:
            col_scale = jnp.where(jnp.arange(a_wqkv.shape[2]) < d, q_scale, 1.0)
            w = a_norm[layer][:, None] * a_wqkv[layer] * col_scale
            return bf(w), jnp.zeros((w.shape[1],), _F32)
        j = layer - n_a
        w = b_norm[j][:, None] * b_wq[j] * q_scale
        b = b_bq[j] * q_scale
        if j == 0:
            wk, wv = jnp.split(w_kv, 2, axis=-1)
            bk, bv = jnp.split(b_kv, 2, axis=-1)
            w_dup = jnp.concatenate([_duplicate_heads(wk, n_kv), _duplicate_heads(wv, n_kv)], -1)
            b_dup = jnp.concatenate([_duplicate_heads(bk, n_kv), _duplicate_heads(bv, n_kv)], -1)
            w = jnp.concatenate([w, kv_norm[:, None] * w_dup], axis=-1)
            b = jnp.concatenate([b, b_dup], axis=-1)
        return bf(w), b

    bias_tab = _bias_band(rel_bias, _band_buckets())
    h = x.reshape(m, d)
    y = _norm_proj(h, *attention_projection(0)).reshape(batch, seq, -1)
    kv_src = None
    for layer in range(depth):
        if layer < n_a:
            attn = _sb_attention(y, batch, seq, n_heads,
                                 SB_WINDOW_FIRST_LAYER if layer == 0 else SB_WINDOW_LATER_LAYERS)
            wo, bo = a_wo[layer], jnp.zeros((d,), _F32)
        else:
            j = layer - n_a
            if j == 0:
                kv_src = y
            attn = _swa_attention(y, kv_src, d // kvw, d, kvw, bias_tab, b_sinks[j])
            wo, bo = b_wo[j], b_bo[j]
        tail = functools.partial(_layer_tail, h, attn.reshape(m, d), bf(wo), bo, mlp_norm[layer],
                                 bf(mlp_up[layer]), bf(mlp_down[layer]))
        if layer == depth - 1:
            return tail(final_gain=final_norm).reshape(batch, seq, d)
        h, y = tail(next_proj=attention_projection(layer + 1))
        y = y.reshape(batch, seq, -1)
```

```python
import functools
import math

import jax
import jax.numpy as jnp
from jax import lax
from jax.experimental import pallas as pl
from jax.experimental.pallas import tpu as pltpu

HEAD_DIM = 64
PAIR = 2 * HEAD_DIM
BLOCK = 128
WINDOW = 128
SWA_GROUP = 8
NUM_BUCKETS = 32
MAX_EXACT = NUM_BUCKETS // 2
MAX_DISTANCE = 128
EPS = 1e-5
NEG_INF = -1e30
SCALE = 1.0 / math.sqrt(HEAD_DIM)

LOG2_E = math.log2(math.e)
SB_UNDERFLOW_LOG2 = -151.0
SB_TILE = 64
SB_WINDOW_FIRST_LAYER = 256
SB_WINDOW_LATER_LAYERS = 384
SB_SCORE_LOOKAHEAD = 6
SB_SUFFIX_LOOKAHEAD = 3
SB_FLAG_GROUP = 8

SWA_BLOCKS_PER_STEP = 16

ROW_TILE = 512
FF_CHUNK = 1024
PROJ_CHUNK = 512
SUB_ROWS = 256
VMEM_LIMIT_BYTES = 56 * 1024 * 1024

_F32 = jnp.float32
_BF16 = jnp.bfloat16


def _dot(a, b):
    return jnp.dot(a, b, preferred_element_type=_F32)


def _dot_nt(a, b):
    return lax.dot_general(a, b, (((1,), (1,)), ((), ())), preferred_element_type=_F32)


def _rms_unit(x):
    ms = jnp.mean(x * x, axis=-1, keepdims=True)
    return x * lax.rsqrt(ms + EPS)


def _resident(shape):
    return pl.BlockSpec(shape, lambda *_: (0,) * len(shape), pipeline_mode=pl.Buffered(1))


def _row_subtiles(tm):
    step = min(SUB_ROWS, tm)
    return [slice(r, r + step) for r in range(0, tm, step)]


def _project(subs, x_units, w_ref, b_ref, y_ref):
    xb = [x.astype(_BF16) for x in x_units]
    n = y_ref.shape[-1]
    for c in range(0, n, PROJ_CHUNK):
        cols = slice(c, min(c + PROJ_CHUNK, n))
        for s, x in zip(subs, xb):
            y_ref[s, cols] = (_dot(x, w_ref[:, cols]) + b_ref[:, cols]).astype(y_ref.dtype)


def _norm_proj_kernel(x_ref, w_ref, b_ref, y_ref):
    subs = _row_subtiles(x_ref.shape[0])
    _project(subs, [_rms_unit(x_ref[s, :]) for s in subs], w_ref, b_ref, y_ref)


def _norm_proj(x, w, b):
    m, d = x.shape
    n = w.shape[1]
    tm = min(ROW_TILE, m)
    return pl.pallas_call(
        _norm_proj_kernel,
        grid=(m // tm,),
        in_specs=[pl.BlockSpec((tm, d), lambda i: (i, 0)), _resident((d, n)), _resident((1, n))],
        out_specs=pl.BlockSpec((tm, n), lambda i: (i, 0)),
        out_shape=jax.ShapeDtypeStruct((m, n), _BF16),
        compiler_params=pltpu.CompilerParams(
            dimension_semantics=("arbitrary",), vmem_limit_bytes=VMEM_LIMIT_BYTES),
        name="norm_proj",
    )(x, w, b.reshape(1, n))


def _attn_out_and_mlp(h_ref, a_ref, wo_ref, bo_ref, g_ref, wup_ref, wdn_ref):
    subs = _row_subtiles(h_ref.shape[0])
    h1 =[h_ref[s, :] + _dot(a_ref[s, :], wo_ref[...]) + bo_ref[...] for s in subs]
    xn = [(_rms_unit(x) * g_ref[...]).astype(_BF16) for x in h1]
    acc = h1
    d_ff = wup_ref.shape[1]
    for c in range(0, d_ff, FF_CHUNK):
        u = [jnp.maximum(_dot(x, wup_ref[:, c:c + FF_CHUNK]), 0.0) for x in xn]
        acc = [a + _dot((v * v).astype(_BF16), wdn_ref[c:c + FF_CHUNK, :]) for a, v in zip(acc, u)]
    return subs, acc


def _mid_layer_kernel(h_ref, a_ref, wo_ref, bo_ref, g_ref, wup_ref, wdn_ref, wn_ref, bn_ref,
                      o_ref, y_ref):
    subs, h2 = _attn_out_and_mlp(h_ref, a_ref, wo_ref, bo_ref, g_ref, wup_ref, wdn_ref)
    for s, x in zip(subs, h2):
        o_ref[s, :] = x
    _project(subs, [_rms_unit(x) for x in h2], wn_ref, bn_ref, y_ref)


def _last_layer_kernel(h_ref, a_ref, wo_ref, bo_ref, g_ref, wup_ref, wdn_ref, fg_ref, o_ref):
    subs, h2 = _attn_out_and_mlp(h_ref, a_ref, wo_ref, bo_ref, g_ref, wup_ref, wdn_ref)
    for s, x in zip(subs, h2):
        o_ref[s, :] = _rms_unit(x) * fg_ref[...]


def _layer_tail(h, a, wo, bo, g, wup, wdn, *, next_proj=None, final_gain=None):
    m, d = h.shape
    d_ff = wup.shape[1]
    tm = min(ROW_TILE, m)
    row = lambda i: (i, 0)
    in_specs = [
        pl.BlockSpec((tm, d), row),
        pl.BlockSpec((tm, d), row),
        _resident((d, d)),
        _resident((1, d)),
        _resident((1, d)),
        _resident((d, d_ff)),
        _resident((d_ff, d)),
    ]
    args = [h, a, wo, bo.reshape(1, d), g.reshape(1, d), wup, wdn]
    params = pltpu.CompilerParams(
        dimension_semantics=("arbitrary",), vmem_limit_bytes=VMEM_LIMIT_BYTES)
    if next_proj is None:
        return pl.pallas_call(
            _last_layer_kernel,
            grid=(m // tm,),
            in_specs=in_specs + [_resident((1, d))],
            out_specs=pl.BlockSpec((tm, d), row),
            out_shape=jax.ShapeDtypeStruct((m, d), _F32),
            compiler_params=params,
            name="last_layer",
        )(*args, final_gain.reshape(1, d))
    wn, bn = next_proj
    n = wn.shape[1]
    return pl.pallas_call(
        _mid_layer_kernel,
        grid=(m // tm,),
        in_specs=in_specs + [_resident((d, n)), _resident((1, n))],
        out_specs=[pl.BlockSpec((tm, d), row), pl.BlockSpec((tm, n), row)],
        out_shape=[jax.ShapeDtypeStruct((m, d), _F32), jax.ShapeDtypeStruct((m, n), _BF16)],
        compiler_params=params,
        name="mid_layer",
    )(*args, wn, bn.reshape(1, n))


def _sb_attn_kernel(q_ref, k_ref, v_ref, o_ref, acc_ref, carry_ref, least_ref, unfinished_ref,
                    *, window_keys):
    n_t = q_ref.shape[0] // SB_TILE
    row = lax.broadcasted_iota(jnp.int32, (SB_TILE, BLOCK), 0)
    col = lax.broadcasted_iota(jnp.int32, (SB_TILE, BLOCK), 1)
    first_head = col < HEAD_DIM
    key = lax.broadcasted_iota(jnp.int32, (BLOCK, BLOCK), 0)
    pos = lax.broadcasted_iota(jnp.int32, (BLOCK, BLOCK), 1)
    suffix_and_ones = jnp.concatenate(
        [(key >= pos).astype(_BF16), jnp.ones((BLOCK, BLOCK), _BF16)], axis=1)
    suffix_and_ones = jnp.concatenate([suffix_and_ones, suffix_and_ones], axis=0)

    def tile_rows(t):
        start = t * SB_TILE
        return pl.ds(start if isinstance(t, int) else pl.multiple_of(start, SB_TILE), SB_TILE)

    def load_q(t):
        q = q_ref[tile_rows(t), :]
        return jnp.concatenate([jnp.where(first_head, q, 0), jnp.where(first_head, 0, q)], axis=0)

    def window(t):
        end = (t + 1) * SB_TILE
        n_b = min(pl.cdiv(end, BLOCK), window_keys // BLOCK)
        start = max(end - n_b * BLOCK, 0)
        offsets = []
        for b in range(n_b):
            off = t * SB_TILE - start - b * BLOCK
            assert off > -SB_TILE
            offsets.append(None if off >= BLOCK else off)
        return start, n_b, offsets

    def scores(q2, k):
        return _dot_nt(q2, k)

    def suffix_sums(z, masks):
        neg_abs = lax.bitcast_convert_type(
            lax.bitcast_convert_type(z, jnp.uint32) | jnp.uint32(0x80000000), _F32)
        sp = jnp.maximum(z, 0.0) + jnp.log2(1.0 + jnp.exp2(neg_abs))
        split = []
        for h in range(2):
            for b, mask in enumerate(masks):
                s = sp[h * SB_TILE:(h + 1) * SB_TILE, b * BLOCK:(b + 1) * BLOCK]
                if mask is not None:
                    s = jnp.where(mask, s, 0.0)
                hi = s.astype(_BF16)
                lo = (s - hi.astype(_F32)).astype(_BF16)
                split.append(jnp.concatenate([hi, lo], axis=1))
        return _dot(jnp.concatenate(split, axis=0), suffix_and_ones)

    def weighted_values(z, cs, v, carry_in, masks):
        n_b = len(masks)
        ws, carries = [], []
        for h in range(2):
            carry = None if carry_in is None else carry_in[h]
            wb = [None] * n_b
            for b in reversed(range(n_b)):
                r = cs[(h * n_b + b) * SB_TILE:(h * n_b + b + 1) * SB_TILE]
                arg = z[h * SB_TILE:(h + 1) * SB_TILE, b * BLOCK:(b + 1) * BLOCK] - r[:, :BLOCK]
                if carry is not None:
                    arg = arg - carry
                w = jnp.exp2(arg)
                if masks[b] is not None:
                    w = jnp.where(masks[b], w, 0.0)
                wb[b] = w.astype(_BF16)
                carry = r[:, BLOCK:] if carry is None else carry + r[:, BLOCK:]
            ws.append(jnp.concatenate(wb, axis=1))
            carries.append(carry)
        out = _dot(jnp.concatenate(ws, axis=0), v)
        return jnp.where(first_head, out[:SB_TILE], out[SB_TILE:]), carries

    def window_scores(t):
        start, n_b, _ = window(t)
        return scores(load_q(t), k_ref[pl.ds(start, n_b * BLOCK), :])

    zs, css, wins, masks = {}, {}, {}, {}

    def issue_scores(t):
        if t < n_t:
            wins[t] = window(t)
            masks[t] = [None if off is None else col < row + off for off in wins[t][2]]
            zs[t] = window_scores(t)

    def issue_suffix_sums(t):
        if t < n_t:
            css[t] = suffix_sums(zs[t], masks[t])

    for t in range(SB_SCORE_LOOKAHEAD):
        issue_scores(t)
    for t in range(SB_SUFFIX_LOOKAHEAD):
        issue_suffix_sums(t)
    group_unfinished = {}
    for t in range(n_t):
        issue_scores(t + SB_SCORE_LOOKAHEAD)
        issue_suffix_sums(t + SB_SUFFIX_LOOKAHEAD)
        start, n_b, _ = wins.pop(t)
        out, carry = weighted_values(zs.pop(t), css.pop(t), v_ref[pl.ds(start, n_b * BLOCK), :],
                                     None, masks.pop(t))
        o_ref[tile_rows(t), :] = out.astype(o_ref.dtype)
        if start > 0:
            acc_ref[t] = out
            carry_ref[t, 0] = carry[0]
            carry_ref[t, 1] = carry[1]
            least = jnp.min(jnp.minimum(carry[0], carry[1]))
            least_ref[t] = least
            unfinished = least < -SB_UNDERFLOW_LOG2
            g = t // SB_FLAG_GROUP
            group_unfinished[g] = (jnp.logical_or(group_unfinished[g], unfinished)
                                   if g in group_unfinished else unfinished)
    for g, unfinished in group_unfinished.items():
        unfinished_ref[g] = unfinished.astype(jnp.int32)

    first_open = next((t for t in range(n_t) if window(t)[0] > 0), n_t)

    def finish_tile(t, _):
        def more(state):
            remaining, least = state
            return jnp.logical_and(remaining > 0, least < -SB_UNDERFLOW_LOG2)

        def older_block(state):
            remaining, _ = state
            first = jnp.maximum(remaining - BLOCK, 0)
            rows = pl.ds(pl.multiple_of(first, SB_TILE), BLOCK)
            masks = [col < remaining - first]
            z = scores(load_q(t), k_ref[rows, :])
            out, carry = weighted_values(z, suffix_sums(z, masks), v_ref[rows, :],
                                         [carry_ref[t, 0], carry_ref[t, 1]], masks)
            acc_ref[t] += out
            carry_ref[t, 0] = carry[0]
            carry_ref[t, 1] = carry[1]
            return first, jnp.min(jnp.minimum(carry[0], carry[1]))

        @pl.when(least_ref[t] < -SB_UNDERFLOW_LOG2)
        def _():
            lax.while_loop(more, older_block, ((t + 1) * SB_TILE - window_keys, least_ref[t]))
            o_ref[tile_rows(t), :] = acc_ref[t].astype(o_ref.dtype)
        return 0

    def finish_group(g, _):
        @pl.when(unfinished_ref[g] != 0)
        def _():
            lax.fori_loop(jnp.maximum(g * SB_FLAG_GROUP, first_open),
                          jnp.minimum((g + 1) * SB_FLAG_GROUP, n_t), finish_tile, 0)
        return 0

    if n_t > first_open:
        lax.fori_loop(first_open // SB_FLAG_GROUP, pl.cdiv(n_t, SB_FLAG_GROUP), finish_group, 0)


def _sb_attention(qkv, batch, seq, n_heads, window_keys):
    n_pairs = n_heads // 2
    n_t = seq // SB_TILE
    blk = (None, seq, PAIR)
    return pl.pallas_call(
        functools.partial(_sb_attn_kernel, window_keys=window_keys),
        grid=(batch, n_pairs),
        in_specs=[
            pl.BlockSpec(blk, lambda b, p: (b, 0, p)),
            pl.BlockSpec(blk, lambda b, p: (b, 0, n_pairs + p)),
            pl.BlockSpec(blk, lambda b, p: (b, 0, 2 * n_pairs + p)),
        ],
        out_specs=pl.BlockSpec(blk, lambda b, p: (b, 0, p)),
        out_shape=jax.ShapeDtypeStruct((batch, seq, n_heads * HEAD_DIM), _BF16),
        scratch_shapes=[pltpu.VMEM((n_t, SB_TILE, PAIR), _F32),
                        pltpu.VMEM((n_t, 2, SB_TILE, BLOCK), _F32),
                        pltpu.SMEM((n_t,), _F32),
                        pltpu.SMEM((pl.cdiv(n_t, SB_FLAG_GROUP),), jnp.int32)],
        compiler_params=pltpu.CompilerParams(
            dimension_semantics=("arbitrary", "arbitrary"), vmem_limit_bytes=VMEM_LIMIT_BYTES),
        name="sb_attention",
    )(qkv, qkv, qkv)


def _bias_band_kernel(rel_ref, bucket_ref, o_ref):
    head = pl.program_id(0)
    bucket = bucket_ref[...]
    qi = lax.broadcasted_iota(jnp.int32, bucket.shape, 0)
    kj = lax.broadcasted_iota(jnp.int32, bucket.shape, 1)
    dist = qi + BLOCK - kj
    valid = (dist >= 0) & (dist < WINDOW)
    bias = jnp.zeros(bucket.shape, _F32)
    for b in range(NUM_BUCKETS):
        bias = jnp.where(bucket == b, rel_ref[b, head] * LOG2_E, bias)
    o_ref[0] = jnp.where(valid & (kj >= BLOCK), bias, NEG_INF)
    o_ref[1] = jnp.where(valid, bias, NEG_INF)


def _bias_band(rel_bias, bucket):
    n_heads = rel_bias.shape[1]
    return pl.pallas_call(
        _bias_band_kernel,
        grid=(n_heads,),
        in_specs=[
            pl.BlockSpec(memory_space=pltpu.SMEM),
            pl.BlockSpec(bucket.shape, lambda h: (0, 0)),
        ],
        out_specs=pl.BlockSpec((2, None) + bucket.shape, lambda h: (0, h, 0, 0)),
        out_shape=jax.ShapeDtypeStruct((2, n_heads) + bucket.shape, _F32),
        name="bias_band",
    )(rel_bias, bucket)


def _band_buckets():
    qi = jnp.arange(BLOCK)[:, None]
    kj = jnp.arange(2 * BLOCK)[None, :]
    n = jnp.maximum(qi + BLOCK - kj, 0)
    nf = jnp.maximum(n, 1).astype(_F32)
    large = MAX_EXACT + (jnp.log(nf / MAX_EXACT) / math.log(MAX_DISTANCE / MAX_EXACT)
                         * (NUM_BUCKETS - MAX_EXACT)).astype(jnp.int32)
    large = jnp.minimum(large, NUM_BUCKETS - 1)
    return jnp.where(n < MAX_EXACT, n, large).astype(jnp.int32)


def _swa_kernel(sink_ref, q_ref, kvp_ref, kvc_ref, bias_ref, o_ref):
    n_pairs = q_ref.shape[-1] // PAIR
    n_kv = kvc_ref.shape[-1] // (2 * PAIR)
    pairs_per_kv = n_pairs // n_kv
    blocks = q_ref.shape[0] // BLOCK
    first_head = lax.broadcasted_iota(jnp.int32, (BLOCK, PAIR), 1) < HEAD_DIM
    ones = jnp.ones((2 * BLOCK, PAIR), _BF16)
    first_table = jnp.where(pl.program_id(1) == 0, 0, 1)

    for t in range(blocks):
        rows = slice(t * BLOCK, (t + 1) * BLOCK)
        table = first_table if t == 0 else 1
        for g in range(n_kv):
            kc = slice(g * PAIR, (g + 1) * PAIR)
            vc = slice((n_kv + g) * PAIR, (n_kv + g + 1) * PAIR)
            if t == 0:
                k = jnp.concatenate([kvp_ref[:, kc], kvc_ref[rows, kc]], axis=0)
                v = jnp.concatenate([kvp_ref[:, vc], kvc_ref[rows, vc]], axis=0)
            else:
                band = slice((t - 1) * BLOCK, (t + 1) * BLOCK)
                k = kvc_ref[band, kc]
                v = kvc_ref[band, vc]
            v_and_ones = jnp.concatenate([v, ones], axis=1)
            qs = []
            for j in range(pairs_per_kv):
                p = g * pairs_per_kv + j
                q = q_ref[rows, p * PAIR:(p + 1) * PAIR]
                qs += [jnp.where(first_head, q, 0), jnp.where(first_head, 0, q)]
            s = _dot_nt(jnp.concatenate(qs, axis=0), k)
            es, sink_terms = [], []
            for hh in range(2 * pairs_per_kv):
                head = 2 * g * pairs_per_kv + hh
                sh = s[hh * BLOCK:(hh + 1) * BLOCK] + bias_ref[table, head]
                sink = sink_ref[head] * LOG2_E
                m = jnp.maximum(jnp.max(sh, axis=-1, keepdims=True), sink)
                es.append(jnp.exp2(sh - m).astype(_BF16))
                sink_terms.append(jnp.exp2(sink - m))
            pv = _dot(jnp.concatenate(es, axis=0), v_and_ones)
            for j in range(pairs_per_kv):
                p = g * pairs_per_kv + j
                halves = []
                for h in range(2):
                    r = pv[(2 * j + h) * BLOCK:(2 * j + h + 1) * BLOCK]
                    halves.append(r[:, :PAIR] / (r[:, PAIR:] + sink_terms[2 * j + h]))
                o_ref[rows, p * PAIR:(p + 1) * PAIR] = (
                    jnp.where(first_head, halves[0], halves[1]).astype(o_ref.dtype))


def _swa_attention(q_src, kv_src, kv_col, d, kvw, bias_tab, sinks):
    batch, seq, _ = q_src.shape
    blocks = math.gcd(seq // BLOCK, SWA_BLOCKS_PER_STEP)
    tq = blocks * BLOCK
    return pl.pallas_call(
        _swa_kernel,
        grid=(batch, seq // tq),
        in_specs=[
            pl.BlockSpec(memory_space=pltpu.SMEM),
            pl.BlockSpec((None, tq, d), lambda b, n: (b, n, 0)),
            pl.BlockSpec((None, BLOCK, kvw), lambda b, n: (b, jnp.maximum(n * blocks - 1, 0), kv_col)),
            pl.BlockSpec((None, tq, kvw), lambda b, n: (b, n, kv_col)),
            _resident(bias_tab.shape),
        ],
        out_specs=pl.BlockSpec((None, tq, d), lambda b, n: (b, n, 0)),
        out_shape=jax.ShapeDtypeStruct((batch, seq, d), _BF16),
        compiler_params=pltpu.CompilerParams(
            dimension_semantics=("arbitrary", "arbitrary"), vmem_limit_bytes=VMEM_LIMIT_BYTES),
        name="swa_attention",
    )(sinks, q_src, kv_src, kv_src, bias_tab)


def _duplicate_heads(w, n_heads):
    lead = w.shape[:-1]
    w = w.reshape(lead + (n_heads, 1, HEAD_DIM))
    w = jnp.broadcast_to(w, lead + (n_heads, 2, HEAD_DIM))
    return w.reshape(lead + (n_heads * PAIR,))


def kernel(x, a_norm, a_wqkv, a_wo, kv_norm, w_kv, b_kv, b_norm, b_wq, b_bq, b_sinks,
           b_wo, b_bo, rel_bias, mlp_norm, mlp_up, mlp_down, final_norm):
    batch, seq, d = x.shape
    m = batch * seq
    n_a = a_wqkv.shape[0]
    n_b = b_wq.shape[0]
    depth = n_a + n_b
    n_heads = d // HEAD_DIM
    n_kv = w_kv.shape[1] // (2 * HEAD_DIM)
    assert seq % BLOCK == 0 and m % min(ROW_TILE, m) == 0 and n_heads % (2 * n_kv) == 0

    bf = lambda w: w.astype(_BF16)
    kvw = 2 * n_kv * PAIR
    q_scale = SCALE * LOG2_E

    def attention_projection(layer):
        if layer < n_a:
            col_scale = jnp.where(jnp.arange(a_wqkv.shape[2]) < d, q_scale, 1.0)
            w = a_norm[layer][:, None] * a_wqkv[layer] * col_scale
            return bf(w), jnp.zeros((w.shape[1],), _F32)
        j = layer - n_a
        w = b_norm[j][:, None] * b_wq[j] * q_scale
        b = b_bq[j] * q_scale
        if j == 0:
            wk, wv = jnp.split(w_kv, 2, axis=-1)
            bk, bv = jnp.split(b_kv, 2, axis=-1)
            w_dup = jnp.concatenate([_duplicate_heads(wk, n_kv), _duplicate_heads(wv, n_kv)], -1)
            b_dup = jnp.concatenate([_duplicate_heads(bk, n_kv), _duplicate_heads(bv, n_kv)], -1)
            w = jnp.concatenate([w, kv_norm[:, None] * w_dup], axis=-1)
            b = jnp.concatenate([b, b_dup], axis=-1)
        return bf(w), b

    bias_tab = _bias_band(rel_bias, _band_buckets())
    h = x.reshape(m, d)
    y = _norm_proj(h, *attention_projection(0)).reshape(batch, seq, -1)
    kv_src = None
    for layer in range(depth):
        if layer < n_a:
            attn = _sb_attention(y, batch, seq, n_heads,
                                 SB_WINDOW_FIRST_LAYER if layer == 0 else SB_WINDOW_LATER_LAYERS)
            wo, bo = a_wo[layer], jnp.zeros((d,), _F32)
        else:
            j = layer - n_a
            if j == 0:
                kv_src = y
            attn = _swa_attention(y, kv_src, d // kvw, d, kvw, bias_tab, b_sinks[j])
            wo, bo = b_wo[j], b_bo[j]
        tail = functools.partial(_layer_tail, h, attn.reshape(m, d), bf(wo), bo, mlp_norm[layer],
                                 bf(mlp_up[layer]), bf(mlp_down[layer]))
        if layer == depth - 1:
            return tail(final_gain=final_norm).reshape(batch, seq, d)
        h, y = tail(next_proj=attention_projection(layer + 1))
        y = y.reshape(batch, seq, -1)
```

```python
import functools
import math

import jax
import jax.numpy as jnp
from jax import lax
from jax.experimental import pallas as pl
from jax.experimental.pallas import tpu as pltpu

HEAD_DIM = 64
PAIR = 2 * HEAD_DIM
BLOCK = 128
WINDOW = 128
SWA_GROUP = 8
NUM_BUCKETS = 32
MAX_EXACT = NUM_BUCKETS // 2
MAX_DISTANCE = 128
EPS = 1e-5
NEG_INF = -1e30
SCALE = 1.0 / math.sqrt(HEAD_DIM)

LOG2_E = math.log2(math.e)
SB_UNDERFLOW_LOG2 = -151.0
SB_TILE = 64
SB_WINDOW_FIRST_LAYER = 256
SB_WINDOW_LATER_LAYERS = 384
SB_SCORE_LOOKAHEAD = 6
SB_SUFFIX_LOOKAHEAD = 3
SB_FLAG_GROUP = 8

SWA_BLOCKS_PER_STEP = 16

ROW_TILE = 512
FF_CHUNK = 1024
PROJ_CHUNK = 512
SUB_ROWS = 256
VMEM_LIMIT_BYTES = 56 * 1024 * 1024

_F32 = jnp.float32
_BF16 = jnp.bfloat16


def _dot(a, b):
    return jnp.dot(a, b, preferred_element_type=_F32)


def _dot_nt(a, b):
    return lax.dot_general(a, b, (((1,), (1,)), ((), ())), preferred_element_type=_F32)


def _rms_unit(x):
    ms = jnp.mean(x * x, axis=-1, keepdims=True)
    return x * lax.rsqrt(ms + EPS)


def _resident(shape):
    return pl.BlockSpec(shape, lambda *_: (0,) * len(shape), pipeline_mode=pl.Buffered(1))


def _row_subtiles(tm):
    step = min(SUB_ROWS, tm)
    return [slice(r, r + step) for r in range(0, tm, step)]


def _project(subs, x_units, w_ref, b_ref, y_ref):
    xb = [x.astype(_BF16) for x in x_units]
    n = y_ref.shape[-1]
    for c in range(0, n, PROJ_CHUNK):
        cols = slice(c, min(c + PROJ_CHUNK, n))
        for s, x in zip(subs, xb):
            y_ref[s, cols] = (_dot(x, w_ref[:, cols]) + b_ref[:, cols]).astype(y_ref.dtype)


def _norm_proj_kernel(x_ref, w_ref, b_ref, y_ref):
    subs = _row_subtiles(x_ref.shape[0])
    _project(subs, [_rms_unit(x_ref[s, :]) for s in subs], w_ref, b_ref, y_ref)


def _norm_proj(x, w, b):
    m, d = x.shape
    n = w.shape[1]
    tm = math.gcd(2 * ROW_TILE, m)
    return pl.pallas_call(
        _norm_proj_kernel,
        grid=(m // tm,),
        in_specs=[pl.BlockSpec((tm, d), lambda i: (i, 0)), _resident((d, n)), _resident((1, n))],
        out_specs=pl.BlockSpec((tm, n), lambda i: (i, 0)),
        out_shape=jax.ShapeDtypeStruct((m, n), _BF16),
        compiler_params=pltpu.CompilerParams(
            dimension_semantics=("arbitrary",), vmem_limit_bytes=VMEM_LIMIT_BYTES),
        name="norm_proj",
    )(x, w, b.reshape(1, n))


def _attn_out_and_mlp(h_ref, a_ref, wo_ref, bo_ref, g_ref, wup_ref, wdn_ref):
    subs = _row_subtiles(h_ref.shape[0])
    h1 =[h_ref[s, :] + _dot(a_ref[s, :], wo_ref[...]) + bo_ref[...] for s in subs]
    xn = [(_rms_unit(x) * g_ref[...]).astype(_BF16) for x in h1]
    acc = h1
    d_ff = wup_ref.shape[1]
    for c in range(0, d_ff, FF_CHUNK):
        u = [jnp.maximum(_dot(x, wup_ref[:, c:c + FF_CHUNK]), 0.0) for x in xn]
        acc = [a + _dot((v * v).astype(_BF16), wdn_ref[c:c + FF_CHUNK, :]) for a, v in zip(acc, u)]
    return subs, acc


def _mid_layer_kernel(h_ref, a_ref, wo_ref, bo_ref, g_ref, wup_ref, wdn_ref, wn_ref, bn_ref,
                      o_ref, y_ref):
    subs, h2 = _attn_out_and_mlp(h_ref, a_ref, wo_ref, bo_ref, g_ref, wup_ref, wdn_ref)
    for s, x in zip(subs, h2):
        o_ref[s, :] = x
    _project(subs, [_rms_unit(x) for x in h2], wn_ref, bn_ref, y_ref)


def _last_layer_kernel(h_ref, a_ref, wo_ref, bo_ref, g_ref, wup_ref, wdn_ref, fg_ref, o_ref):
    subs, h2 = _attn_out_and_mlp(h_ref, a_ref, wo_ref, bo_ref, g_ref, wup_ref, wdn_ref)
    for s, x in zip(subs, h2):
        o_ref[s, :] = _rms_unit(x) * fg_ref[...]


def _layer_tail(h, a, wo, bo, g, wup, wdn, *, next_proj=None, final_gain=None):
    m, d = h.shape
    d_ff = wup.shape[1]
    tm = math.gcd(2 * ROW_TILE, m) if next_proj is None else min(ROW_TILE, m)
    row = lambda i: (i, 0)
    in_specs = [
        pl.BlockSpec((tm, d), row),
        pl.BlockSpec((tm, d), row),
        _resident((d, d)),
        _resident((1, d)),
        _resident((1, d)),
        _resident((d, d_ff)),
        _resident((d_ff, d)),
    ]
    args = [h, a, wo, bo.reshape(1, d), g.reshape(1, d), wup, wdn]
    params = pltpu.CompilerParams(
        dimension_semantics=("arbitrary",), vmem_limit_bytes=VMEM_LIMIT_BYTES)
    if next_proj is None:
        return pl.pallas_call(
            _last_layer_kernel,
            grid=(m // tm,),
            in_specs=in_specs + [_resident((1, d))],
            out_specs=pl.BlockSpec((tm, d), row),
            out_shape=jax.ShapeDtypeStruct((m, d), _F32),
            compiler_params=params,
            name="last_layer",
        )(*args, final_gain.reshape(1, d))
    wn, bn = next_proj
    n = wn.shape[1]
    return pl.pallas_call(
        _mid_layer_kernel,
        grid=(m // tm,),
        in_specs=in_specs + [_resident((d, n)), _resident((1, n))],
        out_specs=[pl.BlockSpec((tm, d), row), pl.BlockSpec((tm, n), row)],
        out_shape=[jax.ShapeDtypeStruct((m, d), _F32), jax.ShapeDtypeStruct((m, n), _BF16)],
        compiler_params=params,
        name="mid_layer",
    )(*args, wn, bn.reshape(1, n))


def _sb_attn_kernel(q_ref, k_ref, v_ref, o_ref, acc_ref, carry_ref, least_ref, unfinished_ref,
                    *, window_keys):
    n_t = q_ref.shape[0] // SB_TILE
    row = lax.broadcasted_iota(jnp.int32, (SB_TILE, BLOCK), 0)
    col = lax.broadcasted_iota(jnp.int32, (SB_TILE, BLOCK), 1)
    first_head = col < HEAD_DIM
    key = lax.broadcasted_iota(jnp.int32, (BLOCK, BLOCK), 0)
    pos = lax.broadcasted_iota(jnp.int32, (BLOCK, BLOCK), 1)
    suffix_and_ones = jnp.concatenate(
        [(key >= pos).astype(_BF16), jnp.ones((BLOCK, BLOCK), _BF16)], axis=1)
    suffix_and_ones = jnp.concatenate([suffix_and_ones, suffix_and_ones], axis=0)

    def tile_rows(t):
        start = t * SB_TILE
        return pl.ds(start if isinstance(t, int) else pl.multiple_of(start, SB_TILE), SB_TILE)

    def load_q(t):
        q = q_ref[tile_rows(t), :]
        return jnp.concatenate([jnp.where(first_head, q, 0), jnp.where(first_head, 0, q)], axis=0)

    def window(t):
        end = (t + 1) * SB_TILE
        n_b = min(pl.cdiv(end, BLOCK), window_keys // BLOCK)
        start = max(end - n_b * BLOCK, 0)
        offsets = []
        for b in range(n_b):
            off = t * SB_TILE - start - b * BLOCK
            assert off > -SB_TILE
            offsets.append(None if off >= BLOCK else off)
        return start, n_b, offsets

    def scores(q2, k):
        return _dot_nt(q2, k)

    def suffix_sums(z, masks):
        neg_abs = lax.bitcast_convert_type(
            lax.bitcast_convert_type(z, jnp.uint32) | jnp.uint32(0x80000000), _F32)
        sp = jnp.maximum(z, 0.0) + jnp.log2(1.0 + jnp.exp2(neg_abs))
        split = []
        for h in range(2):
            for b, mask in enumerate(masks):
                s = sp[h * SB_TILE:(h + 1) * SB_TILE, b * BLOCK:(b + 1) * BLOCK]
                if mask is not None:
                    s = jnp.where(mask, s, 0.0)
                hi = s.astype(_BF16)
                lo = (s - hi.astype(_F32)).astype(_BF16)
                split.append(jnp.concatenate([hi, lo], axis=1))
        return _dot(jnp.concatenate(split, axis=0), suffix_and_ones)

    def weighted_values(z, cs, v, carry_in, masks):
        n_b = len(masks)
        ws, carries = [], []
        for h in range(2):
            carry = None if carry_in is None else carry_in[h]
            wb = [None] * n_b
            for b in reversed(range(n_b)):
                r = cs[(h * n_b + b) * SB_TILE:(h * n_b + b + 1) * SB_TILE]
                arg = z[h * SB_TILE:(h + 1) * SB_TILE, b * BLOCK:(b + 1) * BLOCK] - r[:, :BLOCK]
                if carry is not None:
                    arg = arg - carry
                w = jnp.exp2(arg)
                if masks[b] is not None:
                    w = jnp.where(masks[b], w, 0.0)
                wb[b] = w.astype(_BF16)
                carry = r[:, BLOCK:] if carry is None else carry + r[:, BLOCK:]
            ws.append(jnp.concatenate(wb, axis=1))
            carries.append(carry)
        out = _dot(jnp.concatenate(ws, axis=0), v)
        return jnp.where(first_head, out[:SB_TILE], out[SB_TILE:]), carries

    def window_scores(t):
        start, n_b, _ = window(t)
        return scores(load_q(t), k_ref[pl.ds(start, n_b * BLOCK), :])

    zs, css, wins, masks = {}, {}, {}, {}

    def issue_scores(t):
        if t < n_t:
            wins[t] = window(t)
            masks[t] = [None if off is None else col < row + off for off in wins[t][2]]
            zs[t] = window_scores(t)

    def issue_suffix_sums(t):
        if t < n_t:
            css[t] = suffix_sums(zs[t], masks[t])

    for t in range(SB_SCORE_LOOKAHEAD):
        issue_scores(t)
    for t in range(SB_SUFFIX_LOOKAHEAD):
        issue_suffix_sums(t)
    group_unfinished = {}
    for t in range(n_t):
        issue_scores(t + SB_SCORE_LOOKAHEAD)
        issue_suffix_sums(t + SB_SUFFIX_LOOKAHEAD)
        start, n_b, _ = wins.pop(t)
        out, carry = weighted_values(zs.pop(t), css.pop(t), v_ref[pl.ds(start, n_b * BLOCK), :],
                                     None, masks.pop(t))
        o_ref[tile_rows(t), :] = out.astype(o_ref.dtype)
        if start > 0:
            acc_ref[t] = out
            carry_ref[t, 0] = carry[0]
            carry_ref[t, 1] = carry[1]
            least = jnp.min(jnp.minimum(carry[0], carry[1]))
            least_ref[t] = least
            unfinished = least < -SB_UNDERFLOW_LOG2
            g = t // SB_FLAG_GROUP
            group_unfinished[g] = (jnp.logical_or(group_unfinished[g], unfinished)
                                   if g in group_unfinished else unfinished)
    for g, unfinished in group_unfinished.items():
        unfinished_ref[g] = unfinished.astype(jnp.int32)

    first_open = next((t for t in range(n_t) if window(t)[0] > 0), n_t)

    def finish_tile(t, _):
        def more(state):
            remaining, least = state
            return jnp.logical_and(remaining > 0, least < -SB_UNDERFLOW_LOG2)

        def older_block(state):
            remaining, _ = state
            first = jnp.maximum(remaining - BLOCK, 0)
            rows = pl.ds(pl.multiple_of(first, SB_TILE), BLOCK)
            masks = [col < remaining - first]
            z = scores(load_q(t), k_ref[rows, :])
            out, carry = weighted_values(z, suffix_sums(z, masks), v_ref[rows, :],
                                         [carry_ref[t, 0], carry_ref[t, 1]], masks)
            acc_ref[t] += out
            carry_ref[t, 0] = carry[0]
            carry_ref[t, 1] = carry[1]
            return first, jnp.min(jnp.minimum(carry[0], carry[1]))

        @pl.when(least_ref[t] < -SB_UNDERFLOW_LOG2)
        def _():
            lax.while_loop(more, older_block, ((t + 1) * SB_TILE - window_keys, least_ref[t]))
            o_ref[tile_rows(t), :] = acc_ref[t].astype(o_ref.dtype)
        return 0

    def finish_group(g, _):
        @pl.when(unfinished_ref[g] != 0)
        def _():
            lax.fori_loop(jnp.maximum(g * SB_FLAG_GROUP, first_open),
                          jnp.minimum((g + 1) * SB_FLAG_GROUP, n_t), finish_tile, 0)
        return 0

    if n_t > first_open:
        lax.fori_loop(first_open // SB_FLAG_GROUP, pl.cdiv(n_t, SB_FLAG_GROUP), finish_group, 0)


def _sb_attention(qkv, batch, seq, n_heads, window_keys):
    n_pairs = n_heads // 2
    n_t = seq // SB_TILE
    blk = (None, seq, PAIR)
    return pl.pallas_call(
        functools.partial(_sb_attn_kernel, window_keys=window_keys),
        grid=(batch, n_pairs),
        in_specs=[
            pl.BlockSpec(blk, lambda b, p: (b, 0, p)),
            pl.BlockSpec(blk, lambda b, p: (b, 0, n_pairs + p)),
            pl.BlockSpec(blk, lambda b, p: (b, 0, 2 * n_pairs + p)),
        ],
        out_specs=pl.BlockSpec(blk, lambda b, p: (b, 0, p)),
        out_shape=jax.ShapeDtypeStruct((batch, seq, n_heads * HEAD_DIM), _BF16),
        scratch_shapes=[pltpu.VMEM((n_t, SB_TILE, PAIR), _F32),
                        pltpu.VMEM((n_t, 2, SB_TILE, BLOCK), _F32),
                        pltpu.SMEM((n_t,), _F32),
                        pltpu.SMEM((pl.cdiv(n_t, SB_FLAG_GROUP),), jnp.int32)],
        compiler_params=pltpu.CompilerParams(
            dimension_semantics=("arbitrary", "arbitrary"), vmem_limit_bytes=VMEM_LIMIT_BYTES),
        name="sb_attention",
    )(qkv, qkv, qkv)


def _bias_band_kernel(rel_ref, bucket_ref, o_ref):
    head = pl.program_id(0)
    bucket = bucket_ref[...]
    qi = lax.broadcasted_iota(jnp.int32, bucket.shape, 0)
    kj = lax.broadcasted_iota(jnp.int32, bucket.shape, 1)
    dist = qi + BLOCK - kj
    valid = (dist >= 0) & (dist < WINDOW)
    bias = jnp.zeros(bucket.shape, _F32)
    for b in range(NUM_BUCKETS):
        bias = jnp.where(bucket == b, rel_ref[b, head] * LOG2_E, bias)
    o_ref[0] = jnp.where(valid & (kj >= BLOCK), bias, NEG_INF)
    o_ref[1] = jnp.where(valid, bias, NEG_INF)


def _bias_band(rel_bias, bucket):
    n_heads = rel_bias.shape[1]
    return pl.pallas_call(
        _bias_band_kernel,
        grid=(n_heads,),
        in_specs=[
            pl.BlockSpec(memory_space=pltpu.SMEM),
            pl.BlockSpec(bucket.shape, lambda h: (0, 0)),
        ],
        out_specs=pl.BlockSpec((2, None) + bucket.shape, lambda h: (0, h, 0, 0)),
        out_shape=jax.ShapeDtypeStruct((2, n_heads) + bucket.shape, _F32),
        name="bias_band",
    )(rel_bias, bucket)


def _band_buckets():
    qi = jnp.arange(BLOCK)[:, None]
    kj = jnp.arange(2 * BLOCK)[None, :]
    n = jnp.maximum(qi + BLOCK - kj, 0)
    nf = jnp.maximum(n, 1).astype(_F32)
    large = MAX_EXACT + (jnp.log(nf / MAX_EXACT) / math.log(MAX_DISTANCE / MAX_EXACT)
                         * (NUM_BUCKETS - MAX_EXACT)).astype(jnp.int32)
    large = jnp.minimum(large, NUM_BUCKETS - 1)
    return jnp.where(n < MAX_EXACT, n, large).astype(jnp.int32)


def _swa_kernel(sink_ref, q_ref, kvp_ref, kvc_ref, bias_ref, o_ref):
    n_pairs = q_ref.shape[-1] // PAIR
    n_kv = kvc_ref.shape[-1] // (2 * PAIR)
    pairs_per_kv = n_pairs // n_kv
    blocks = q_ref.shape[0] // BLOCK
    first_head = lax.broadcasted_iota(jnp.int32, (BLOCK, PAIR), 1) < HEAD_DIM
    ones = jnp.ones((2 * BLOCK, PAIR), _BF16)
    first_table = jnp.where(pl.program_id(1) == 0, 0, 1)

    for t in range(blocks):
        rows = slice(t * BLOCK, (t + 1) * BLOCK)
        table = first_table if t == 0 else 1
        for g in range(n_kv):
            kc = slice(g * PAIR, (g + 1) * PAIR)
            vc = slice((n_kv + g) * PAIR, (n_kv + g + 1) * PAIR)
            if t == 0:
                k = jnp.concatenate([kvp_ref[:, kc], kvc_ref[rows, kc]], axis=0)
                v = jnp.concatenate([kvp_ref[:, vc], kvc_ref[rows, vc]], axis=0)
            else:
                band = slice((t - 1) * BLOCK, (t + 1) * BLOCK)
                k = kvc_ref[band, kc]
                v = kvc_ref[band, vc]
            v_and_ones = jnp.concatenate([v, ones], axis=1)
            qs = []
            for j in range(pairs_per_kv):
                p = g * pairs_per_kv + j
                q = q_ref[rows, p * PAIR:(p + 1) * PAIR]
                qs += [jnp.where(first_head, q, 0), jnp.where(first_head, 0, q)]
            s = _dot_nt(jnp.concatenate(qs, axis=0), k)
            es, sink_terms = [], []
            for hh in range(2 * pairs_per_kv):
                head = 2 * g * pairs_per_kv + hh
                sh = s[hh * BLOCK:(hh + 1) * BLOCK] + bias_ref[table, head]
                sink = sink_ref[head] * LOG2_E
                m = jnp.maximum(jnp.max(sh, axis=-1, keepdims=True), sink)
                es.append(jnp.exp2(sh - m).astype(_BF16))
                sink_terms.append(jnp.exp2(sink - m))
            pv = _dot(jnp.concatenate(es, axis=0), v_and_ones)
            for j in range(pairs_per_kv):
                p = g * pairs_per_kv + j
                halves = []
                for h in range(2):
                    r = pv[(2 * j + h) * BLOCK:(2 * j + h + 1) * BLOCK]
                    halves.append(r[:, :PAIR] / (r[:, PAIR:] + sink_terms[2 * j + h]))
                o_ref[rows, p * PAIR:(p + 1) * PAIR] = (
                    jnp.where(first_head, halves[0], halves[1]).astype(o_ref.dtype))


def _swa_attention(q_src, kv_src, kv_col, d, kvw, bias_tab, sinks):
    batch, seq, _ = q_src.shape
    blocks = math.gcd(seq // BLOCK, SWA_BLOCKS_PER_STEP)
    tq = blocks * BLOCK
    return pl.pallas_call(
        _swa_kernel,
        grid=(batch, seq // tq),
        in_specs=[
            pl.BlockSpec(memory_space=pltpu.SMEM),
            pl.BlockSpec((None, tq, d), lambda b, n: (b, n, 0)),
            pl.BlockSpec((None, BLOCK, kvw), lambda b, n: (b, jnp.maximum(n * blocks - 1, 0), kv_col)),
            pl.BlockSpec((None, tq, kvw), lambda b, n: (b, n, kv_col)),
            _resident(bias_tab.shape),
        ],
        out_specs=pl.BlockSpec((None, tq, d), lambda b, n: (b, n, 0)),
        out_shape=jax.ShapeDtypeStruct((batch, seq, d), _BF16),
        compiler_params=pltpu.CompilerParams(
            dimension_semantics=("arbitrary", "arbitrary"), vmem_limit_bytes=VMEM_LIMIT_BYTES),
        name="swa_attention",
    )(sinks, q_src, kv_src, kv_src, bias_tab)


def _duplicate_heads(w, n_heads):
    lead = w.shape[:-1]
    w = w.reshape(lead + (n_heads, 1, HEAD_DIM))
    w = jnp.broadcast_to(w, lead + (n_heads, 2, HEAD_DIM))
    return w.reshape(lead + (n_heads * PAIR,))


def kernel(x, a_norm, a_wqkv, a_wo, kv_norm, w_kv, b_kv, b_norm, b_wq, b_bq, b_sinks,
           b_wo, b_bo, rel_bias, mlp_norm, mlp_up, mlp_down, final_norm):
    batch, seq, d = x.shape
    m = batch * seq
    n_a = a_wqkv.shape[0]
    n_b = b_wq.shape[0]
    depth = n_a + n_b
    n_heads = d // HEAD_DIM
    n_kv = w_kv.shape[1] // (2 * HEAD_DIM)
    assert seq % BLOCK == 0 and m % min(ROW_TILE, m) == 0 and n_heads % (2 * n_kv) == 0

    bf = lambda w: w.astype(_BF16)
    kvw = 2 * n_kv * PAIR
    q_scale = SCALE * LOG2_E

    def attention_projection(layer):
        if layer < n_a:
            col_scale = jnp.where(jnp.arange(a_wqkv.shape[2]) < d, q_scale, 1.0)
            w = a_norm[layer][:, None] * a_wqkv[layer] * col_scale
            return bf(w), jnp.zeros((w.shape[1],), _F32)
        j = layer - n_a
        w = b_norm[j][:, None] * b_wq[j] * q_scale
        b = b_bq[j] * q_scale
        if j == 0:
            wk, wv = jnp.split(w_kv, 2, axis=-1)
            bk, bv = jnp.split(b_kv, 2, axis=-1)
            w_dup = jnp.concatenate([_duplicate_heads(wk, n_kv), _duplicate_heads(wv, n_kv)], -1)
            b_dup = jnp.concatenate([_duplicate_heads(bk, n_kv), _duplicate_heads(bv, n_kv)], -1)
            w = jnp.concatenate([w, kv_norm[:, None] * w_dup], axis=-1)
            b = jnp.concatenate([b, b_dup], axis=-1)
        return bf(w), b

    bias_tab = _bias_band(rel_bias, _band_buckets())
    h = x.reshape(m, d)
    y = _norm_proj(h, *attention_projection(0)).reshape(batch, seq, -1)
    kv_src = None
    for layer in range(depth):
        if layer < n_a:
            attn = _sb_attention(y, batch, seq, n_heads,
                                 SB_WINDOW_FIRST_LAYER if layer == 0 else SB_WINDOW_LATER_LAYERS)
            wo, bo = a_wo[layer], jnp.zeros((d,), _F32)
        else:
            j = layer - n_a
            if j == 0:
                kv_src = y
            attn = _swa_attention(y, kv_src, d // kvw, d, kvw, bias_tab, b_sinks[j])
            wo, bo = b_wo[j], b_bo[j]
        tail = functools.partial(_layer_tail, h, attn.reshape(m, d), bf(wo), bo, mlp_norm[layer],
                                 bf(mlp_up[layer]), bf(mlp_down[layer]))
        if layer == depth - 1:
            return tail(final_gain=final_norm).reshape(batch, seq, d)
        h, y = tail(next_proj=attention_projection(layer + 1))
        y = y.reshape(batch, seq, -1)
```

```python
import functools
import math

import jax
import jax.numpy as jnp
from jax import lax
from jax.experimental import pallas as pl
from jax.experimental.pallas import tpu as pltpu

HEAD_DIM = 64
PAIR = 2 * HEAD_DIM
BLOCK = 128
WINDOW = 128
SWA_GROUP = 8
NUM_BUCKETS = 32
MAX_EXACT = NUM_BUCKETS // 2
MAX_DISTANCE = 128
EPS = 1e-5
NEG_INF = -1e30
SCALE = 1.0 / math.sqrt(HEAD_DIM)

LOG2_E = math.log2(math.e)
SB_UNDERFLOW_LOG2 = -151.0
SB_TILE = 64
SB_WINDOW_FIRST_LAYER = 256
SB_WINDOW_LATER_LAYERS = 384
SB_SCORE_LOOKAHEAD = 6
SB_SUFFIX_LOOKAHEAD = 3
SB_FLAG_GROUP = 8

SWA_BLOCKS_PER_STEP = 16

ROW_TILE = 512
FF_CHUNK = 1024
PROJ_CHUNK = 512
SUB_ROWS = 256
VMEM_LIMIT_BYTES = 56 * 1024 * 1024

_F32 = jnp.float32
_BF16 = jnp.bfloat16


def _dot(a, b):
    return jnp.dot(a, b, preferred_element_type=_F32)


def _dot_nt(a, b):
    return lax.dot_general(a, b, (((1,), (1,)), ((), ())), preferred_element_type=_F32)


def _rms_unit(x):
    ms = jnp.mean(x * x, axis=-1, keepdims=True)
    return x * lax.rsqrt(ms + EPS)


def _resident(shape):
    return pl.BlockSpec(shape, lambda *_: (0,) * len(shape), pipeline_mode=pl.Buffered(1))


def _row_subtiles(tm):
    step = min(SUB_ROWS, tm)
    return [slice(r, r + step) for r in range(0, tm, step)]


def _project(subs, x_units, w_ref, b_ref, y_ref):
    xb = [x.astype(_BF16) for x in x_units]
    n = y_ref.shape[-1]
    for c in range(0, n, PROJ_CHUNK):
        cols = slice(c, min(c + PROJ_CHUNK, n))
        for s, x in zip(subs, xb):
            y_ref[s, cols] = (_dot(x, w_ref[:, cols]) + b_ref[:, cols]).astype(y_ref.dtype)


def _norm_proj_kernel(x_ref, w_ref, b_ref, y_ref):
    subs = _row_subtiles(x_ref.shape[0])
    _project(subs, [_rms_unit(x_ref[s, :]) for s in subs], w_ref, b_ref, y_ref)


def _norm_proj(x, w, b):
    m, d = x.shape
    n = w.shape[1]
    tm = math.gcd(2 * ROW_TILE, m)
    return pl.pallas_call(
        _norm_proj_kernel,
        grid=(m // tm,),
        in_specs=[pl.BlockSpec((tm, d), lambda i: (i, 0)), _resident((d, n)), _resident((1, n))],
        out_specs=pl.BlockSpec((tm, n), lambda i: (i, 0)),
        out_shape=jax.ShapeDtypeStruct((m, n), _BF16),
        compiler_params=pltpu.CompilerParams(
            dimension_semantics=("arbitrary",), vmem_limit_bytes=VMEM_LIMIT_BYTES),
        name="norm_proj",
    )(x, w, b.reshape(1, n))


def _attn_out_and_mlp(h_ref, a_ref, wo_ref, bo_ref, g_ref, wup_ref, wdn_ref):
    subs = _row_subtiles(h_ref.shape[0])
    h1 =[h_ref[s, :] + _dot(a_ref[s, :], wo_ref[...]) + bo_ref[...] for s in subs]
    xn = [(_rms_unit(x) * g_ref[...]).astype(_BF16) for x in h1]
    acc = h1
    d_ff = wup_ref.shape[1]
    for c in range(0, d_ff, FF_CHUNK):
        u = [jnp.maximum(_dot(x, wup_ref[:, c:c + FF_CHUNK]), 0.0) for x in xn]
        acc = [a + _dot((v * v).astype(_BF16), wdn_ref[c:c + FF_CHUNK, :]) for a, v in zip(acc, u)]
    return subs, acc


def _mid_layer_kernel(h_ref, a_ref, wo_ref, bo_ref, g_ref, wup_ref, wdn_ref, wn_ref, bn_ref,
                      o_ref, y_ref):
    subs, h2 = _attn_out_and_mlp(h_ref, a_ref, wo_ref, bo_ref, g_ref, wup_ref, wdn_ref)
    for s, x in zip(subs, h2):
        o_ref[s, :] = x
    _project(subs, [_rms_unit(x) for x in h2], wn_ref, bn_ref, y_ref)


def _last_layer_kernel(h_ref, a_ref, wo_ref, bo_ref, g_ref, wup_ref, wdn_ref, fg_ref, o_ref):
    subs, h2 = _attn_out_and_mlp(h_ref, a_ref, wo_ref, bo_ref, g_ref, wup_ref, wdn_ref)
    for s, x in zip(subs, h2):
        o_ref[s, :] = _rms_unit(x) * fg_ref[...]


def _layer_tail(h, a, wo, bo, g, wup, wdn, *, next_proj=None, final_gain=None):
    m, d = h.shape
    d_ff = wup.shape[1]
    tm = math.gcd(2 * ROW_TILE, m) if next_proj is None else min(ROW_TILE, m)
    row = lambda i: (i, 0)
    in_specs = [
        pl.BlockSpec((tm, d), row),
        pl.BlockSpec((tm, d), row),
        _resident((d, d)),
        _resident((1, d)),
        _resident((1, d)),
        _resident((d, d_ff)),
        _resident((d_ff, d)),
    ]
    args = [h, a, wo, bo.reshape(1, d), g.reshape(1, d), wup, wdn]
    params = pltpu.CompilerParams(
        dimension_semantics=("arbitrary",), vmem_limit_bytes=VMEM_LIMIT_BYTES)
    if next_proj is None:
        return pl.pallas_call(
            _last_layer_kernel,
            grid=(m // tm,),
            in_specs=in_specs + [_resident((1, d))],
            out_specs=pl.BlockSpec((tm, d), row),
            out_shape=jax.ShapeDtypeStruct((m, d), _F32),
            compiler_params=params,
            name="last_layer",
        )(*args, final_gain.reshape(1, d))
    wn, bn = next_proj
    n = wn.shape[1]
    return pl.pallas_call(
        _mid_layer_kernel,
        grid=(m // tm,),
        in_specs=in_specs + [_resident((d, n)), _resident((1, n))],
        out_specs=[pl.BlockSpec((tm, d), row), pl.BlockSpec((tm, n), row)],
        out_shape=[jax.ShapeDtypeStruct((m, d), _F32), jax.ShapeDtypeStruct((m, n), _BF16)],
        compiler_params=params,
        name="mid_layer",
    )(*args, wn, bn.reshape(1, n))


def _sb_attn_kernel(q_ref, k_ref, v_ref, o_ref, acc_ref, carry_ref, least_ref, unfinished_ref,
                    *, window_keys):
    n_t = q_ref.shape[0] // SB_TILE
    row = lax.broadcasted_iota(jnp.int32, (SB_TILE, BLOCK), 0)
    col = lax.broadcasted_iota(jnp.int32, (SB_TILE, BLOCK), 1)
    first_head = col < HEAD_DIM
    key = lax.broadcasted_iota(jnp.int32, (BLOCK, BLOCK), 0)
    pos = lax.broadcasted_iota(jnp.int32, (BLOCK, BLOCK), 1)
    suffix_and_ones = jnp.concatenate(
        [(key >= pos).astype(_BF16), jnp.ones((BLOCK, BLOCK), _BF16)], axis=1)
    suffix_and_ones = jnp.concatenate([suffix_and_ones, suffix_and_ones], axis=0)

    def tile_rows(t):
        start = t * SB_TILE
        return pl.ds(start if isinstance(t, int) else pl.multiple_of(start, SB_TILE), SB_TILE)

    def load_q(t):
        q = q_ref[tile_rows(t), :]
        return jnp.concatenate([jnp.where(first_head, q, 0), jnp.where(first_head, 0, q)], axis=0)

    def window(t):
        end = (t + 1) * SB_TILE
        n_b = min(pl.cdiv(end, BLOCK), window_keys // BLOCK)
        start = max(end - n_b * BLOCK, 0)
        offsets = []
        for b in range(n_b):
            off = t * SB_TILE - start - b * BLOCK
            assert off > -SB_TILE
            offsets.append(None if off >= BLOCK else off)
        return start, n_b, offsets

    def scores(q2, k):
        return _dot_nt(q2, k)

    def suffix_sums(z, masks):
        neg_abs = lax.bitcast_convert_type(
            lax.bitcast_convert_type(z, jnp.uint32) | jnp.uint32(0x80000000), _F32)
        sp = jnp.maximum(z, 0.0) + jnp.log2(1.0 + jnp.exp2(neg_abs))
        split = []
        for h in range(2):
            for b, mask in enumerate(masks):
                s = sp[h * SB_TILE:(h + 1) * SB_TILE, b * BLOCK:(b + 1) * BLOCK]
                if mask is not None:
                    s = jnp.where(mask, s, 0.0)
                hi = s.astype(_BF16)
                lo = (s - hi.astype(_F32)).astype(_BF16)
                split.append(jnp.concatenate([hi, lo], axis=1))
        return _dot(jnp.concatenate(split, axis=0), suffix_and_ones)

    def weighted_values(z, cs, v, carry_in, masks):
        n_b = len(masks)
        ws, carries = [], []
        for h in range(2):
            carry = None if carry_in is None else carry_in[h]
            wb = [None] * n_b
            for b in reversed(range(n_b)):
                r = cs[(h * n_b + b) * SB_TILE:(h * n_b + b + 1) * SB_TILE]
                arg = z[h * SB_TILE:(h + 1) * SB_TILE, b * BLOCK:(b + 1) * BLOCK] - r[:, :BLOCK]
                if carry is not None:
                    arg = arg - carry
                w = jnp.exp2(arg)
                if masks[b] is not None:
                    w = jnp.where(masks[b], w, 0.0)
                wb[b] = w.astype(_BF16)
                carry = r[:, BLOCK:] if carry is None else carry + r[:, BLOCK:]
            ws.append(jnp.concatenate(wb, axis=1))
            carries.append(carry)
        out = _dot(jnp.concatenate(ws, axis=0), v)
        return jnp.where(first_head, out[:SB_TILE], out[SB_TILE:]), carries

    def window_scores(t):
        start, n_b, _ = window(t)
        return scores(load_q(t), k_ref[pl.ds(start, n_b * BLOCK), :])

    zs, css, wins, masks = {}, {}, {}, {}

    def issue_scores(t):
        if t < n_t:
            wins[t] = window(t)
            masks[t] = [None if off is None else col < row + off for off in wins[t][2]]
            zs[t] = window_scores(t)

    def issue_suffix_sums(t):
        if t < n_t:
            css[t] = suffix_sums(zs[t], masks[t])

    for t in range(SB_SCORE_LOOKAHEAD):
        issue_scores(t)
    for t in range(SB_SUFFIX_LOOKAHEAD):
        issue_suffix_sums(t)
    group_unfinished = {}
    for t in range(n_t):
        issue_scores(t + SB_SCORE_LOOKAHEAD)
        issue_suffix_sums(t + SB_SUFFIX_LOOKAHEAD)
        start, n_b, _ = wins.pop(t)
        out, carry = weighted_values(zs.pop(t), css.pop(t), v_ref[pl.ds(start, n_b * BLOCK), :],
                                     None, masks.pop(t))
        o_ref[tile_rows(t), :] = out.astype(o_ref.dtype)
        if start > 0:
            acc_ref[t] = out
            carry_ref[t, 0] = carry[0]
            carry_ref[t, 1] = carry[1]
            least = jnp.min(jnp.minimum(carry[0], carry[1]))
            least_ref[t] = least
            unfinished = least < -SB_UNDERFLOW_LOG2
            g = t // SB_FLAG_GROUP
            group_unfinished[g] = (jnp.logical_or(group_unfinished[g], unfinished)
                                   if g in group_unfinished else unfinished)
    for g, unfinished in group_unfinished.items():
        unfinished_ref[g] = unfinished.astype(jnp.int32)

    first_open = next((t for t in range(n_t) if window(t)[0] > 0), n_t)

    def finish_tile(t, _):
        def more(state):
            remaining, least = state
            return jnp.logical_and(remaining > 0, least < -SB_UNDERFLOW_LOG2)

        def older_block(state):
            remaining, _ = state
            first = jnp.maximum(remaining - BLOCK, 0)
            rows = pl.ds(pl.multiple_of(first, SB_TILE), BLOCK)
            masks = [col < remaining - first]
            z = scores(load_q(t), k_ref[rows, :])
            out, carry = weighted_values(z, suffix_sums(z, masks), v_ref[rows, :],
                                         [carry_ref[t, 0], carry_ref[t, 1]], masks)
            acc_ref[t] += out
            carry_ref[t, 0] = carry[0]
            carry_ref[t, 1] = carry[1]
            return first, jnp.min(jnp.minimum(carry[0], carry[1]))

        @pl.when(least_ref[t] < -SB_UNDERFLOW_LOG2)
        def _():
            lax.while_loop(more, older_block, ((t + 1) * SB_TILE - window_keys, least_ref[t]))
            o_ref[tile_rows(t), :] = acc_ref[t].astype(o_ref.dtype)
        return 0

    def finish_group(g, _):
        @pl.when(unfinished_ref[g] != 0)
        def _():
            lax.fori_loop(jnp.maximum(g * SB_FLAG_GROUP, first_open),
                          jnp.minimum((g + 1) * SB_FLAG_GROUP, n_t), finish_tile, 0)
        return 0

    if n_t > first_open:
        lax.fori_loop(first_open // SB_FLAG_GROUP, pl.cdiv(n_t, SB_FLAG_GROUP), finish_group, 0)


def _sb_attention(qkv, batch, seq, n_heads, window_keys):
    n_pairs = n_heads // 2
    n_t = seq // SB_TILE
    blk = (None, seq, PAIR)
    return pl.pallas_call(
        functools.partial(_sb_attn_kernel, window_keys=window_keys),
        grid=(batch, n_pairs),
        in_specs=[
            pl.BlockSpec(blk, lambda b, p: (b, 0, p)),
            pl.BlockSpec(blk, lambda b, p: (b, 0, n_pairs + p)),
            pl.BlockSpec(blk, lambda b, p: (b, 0, 2 * n_pairs + p)),
        ],
        out_specs=pl.BlockSpec(blk, lambda b, p: (b, 0, p)),
        out_shape=jax.ShapeDtypeStruct((batch, seq, n_heads * HEAD_DIM), _BF16),
        scratch_shapes=[pltpu.VMEM((n_t, SB_TILE, PAIR), _F32),
                        pltpu.VMEM((n_t, 2, SB_TILE, BLOCK), _F32),
                        pltpu.SMEM((n_t,), _F32),
                        pltpu.SMEM((pl.cdiv(n_t, SB_FLAG_GROUP),), jnp.int32)],
        compiler_params=pltpu.CompilerParams(
            dimension_semantics=("arbitrary", "arbitrary"), vmem_limit_bytes=VMEM_LIMIT_BYTES),
        name="sb_attention",
    )(qkv, qkv, qkv)


def _bias_band_kernel(rel_ref, sink_ref, bucket_ref, o_ref):
    layer = pl.program_id(0)
    head = pl.program_id(1)
    bucket = bucket_ref[...]
    qi = lax.broadcasted_iota(jnp.int32, bucket.shape, 0)
    kj = lax.broadcasted_iota(jnp.int32, bucket.shape, 1)
    dist = qi + BLOCK - kj
    valid = (dist >= 0) & (dist < WINDOW)
    bias = jnp.zeros(bucket.shape, _F32)
    for b in range(NUM_BUCKETS):
        bias = jnp.where(bucket == b, rel_ref[b, head] * LOG2_E, bias)
    sink = sink_ref[layer, head] * LOG2_E
    o_ref[0] = jnp.where(kj == 0, sink, jnp.where(valid & (kj >= BLOCK), bias, NEG_INF))
    o_ref[1] = jnp.where(kj == 0, sink, jnp.where(valid, bias, NEG_INF))


def _bias_band(rel_bias, sinks, bucket):
    n_layers, n_heads = sinks.shape
    return pl.pallas_call(
        _bias_band_kernel,
        grid=(n_layers, n_heads),
        in_specs=[
            pl.BlockSpec(memory_space=pltpu.SMEM),
            pl.BlockSpec(memory_space=pltpu.SMEM),
            pl.BlockSpec(bucket.shape, lambda j, h: (0, 0)),
        ],
        out_specs=pl.BlockSpec((None, 2, None) + bucket.shape, lambda j, h: (j, 0, h, 0, 0)),
        out_shape=jax.ShapeDtypeStruct((n_layers, 2, n_heads) + bucket.shape, _F32),
        name="bias_band",
    )(rel_bias, sinks, bucket)


def _band_buckets():
    qi = jnp.arange(BLOCK)[:, None]
    kj = jnp.arange(2 * BLOCK)[None, :]
    n = jnp.maximum(qi + BLOCK - kj, 0)
    nf = jnp.maximum(n, 1).astype(_F32)
    large = MAX_EXACT + (jnp.log(nf / MAX_EXACT) / math.log(MAX_DISTANCE / MAX_EXACT)
                         * (NUM_BUCKETS - MAX_EXACT)).astype(jnp.int32)
    large = jnp.minimum(large, NUM_BUCKETS - 1)
    return jnp.where(n < MAX_EXACT, n, large).astype(jnp.int32)


def _swa_kernel(q_ref, kvp_ref, kvc_ref, bias_ref, o_ref):
    n_pairs = q_ref.shape[-1] // PAIR
    n_kv = kvc_ref.shape[-1] // (2 * PAIR)
    pairs_per_kv = n_pairs // n_kv
    blocks = q_ref.shape[0] // BLOCK
    first_head = lax.broadcasted_iota(jnp.int32, (BLOCK, PAIR), 1) < HEAD_DIM
    ones = jnp.ones((2 * BLOCK, PAIR), _BF16)
    sink_key = lax.broadcasted_iota(jnp.int32, (2 * BLOCK, PAIR), 0) == 0
    first_table = jnp.where(pl.program_id(1) == 0, 0, 1)

    for t in range(blocks):
        rows = slice(t * BLOCK, (t + 1) * BLOCK)
        table = first_table if t == 0 else 1
        for g in range(n_kv):
            kc = slice(g * PAIR, (g + 1) * PAIR)
            vc = slice((n_kv + g) * PAIR, (n_kv + g + 1) * PAIR)
            if t == 0:
                k = jnp.concatenate([kvp_ref[:, kc], kvc_ref[rows, kc]], axis=0)
                v = jnp.concatenate([kvp_ref[:, vc], kvc_ref[rows, vc]], axis=0)
            else:
                band = slice((t - 1) * BLOCK, (t + 1) * BLOCK)
                k = kvc_ref[band, kc]
                v = kvc_ref[band, vc]
            k = jnp.where(sink_key, 0, k)
            v = jnp.where(sink_key, 0, v)
            v_and_ones = jnp.concatenate([v, ones], axis=1)
            qs = []
            for j in range(pairs_per_kv):
                p = g * pairs_per_kv + j
                q = q_ref[rows, p * PAIR:(p + 1) * PAIR]
                qs += [jnp.where(first_head, q, 0), jnp.where(first_head, 0, q)]
            s = _dot_nt(jnp.concatenate(qs, axis=0), k)
            es = []
            for hh in range(2 * pairs_per_kv):
                head = 2 * g * pairs_per_kv + hh
                sh = s[hh * BLOCK:(hh + 1) * BLOCK] + bias_ref[table, head]
                es.append(jnp.exp2(sh - jnp.max(sh, axis=-1, keepdims=True)).astype(_BF16))
            pv = _dot(jnp.concatenate(es, axis=0), v_and_ones)
            for j in range(pairs_per_kv):
                p = g * pairs_per_kv + j
                halves = []
                for h in range(2):
                    r = pv[(2 * j + h) * BLOCK:(2 * j + h + 1) * BLOCK]
                    halves.append(r[:, :PAIR] / r[:, PAIR:])
                o_ref[rows, p * PAIR:(p + 1) * PAIR] = (
                    jnp.where(first_head, halves[0], halves[1]).astype(o_ref.dtype))


def _swa_attention(q_src, kv_src, kv_col, d, kvw, bias_tabs, layer):
    batch, seq, _ = q_src.shape
    blocks = math.gcd(seq // BLOCK, SWA_BLOCKS_PER_STEP)
    tq = blocks * BLOCK
    return pl.pallas_call(
        _swa_kernel,
        grid=(batch, seq // tq),
        in_specs=[
            pl.BlockSpec((None, tq, d), lambda b, n: (b, n, 0)),
            pl.BlockSpec((None, BLOCK, kvw), lambda b, n: (b, jnp.maximum(n * blocks - 1, 0), kv_col)),
            pl.BlockSpec((None, tq, kvw), lambda b, n: (b, n, kv_col)),
            pl.BlockSpec((None,) + bias_tabs.shape[1:], lambda b, n: (layer, 0, 0, 0, 0),
                         pipeline_mode=pl.Buffered(1)),
        ],
        out_specs=pl.BlockSpec((None, tq, d), lambda b, n: (b, n, 0)),
        out_shape=jax.ShapeDtypeStruct((batch, seq, d), _BF16),
        compiler_params=pltpu.CompilerParams(
            dimension_semantics=("arbitrary", "arbitrary"), vmem_limit_bytes=VMEM_LIMIT_BYTES),
        name="swa_attention",
    )(q_src, kv_src, kv_src, bias_tabs)


def _duplicate_heads(w, n_heads):
    lead = w.shape[:-1]
    w = w.reshape(lead + (n_heads, 1, HEAD_DIM))
    w = jnp.broadcast_to(w, lead + (n_heads, 2, HEAD_DIM))
    return w.reshape(lead + (n_heads * PAIR,))


def kernel(x, a_norm, a_wqkv, a_wo, kv_norm, w_kv, b_kv, b_norm, b_wq, b_bq, b_sinks,
           b_wo, b_bo, rel_bias, mlp_norm, mlp_up, mlp_down, final_norm):
    batch, seq, d = x.shape
    m = batch * seq
    n_a = a_wqkv.shape[0]
    n_b = b_wq.shape[0]
    depth = n_a + n_b
    n_heads = d // HEAD_DIM
    n_kv = w_kv.shape[1] // (2 * HEAD_DIM)
    assert seq % BLOCK == 0 and m % min(ROW_TILE, m) == 0 and n_heads % (2 * n_kv) == 0

    bf = lambda w: w.astype(_BF16)
    kvw = 2 * n_kv * PAIR
    q_scale = SCALE * LOG2_E

    a_w = bf(a_norm[:, :, None] * a_wqkv * jnp.where(jnp.arange(a_wqkv.shape[2]) < d, q_scale, 1.0))

    def attention_projection(layer):
        if layer < n_a:
            return a_w[layer], jnp.zeros((a_w.shape[2],), _F32)
        j = layer - n_a
        w = b_norm[j][:, None] * b_wq[j] * q_scale
        b = b_bq[j] * q_scale
        if j == 0:
            wk, wv = jnp.split(w_kv, 2, axis=-1)
            bk, bv = jnp.split(b_kv, 2, axis=-1)
            w_dup = jnp.concatenate([_duplicate_heads(wk, n_kv), _duplicate_heads(wv, n_kv)], -1)
            b_dup = jnp.concatenate([_duplicate_heads(bk, n_kv), _duplicate_heads(bv, n_kv)], -1)
            w = jnp.concatenate([w, kv_norm[:, None] * w_dup], axis=-1)
            b = jnp.concatenate([b, b_dup], axis=-1)
        return bf(w), b

    bias_tabs = _bias_band(rel_bias, b_sinks, _band_buckets())
    h = x.reshape(m, d)
    y = _norm_proj(h, *attention_projection(0)).reshape(batch, seq, -1)
    kv_src = None
    for layer in range(depth):
        if layer < n_a:
            attn = _sb_attention(y, batch, seq, n_heads,
                                 SB_WINDOW_FIRST_LAYER if layer == 0 else SB_WINDOW_LATER_LAYERS)
            wo, bo = a_wo[layer], jnp.zeros((d,), _F32)
        else:
            j = layer - n_a
            if j == 0:
                kv_src = y
            attn = _swa_attention(y, kv_src, d // kvw, d, kvw, bias_tabs, j)
            wo, bo = b_wo[j], b_bo[j]
        tail = functools.partial(_layer_tail, h, attn.reshape(m, d), bf(wo), bo, mlp_norm[layer],
                                 bf(mlp_up[layer]), bf(mlp_down[layer]))
        if layer == depth - 1:
            return tail(final_gain=final_norm).reshape(batch, seq, d)
        h, y = tail(next_proj=attention_projection(layer + 1))
        y = y.reshape(batch, seq, -1)
```

```python
import functools
import math

import jax
import jax.numpy as jnp
from jax import lax
from jax.experimental import pallas as pl
from jax.experimental.pallas import tpu as pltpu

HEAD_DIM = 64
PAIR = 2 * HEAD_DIM
BLOCK = 128
WINDOW = 128
SWA_GROUP = 8
NUM_BUCKETS = 32
MAX_EXACT = NUM_BUCKETS // 2
MAX_DISTANCE = 128
EPS = 1e-5
NEG_INF = -1e30
SCALE = 1.0 / math.sqrt(HEAD_DIM)

LOG2_E = math.log2(math.e)
SB_UNDERFLOW_LOG2 = -151.0
SB_TILE = 64
SB_WINDOW_FIRST_LAYER = 256
SB_WINDOW_LATER_LAYERS = 384
SB_SCORE_LOOKAHEAD = 6
SB_SUFFIX_LOOKAHEAD = 3
SB_FLAG_GROUP = 8

SWA_BLOCKS_PER_STEP = 16

ROW_TILE = 512
FF_CHUNK = 1024
PROJ_CHUNK = 512
SUB_ROWS = 256
VMEM_LIMIT_BYTES = 56 * 1024 * 1024

_F32 = jnp.float32
_BF16 = jnp.bfloat16


def _dot(a, b):
    return jnp.dot(a, b, preferred_element_type=_F32)


def _dot_nt(a, b):
    return lax.dot_general(a, b, (((1,), (1,)), ((), ())), preferred_element_type=_F32)


def _rms_unit(x):
    ms = jnp.mean(x * x, axis=-1, keepdims=True)
    return x * lax.rsqrt(ms + EPS)


def _resident(shape):
    return pl.BlockSpec(shape, lambda *_: (0,) * len(shape), pipeline_mode=pl.Buffered(1))


def _resident_layer(stack, index):
    return pl.BlockSpec((None,) + stack.shape[1:], lambda *_: (index,) + (0,) * (stack.ndim - 1),
                        pipeline_mode=pl.Buffered(1))


def _row_subtiles(tm):
    step = min(SUB_ROWS, tm)
    return [slice(r, r + step) for r in range(0, tm, step)]


def _project(subs, x_units, w_ref, b_ref, y_ref):
    xb = [x.astype(_BF16) for x in x_units]
    n = y_ref.shape[-1]
    for c in range(0, n, PROJ_CHUNK):
        cols = slice(c, min(c + PROJ_CHUNK, n))
        for s, x in zip(subs, xb):
            y_ref[s, cols] = (_dot(x, w_ref[:, cols]) + b_ref[:, cols]).astype(y_ref.dtype)


def _norm_proj_kernel(x_ref, w_ref, b_ref, y_ref):
    subs = _row_subtiles(x_ref.shape[0])
    _project(subs, [_rms_unit(x_ref[s, :]) for s in subs], w_ref, b_ref, y_ref)


def _norm_proj(x, w, b):
    m, d = x.shape
    n = w[0].shape[-1]
    tm = math.gcd(2 * ROW_TILE, m)
    return pl.pallas_call(
        _norm_proj_kernel,
        grid=(m // tm,),
        in_specs=[pl.BlockSpec((tm, d), lambda i: (i, 0)), _resident_layer(*w), _resident((1, n))],
        out_specs=pl.BlockSpec((tm, n), lambda i: (i, 0)),
        out_shape=jax.ShapeDtypeStruct((m, n), _BF16),
        compiler_params=pltpu.CompilerParams(
            dimension_semantics=("arbitrary",), vmem_limit_bytes=VMEM_LIMIT_BYTES),
        name="norm_proj",
    )(x, w[0], b.reshape(1, n))


def _attn_out_and_mlp(h_ref, a_ref, wo_ref, bo_ref, g_ref, wup_ref, wdn_ref):
    subs = _row_subtiles(h_ref.shape[0])
    h1 =[h_ref[s, :] + _dot(a_ref[s, :], wo_ref[...]) + bo_ref[...] for s in subs]
    xn = [(_rms_unit(x) * g_ref[...]).astype(_BF16) for x in h1]
    acc = h1
    d_ff = wup_ref.shape[1]
    for c in range(0, d_ff, FF_CHUNK):
        u = [jnp.maximum(_dot(x, wup_ref[:, c:c + FF_CHUNK]), 0.0) for x in xn]
        acc = [a + _dot((v * v).astype(_BF16), wdn_ref[c:c + FF_CHUNK, :]) for a, v in zip(acc, u)]
    return subs, acc


def _mid_layer_kernel(h_ref, a_ref, wo_ref, bo_ref, g_ref, wup_ref, wdn_ref, wn_ref, bn_ref,
                      o_ref, y_ref):
    subs, h2 = _attn_out_and_mlp(h_ref, a_ref, wo_ref, bo_ref, g_ref, wup_ref, wdn_ref)
    for s, x in zip(subs, h2):
        o_ref[s, :] = x
    _project(subs, [_rms_unit(x) for x in h2], wn_ref, bn_ref, y_ref)


def _last_layer_kernel(h_ref, a_ref, wo_ref, bo_ref, g_ref, wup_ref, wdn_ref, fg_ref, o_ref):
    subs, h2 = _attn_out_and_mlp(h_ref, a_ref, wo_ref, bo_ref, g_ref, wup_ref, wdn_ref)
    for s, x in zip(subs, h2):
        o_ref[s, :] = _rms_unit(x) * fg_ref[...]


def _layer_tail(h, a, wo, bo, g, wup, wdn, *, next_proj=None, final_gain=None):
    m, d = h.shape
    tm = math.gcd(2 * ROW_TILE, m) if next_proj is None else min(ROW_TILE, m)
    row = lambda i: (i, 0)
    in_specs = [
        pl.BlockSpec((tm, d), row),
        pl.BlockSpec((tm, d), row),
        _resident_layer(*wo),
        _resident((1, d)),
        _resident((1, d)),
        _resident_layer(*wup),
        _resident_layer(*wdn),
    ]
    args = [h, a, wo[0], bo.reshape(1, d), g.reshape(1, d), wup[0], wdn[0]]
    params = pltpu.CompilerParams(
        dimension_semantics=("arbitrary",), vmem_limit_bytes=VMEM_LIMIT_BYTES)
    if next_proj is None:
        return pl.pallas_call(
            _last_layer_kernel,
            grid=(m // tm,),
            in_specs=in_specs + [_resident((1, d))],
            out_specs=pl.BlockSpec((tm, d), row),
            out_shape=jax.ShapeDtypeStruct((m, d), _F32),
            compiler_params=params,
            name="last_layer",
        )(*args, final_gain.reshape(1, d))
    wn, bn = next_proj
    n = wn[0].shape[-1]
    return pl.pallas_call(
        _mid_layer_kernel,
        grid=(m // tm,),
        in_specs=in_specs + [_resident_layer(*wn), _resident((1, n))],
        out_specs=[pl.BlockSpec((tm, d), row), pl.BlockSpec((tm, n), row)],
        out_shape=[jax.ShapeDtypeStruct((m, d), _F32), jax.ShapeDtypeStruct((m, n), _BF16)],
        compiler_params=params,
        name="mid_layer",
    )(*args, wn[0], bn.reshape(1, n))


def _sb_attn_kernel(q_ref, k_ref, v_ref, o_ref, acc_ref, carry_ref, least_ref, unfinished_ref,
                    *, window_keys):
    n_t = q_ref.shape[0] // SB_TILE
    row = lax.broadcasted_iota(jnp.int32, (SB_TILE, BLOCK), 0)
    col = lax.broadcasted_iota(jnp.int32, (SB_TILE, BLOCK), 1)
    first_head = col < HEAD_DIM
    key = lax.broadcasted_iota(jnp.int32, (BLOCK, BLOCK), 0)
    pos = lax.broadcasted_iota(jnp.int32, (BLOCK, BLOCK), 1)
    suffix_and_ones = jnp.concatenate(
        [(key >= pos).astype(_BF16), jnp.ones((BLOCK, BLOCK), _BF16)], axis=1)
    suffix_and_ones = jnp.concatenate([suffix_and_ones, suffix_and_ones], axis=0)

    def tile_rows(t):
        start = t * SB_TILE
        return pl.ds(start if isinstance(t, int) else pl.multiple_of(start, SB_TILE), SB_TILE)

    def load_q(t):
        q = q_ref[tile_rows(t), :]
        return jnp.concatenate([jnp.where(first_head, q, 0), jnp.where(first_head, 0, q)], axis=0)

    def window(t):
        end = (t + 1) * SB_TILE
        n_b = min(pl.cdiv(end, BLOCK), window_keys // BLOCK)
        start = max(end - n_b * BLOCK, 0)
        offsets = []
        for b in range(n_b):
            off = t * SB_TILE - start - b * BLOCK
            assert off > -SB_TILE
            offsets.append(None if off >= BLOCK else off)
        return start, n_b, offsets

    def scores(q2, k):
        return _dot_nt(q2, k)

    def suffix_sums(z, masks):
        neg_abs = lax.bitcast_convert_type(
            lax.bitcast_convert_type(z, jnp.uint32) | jnp.uint32(0x80000000), _F32)
        sp = jnp.maximum(z, 0.0) + jnp.log2(1.0 + jnp.exp2(neg_abs))
        split = []
        for h in range(2):
            for b, mask in enumerate(masks):
                s = sp[h * SB_TILE:(h + 1) * SB_TILE, b * BLOCK:(b + 1) * BLOCK]
                if mask is not None:
                    s = jnp.where(mask, s, 0.0)
                hi = s.astype(_BF16)
                lo = (s - hi.astype(_F32)).astype(_BF16)
                split.append(jnp.concatenate([hi, lo], axis=1))
        return _dot(jnp.concatenate(split, axis=0), suffix_and_ones)

    def weighted_values(z, cs, v, carry_in, masks):
        n_b = len(masks)
        ws, carries = [], []
        for h in range(2):
            carry = None if carry_in is None else carry_in[h]
            wb = [None] * n_b
            for b in reversed(range(n_b)):
                r = cs[(h * n_b + b) * SB_TILE:(h * n_b + b + 1) * SB_TILE]
                arg = z[h * SB_TILE:(h + 1) * SB_TILE, b * BLOCK:(b + 1) * BLOCK] - r[:, :BLOCK]
                if carry is not None:
                    arg = arg - carry
                w = jnp.exp2(arg)
                if masks[b] is not None:
                    w = jnp.where(masks[b], w, 0.0)
                wb[b] = w.astype(_BF16)
                carry = r[:, BLOCK:] if carry is None else carry + r[:, BLOCK:]
            ws.append(jnp.concatenate(wb, axis=1))
            carries.append(carry)
        out = _dot(jnp.concatenate(ws, axis=0), v)
        return jnp.where(first_head, out[:SB_TILE], out[SB_TILE:]), carries

    def window_scores(t):
        start, n_b, _ = window(t)
        return scores(load_q(t), k_ref[pl.ds(start, n_b * BLOCK), :])

    zs, css, wins, masks = {}, {}, {}, {}

    def issue_scores(t):
        if t < n_t:
            wins[t] = window(t)
            masks[t] = [None if off is None else col < row + off for off in wins[t][2]]
            zs[t] = window_scores(t)

    def issue_suffix_sums(t):
        if t < n_t:
            css[t] = suffix_sums(zs[t], masks[t])

    for t in range(SB_SCORE_LOOKAHEAD):
        issue_scores(t)
    for t in range(SB_SUFFIX_LOOKAHEAD):
        issue_suffix_sums(t)
    group_unfinished = {}
    for t in range(n_t):
        issue_scores(t + SB_SCORE_LOOKAHEAD)
        issue_suffix_sums(t + SB_SUFFIX_LOOKAHEAD)
        start, n_b, _ = wins.pop(t)
        out, carry = weighted_values(zs.pop(t), css.pop(t), v_ref[pl.ds(start, n_b * BLOCK), :],
                                     None, masks.pop(t))
        o_ref[tile_rows(t), :] = out.astype(o_ref.dtype)
        if start > 0:
            acc_ref[t] = out
            carry_ref[t, 0] = carry[0]
            carry_ref[t, 1] = carry[1]
            least = jnp.min(jnp.minimum(carry[0], carry[1]))
            least_ref[t] = least
            unfinished = least < -SB_UNDERFLOW_LOG2
            g = t // SB_FLAG_GROUP
            group_unfinished[g] = (jnp.logical_or(group_unfinished[g], unfinished)
                                   if g in group_unfinished else unfinished)
    for g, unfinished in group_unfinished.items():
        unfinished_ref[g] = unfinished.astype(jnp.int32)

    first_open = next((t for t in range(n_t) if window(t)[0] > 0), n_t)

    def finish_tile(t, _):
        def more(state):
            remaining, least = state
            return jnp.logical_and(remaining > 0, least < -SB_UNDERFLOW_LOG2)

        def older_block(state):
            remaining, _ = state
            first = jnp.maximum(remaining - BLOCK, 0)
            rows = pl.ds(pl.multiple_of(first, SB_TILE), BLOCK)
            masks = [col < remaining - first]
            z = scores(load_q(t), k_ref[rows, :])
            out, carry = weighted_values(z, suffix_sums(z, masks), v_ref[rows, :],
                                         [carry_ref[t, 0], carry_ref[t, 1]], masks)
            acc_ref[t] += out
            carry_ref[t, 0] = carry[0]
            carry_ref[t, 1] = carry[1]
            return first, jnp.min(jnp.minimum(carry[0], carry[1]))

        @pl.when(least_ref[t] < -SB_UNDERFLOW_LOG2)
        def _():
            lax.while_loop(more, older_block, ((t + 1) * SB_TILE - window_keys, least_ref[t]))
            o_ref[tile_rows(t), :] = acc_ref[t].astype(o_ref.dtype)
        return 0

    def finish_group(g, _):
        @pl.when(unfinished_ref[g] != 0)
        def _():
            lax.fori_loop(jnp.maximum(g * SB_FLAG_GROUP, first_open),
                          jnp.minimum((g + 1) * SB_FLAG_GROUP, n_t), finish_tile, 0)
        return 0

    if n_t > first_open:
        lax.fori_loop(first_open // SB_FLAG_GROUP, pl.cdiv(n_t, SB_FLAG_GROUP), finish_group, 0)


def _sb_attention(qkv, batch, seq, n_heads, window_keys):
    n_pairs = n_heads // 2
    n_t = seq // SB_TILE
    blk = (None, seq, PAIR)
    return pl.pallas_call(
        functools.partial(_sb_attn_kernel, window_keys=window_keys),
        grid=(batch, n_pairs),
        in_specs=[
            pl.BlockSpec(blk, lambda b, p: (b, 0, p)),
            pl.BlockSpec(blk, lambda b, p: (b, 0, n_pairs + p)),
            pl.BlockSpec(blk, lambda b, p: (b, 0, 2 * n_pairs + p)),
        ],
        out_specs=pl.BlockSpec(blk, lambda b, p: (b, 0, p)),
        out_shape=jax.ShapeDtypeStruct((batch, seq, n_heads * HEAD_DIM), _BF16),
        scratch_shapes=[pltpu.VMEM((n_t, SB_TILE, PAIR), _F32),
                        pltpu.VMEM((n_t, 2, SB_TILE, BLOCK), _F32),
                        pltpu.SMEM((n_t,), _F32),
                        pltpu.SMEM((pl.cdiv(n_t, SB_FLAG_GROUP),), jnp.int32)],
        compiler_params=pltpu.CompilerParams(
            dimension_semantics=("arbitrary", "arbitrary"), vmem_limit_bytes=VMEM_LIMIT_BYTES),
        name="sb_attention",
    )(qkv, qkv, qkv)


def _bias_band_kernel(rel_ref, sink_ref, bucket_ref, o_ref):
    layer = pl.program_id(0)
    head = pl.program_id(1)
    bucket = bucket_ref[...]
    qi = lax.broadcasted_iota(jnp.int32, bucket.shape, 0)
    kj = lax.broadcasted_iota(jnp.int32, bucket.shape, 1)
    dist = qi + BLOCK - kj
    valid = (dist >= 0) & (dist < WINDOW)
    bias = jnp.zeros(bucket.shape, _F32)
    for b in range(NUM_BUCKETS):
        bias = jnp.where(bucket == b, rel_ref[b, head] * LOG2_E, bias)
    sink = sink_ref[layer, head] * LOG2_E
    o_ref[0] = jnp.where(kj == 0, sink, jnp.where(valid & (kj >= BLOCK), bias, NEG_INF))
    o_ref[1] = jnp.where(kj == 0, sink, jnp.where(valid, bias, NEG_INF))


def _bias_band(rel_bias, sinks, bucket):
    n_layers, n_heads = sinks.shape
    return pl.pallas_call(
        _bias_band_kernel,
        grid=(n_layers, n_heads),
        in_specs=[
            pl.BlockSpec(memory_space=pltpu.SMEM),
            pl.BlockSpec(memory_space=pltpu.SMEM),
            pl.BlockSpec(bucket.shape, lambda j, h: (0, 0)),
        ],
        out_specs=pl.BlockSpec((None, 2, None) + bucket.shape, lambda j, h: (j, 0, h, 0, 0)),
        out_shape=jax.ShapeDtypeStruct((n_layers, 2, n_heads) + bucket.shape, _F32),
        name="bias_band",
    )(rel_bias, sinks, bucket)


def _band_buckets():
    qi = jnp.arange(BLOCK)[:, None]
    kj = jnp.arange(2 * BLOCK)[None, :]
    n = jnp.maximum(qi + BLOCK - kj, 0)
    nf = jnp.maximum(n, 1).astype(_F32)
    large = MAX_EXACT + (jnp.log(nf / MAX_EXACT) / math.log(MAX_DISTANCE / MAX_EXACT)
                         * (NUM_BUCKETS - MAX_EXACT)).astype(jnp.int32)
    large = jnp.minimum(large, NUM_BUCKETS - 1)
    return jnp.where(n < MAX_EXACT, n, large).astype(jnp.int32)


def _swa_kernel(q_ref, kvp_ref, kvc_ref, bias_ref, o_ref):
    n_pairs = q_ref.shape[-1] // PAIR
    n_kv = kvc_ref.shape[-1] // (2 * PAIR)
    pairs_per_kv = n_pairs // n_kv
    blocks = q_ref.shape[0] // BLOCK
    first_head = lax.broadcasted_iota(jnp.int32, (BLOCK, PAIR), 1) < HEAD_DIM
    ones = jnp.ones((2 * BLOCK, PAIR), _BF16)
    sink_key = lax.broadcasted_iota(jnp.int32, (2 * BLOCK, PAIR), 0) == 0
    first_table = jnp.where(pl.program_id(1) == 0, 0, 1)

    for t in range(blocks):
        rows = slice(t * BLOCK, (t + 1) * BLOCK)
        table = first_table if t == 0 else 1
        for g in range(n_kv):
            kc = slice(g * PAIR, (g + 1) * PAIR)
            vc = slice((n_kv + g) * PAIR, (n_kv + g + 1) * PAIR)
            if t == 0:
                k = jnp.concatenate([kvp_ref[:, kc], kvc_ref[rows, kc]], axis=0)
                v = jnp.concatenate([kvp_ref[:, vc], kvc_ref[rows, vc]], axis=0)
            else:
                band = slice((t - 1) * BLOCK, (t + 1) * BLOCK)
                k = kvc_ref[band, kc]
                v = kvc_ref[band, vc]
            k = jnp.where(sink_key, 0, k)
            v = jnp.where(sink_key, 0, v)
            v_and_ones = jnp.concatenate([v, ones], axis=1)
            qs = []
            for j in range(pairs_per_kv):
                p = g * pairs_per_kv + j
                q = q_ref[rows, p * PAIR:(p + 1) * PAIR]
                qs += [jnp.where(first_head, q, 0), jnp.where(first_head, 0, q)]
            s = _dot_nt(jnp.concatenate(qs, axis=0), k)
            es = []
            for hh in range(2 * pairs_per_kv):
                head = 2 * g * pairs_per_kv + hh
                sh = s[hh * BLOCK:(hh + 1) * BLOCK] + bias_ref[table, head]
                es.append(jnp.exp2(sh - jnp.max(sh, axis=-1, keepdims=True)).astype(_BF16))
            pv = _dot(jnp.concatenate(es, axis=0), v_and_ones)
            for j in range(pairs_per_kv):
                p = g * pairs_per_kv + j
                halves = []
                for h in range(2):
                    r = pv[(2 * j + h) * BLOCK:(2 * j + h + 1) * BLOCK]
                    halves.append(r[:, :PAIR] / r[:, PAIR:])
                o_ref[rows, p * PAIR:(p + 1) * PAIR] = (
                    jnp.where(first_head, halves[0], halves[1]).astype(o_ref.dtype))


def _swa_attention(q_src, kv_src, kv_col, d, kvw, bias_tabs, layer):
    batch, seq, _ = q_src.shape
    blocks = math.gcd(seq // BLOCK, SWA_BLOCKS_PER_STEP)
    tq = blocks * BLOCK
    return pl.pallas_call(
        _swa_kernel,
        grid=(batch, seq // tq),
        in_specs=[
            pl.BlockSpec((None, tq, d), lambda b, n: (b, n, 0)),
            pl.BlockSpec((None, BLOCK, kvw), lambda b, n: (b, jnp.maximum(n * blocks - 1, 0), kv_col)),
            pl.BlockSpec((None, tq, kvw), lambda b, n: (b, n, kv_col)),
            pl.BlockSpec((None,) + bias_tabs.shape[1:], lambda b, n: (layer, 0, 0, 0, 0),
                         pipeline_mode=pl.Buffered(1)),
        ],
        out_specs=pl.BlockSpec((None, tq, d), lambda b, n: (b, n, 0)),
        out_shape=jax.ShapeDtypeStruct((batch, seq, d), _BF16),
        compiler_params=pltpu.CompilerParams(
            dimension_semantics=("arbitrary", "arbitrary"), vmem_limit_bytes=VMEM_LIMIT_BYTES),
        name="swa_attention",
    )(q_src, kv_src, kv_src, bias_tabs)


def _duplicate_heads(w, n_heads):
    lead = w.shape[:-1]
    w = w.reshape(lead + (n_heads, 1, HEAD_DIM))
    w = jnp.broadcast_to(w, lead + (n_heads, 2, HEAD_DIM))
    return w.reshape(lead + (n_heads * PAIR,))


def kernel(x, a_norm, a_wqkv, a_wo, kv_norm, w_kv, b_kv, b_norm, b_wq, b_bq, b_sinks,
           b_wo, b_bo, rel_bias, mlp_norm, mlp_up, mlp_down, final_norm):
    batch, seq, d = x.shape
    m = batch * seq
    n_a = a_wqkv.shape[0]
    n_b = b_wq.shape[0]
    depth = n_a + n_b
    n_heads = d // HEAD_DIM
    n_kv = w_kv.shape[1] // (2 * HEAD_DIM)
    assert seq % BLOCK == 0 and m % min(ROW_TILE, m) == 0 and n_heads % (2 * n_kv) == 0

    bf = lambda w: w.astype(_BF16)
    kvw = 2 * n_kv * PAIR
    q_scale = SCALE * LOG2_E

    a_w = bf(a_norm[:, :, None] * a_wqkv * jnp.where(jnp.arange(a_wqkv.shape[2]) < d, q_scale, 1.0))

    def attention_projection(layer):
        if layer < n_a:
            return (a_w, layer), jnp.zeros((a_w.shape[2],), _F32)
        j = layer - n_a
        w = b_norm[j][:, None] * b_wq[j] * q_scale
        b = b_bq[j] * q_scale
        if j == 0:
            wk, wv = jnp.split(w_kv, 2, axis=-1)
            bk, bv = jnp.split(b_kv, 2, axis=-1)
            w_dup = jnp.concatenate([_duplicate_heads(wk, n_kv), _duplicate_heads(wv, n_kv)], -1)
            b_dup = jnp.concatenate([_duplicate_heads(bk, n_kv), _duplicate_heads(bv, n_kv)], -1)
            w = jnp.concatenate([w, kv_norm[:, None] * w_dup], axis=-1)
            b = jnp.concatenate([b, b_dup], axis=-1)
        return (bf(w)[None], 0), b

    a_wo_bf, b_wo_bf, up_bf, down_bf = bf(a_wo), bf(b_wo), bf(mlp_up), bf(mlp_down)
    bias_tabs = _bias_band(rel_bias, b_sinks, _band_buckets())
    h = x.reshape(m, d)
    y = _norm_proj(h, *attention_projection(0)).reshape(batch, seq, -1)
    kv_src = None
    for layer in range(depth):
        if layer < n_a:
            attn = _sb_attention(y, batch, seq, n_heads,
                                 SB_WINDOW_FIRST_LAYER if layer == 0 else SB_WINDOW_LATER_LAYERS)
            wo, bo = (a_wo_bf, layer), jnp.zeros((d,), _F32)
        else:
            j = layer - n_a
            if j == 0:
                kv_src = y
            attn = _swa_attention(y, kv_src, d // kvw, d, kvw, bias_tabs, j)
            wo, bo = (b_wo_bf, j), b_bo[j]
        tail = functools.partial(_layer_tail, h, attn.reshape(m, d), wo, bo, mlp_norm[layer],
                                 (up_bf, layer), (down_bf, layer))
        if layer == depth - 1:
            return tail(final_gain=final_norm).reshape(batch, seq, d)
        h, y = tail(next_proj=attention_projection(layer + 1))
        y = y.reshape(batch, seq, -1)
```

```python
import functools
import math

import jax
import jax.numpy as jnp
from jax import lax
from jax.experimental import pallas as pl
from jax.experimental.pallas import tpu as pltpu

HEAD_DIM = 64
PAIR = 2 * HEAD_DIM
BLOCK = 128
WINDOW = 128
SWA_GROUP = 8
NUM_BUCKETS = 32
MAX_EXACT = NUM_BUCKETS // 2
MAX_DISTANCE = 128
EPS = 1e-5
NEG_INF = -1e30
SCALE = 1.0 / math.sqrt(HEAD_DIM)

LOG2_E = math.log2(math.e)
SB_UNDERFLOW_LOG2 = -151.0
SB_TILE = 64
SB_WINDOW_FIRST_LAYER = 256
SB_WINDOW_LATER_LAYERS = 384
SB_SCORE_LOOKAHEAD = 6
SB_SUFFIX_LOOKAHEAD = 3
SB_FLAG_GROUP = 8

SWA_BLOCKS_PER_STEP = 16

ROW_TILE = 512
FF_CHUNK = 1024
PROJ_CHUNK = 512
SUB_ROWS = 256
VMEM_LIMIT_BYTES = 56 * 1024 * 1024

_F32 = jnp.float32
_BF16 = jnp.bfloat16
BF16_SUBLANES = 16


def _dot(a, b):
    return jnp.dot(a, b, preferred_element_type=_F32)


def _dot_nt(a, b):
    return lax.dot_general(a, b, (((1,), (1,)), ((), ())), preferred_element_type=_F32)


def _rms_unit(x):
    ms = jnp.mean(x * x, axis=-1, keepdims=True)
    return x * lax.rsqrt(ms + EPS)


def _resident(shape):
    return pl.BlockSpec(shape, lambda *_: (0,) * len(shape), pipeline_mode=pl.Buffered(1))


def _resident_layer(stack, index):
    return pl.BlockSpec((None,) + stack.shape[1:], lambda *_: (index,) + (0,) * (stack.ndim - 1),
                        pipeline_mode=pl.Buffered(1))


def _cast_riders(steps, riders):
    in_specs, inputs, out_specs, out_shapes = [], [], [], []
    for stack, index in riders:
        rows, cols = stack.shape[1:]
        share = rows // steps
        in_specs.append(pl.BlockSpec((None, share, cols), lambda i, index=index: (index, i, 0)))
        inputs.append(stack)
        out_specs.append(pl.BlockSpec((share, cols), lambda i: (i, 0)))
        out_shapes.append(jax.ShapeDtypeStruct((rows, cols), _BF16))
    return in_specs, inputs, out_specs, out_shapes


def _can_ride(steps, riders):
    return all(r[0].shape[1] % (steps * BF16_SUBLANES) == 0 for r in riders)


def _copy_as_bf16(srcs, dsts):
    for src, dst in zip(srcs, dsts):
        dst[...] = src[...].astype(_BF16)


def _row_subtiles(tm):
    step = min(SUB_ROWS, tm)
    return [slice(r, r + step) for r in range(0, tm, step)]


def _project(subs, x_units, w_ref, b_ref, y_ref):
    xb = [x.astype(_BF16) for x in x_units]
    n = y_ref.shape[-1]
    for c in range(0, n, PROJ_CHUNK):
        cols = slice(c, min(c + PROJ_CHUNK, n))
        for s, x in zip(subs, xb):
            y_ref[s, cols] = (_dot(x, w_ref[:, cols]) + b_ref[:, cols]).astype(y_ref.dtype)


def _norm_proj_kernel(x_ref, w_ref, b_ref, *refs):
    n_riders = (len(refs) - 1) // 2
    y_ref = refs[n_riders]
    subs = _row_subtiles(x_ref.shape[0])
    _project(subs, [_rms_unit(x_ref[s, :]) for s in subs], w_ref, b_ref, y_ref)
    _copy_as_bf16(refs[:n_riders], refs[n_riders + 1:])


def _norm_proj(x, w, b, riders=()):
    m, d = x.shape
    n = w[0].shape[-1]
    tm = math.gcd(2 * ROW_TILE, m)
    ride = _can_ride(m // tm, riders)
    r_in, r_args, r_out, r_shapes = _cast_riders(m // tm, riders if ride else ())
    y, *copies = pl.pallas_call(
        _norm_proj_kernel,
        grid=(m // tm,),
        in_specs=[pl.BlockSpec((tm, d), lambda i: (i, 0)), _resident_layer(*w), _resident((1, n))] + r_in,
        out_specs=[pl.BlockSpec((tm, n), lambda i: (i, 0))] + r_out,
        out_shape=[jax.ShapeDtypeStruct((m, n), _BF16)] + r_shapes,
        compiler_params=pltpu.CompilerParams(
            dimension_semantics=("arbitrary",), vmem_limit_bytes=VMEM_LIMIT_BYTES),
        name="norm_proj",
    )(x, w[0], b.reshape(1, n), *r_args)
    return y, copies if ride else [stack[index].astype(_BF16) for stack, index in riders]


def _attn_out_and_mlp(h_ref, a_ref, wo_ref, bo_ref, g_ref, wup_ref, wdn_ref):
    subs = _row_subtiles(h_ref.shape[0])
    h1 =[h_ref[s, :] + _dot(a_ref[s, :], wo_ref[...]) + bo_ref[...] for s in subs]
    xn = [(_rms_unit(x) * g_ref[...]).astype(_BF16) for x in h1]
    acc = h1
    d_ff = wup_ref.shape[1]
    for c in range(0, d_ff, FF_CHUNK):
        u = [jnp.maximum(_dot(x, wup_ref[:, c:c + FF_CHUNK]), 0.0) for x in xn]
        acc = [a + _dot((v * v).astype(_BF16), wdn_ref[c:c + FF_CHUNK, :]) for a, v in zip(acc, u)]
    return subs, acc


def _mid_layer_kernel(h_ref, a_ref, wo_ref, bo_ref, g_ref, wup_ref, wdn_ref, wn_ref, bn_ref,
                      *refs):
    n_riders = (len(refs) - 2) // 2
    o_ref, y_ref = refs[n_riders:n_riders + 2]
    subs, h2 = _attn_out_and_mlp(h_ref, a_ref, wo_ref, bo_ref, g_ref, wup_ref, wdn_ref)
    for s, x in zip(subs, h2):
        o_ref[s, :] = x
    _project(subs, [_rms_unit(x) for x in h2], wn_ref, bn_ref, y_ref)
    _copy_as_bf16(refs[:n_riders], refs[n_riders + 2:])


def _last_layer_kernel(h_ref, a_ref, wo_ref, bo_ref, g_ref, wup_ref, wdn_ref, fg_ref, o_ref):
    subs, h2 = _attn_out_and_mlp(h_ref, a_ref, wo_ref, bo_ref, g_ref, wup_ref, wdn_ref)
    for s, x in zip(subs, h2):
        o_ref[s, :] = _rms_unit(x) * fg_ref[...]


def _layer_tail(h, a, wo, bo, g, wup, wdn, *, next_proj=None, riders=(), final_gain=None):
    m, d = h.shape
    tm = math.gcd(2 * ROW_TILE, m) if next_proj is None else min(ROW_TILE, m)
    row = lambda i: (i, 0)
    in_specs = [
        pl.BlockSpec((tm, d), row),
        pl.BlockSpec((tm, d), row),
        _resident_layer(*wo),
        _resident((1, d)),
        _resident((1, d)),
        _resident_layer(*wup),
        _resident_layer(*wdn),
    ]
    args = [h, a, wo[0], bo.reshape(1, d), g.reshape(1, d), wup[0], wdn[0]]
    params = pltpu.CompilerParams(
        dimension_semantics=("arbitrary",), vmem_limit_bytes=VMEM_LIMIT_BYTES)
    if next_proj is None:
        return pl.pallas_call(
            _last_layer_kernel,
            grid=(m // tm,),
            in_specs=in_specs + [_resident((1, d))],
            out_specs=pl.BlockSpec((tm, d), row),
            out_shape=jax.ShapeDtypeStruct((m, d), _F32),
            compiler_params=params,
            name="last_layer",
        )(*args, final_gain.reshape(1, d))
    wn, bn = next_proj
    n = wn[0].shape[-1]
    ride = _can_ride(m // tm, riders)
    r_in, r_args, r_out, r_shapes = _cast_riders(m // tm, riders if ride else ())
    h2, y, *copies = pl.pallas_call(
        _mid_layer_kernel,
        grid=(m // tm,),
        in_specs=in_specs + [_resident_layer(*wn), _resident((1, n))] + r_in,
        out_specs=[pl.BlockSpec((tm, d), row), pl.BlockSpec((tm, n), row)] + r_out,
        out_shape=[jax.ShapeDtypeStruct((m, d), _F32), jax.ShapeDtypeStruct((m, n), _BF16)] + r_shapes,
        compiler_params=params,
        name="mid_layer",
    )(*args, wn[0], bn.reshape(1, n), *r_args)
    return h2, y, copies if ride else [stack[index].astype(_BF16) for stack, index in riders]


def _sb_attn_kernel(q_ref, k_ref, v_ref, o_ref, acc_ref, carry_ref, least_ref, unfinished_ref,
                    *, window_keys):
    n_t = q_ref.shape[0] // SB_TILE
    row = lax.broadcasted_iota(jnp.int32, (SB_TILE, BLOCK), 0)
    col = lax.broadcasted_iota(jnp.int32, (SB_TILE, BLOCK), 1)
    first_head = col < HEAD_DIM
    key = lax.broadcasted_iota(jnp.int32, (BLOCK, BLOCK), 0)
    pos = lax.broadcasted_iota(jnp.int32, (BLOCK, BLOCK), 1)
    suffix_and_ones = jnp.concatenate(
        [(key >= pos).astype(_BF16), jnp.ones((BLOCK, BLOCK), _BF16)], axis=1)
    suffix_and_ones = jnp.concatenate([suffix_and_ones, suffix_and_ones], axis=0)

    def tile_rows(t):
        start = t * SB_TILE
        return pl.ds(start if isinstance(t, int) else pl.multiple_of(start, SB_TILE), SB_TILE)

    def load_q(t):
        q = q_ref[tile_rows(t), :]
        return jnp.concatenate([jnp.where(first_head, q, 0), jnp.where(first_head, 0, q)], axis=0)

    def window(t):
        end = (t + 1) * SB_TILE
        n_b = min(pl.cdiv(end, BLOCK), window_keys // BLOCK)
        start = max(end - n_b * BLOCK, 0)
        offsets = []
        for b in range(n_b):
            off = t * SB_TILE - start - b * BLOCK
            assert off > -SB_TILE
            offsets.append(None if off >= BLOCK else off)
        return start, n_b, offsets

    def scores(q2, k):
        return _dot_nt(q2, k)

    def suffix_sums(z, masks):
        neg_abs = lax.bitcast_convert_type(
            lax.bitcast_convert_type(z, jnp.uint32) | jnp.uint32(0x80000000), _F32)
        sp = jnp.maximum(z, 0.0) + jnp.log2(1.0 + jnp.exp2(neg_abs))
        split = []
        for h in range(2):
            for b, mask in enumerate(masks):
                s = sp[h * SB_TILE:(h + 1) * SB_TILE, b * BLOCK:(b + 1) * BLOCK]
                if mask is not None:
                    s = jnp.where(mask, s, 0.0)
                hi = s.astype(_BF16)
                lo = (s - hi.astype(_F32)).astype(_BF16)
                split.append(jnp.concatenate([hi, lo], axis=1))
        return _dot(jnp.concatenate(split, axis=0), suffix_and_ones)

    def weighted_values(z, cs, v, carry_in, masks):
        n_b = len(masks)
        ws, carries = [], []
        for h in range(2):
            carry = None if carry_in is None else carry_in[h]
            wb = [None] * n_b
            for b in reversed(range(n_b)):
                r = cs[(h * n_b + b) * SB_TILE:(h * n_b + b + 1) * SB_TILE]
                arg = z[h * SB_TILE:(h + 1) * SB_TILE, b * BLOCK:(b + 1) * BLOCK] - r[:, :BLOCK]
                if carry is not None:
                    arg = arg - carry
                w = jnp.exp2(arg)
                if masks[b] is not None:
                    w = jnp.where(masks[b], w, 0.0)
                wb[b] = w.astype(_BF16)
                carry = r[:, BLOCK:] if carry is None else carry + r[:, BLOCK:]
            ws.append(jnp.concatenate(wb, axis=1))
            carries.append(carry)
        out = _dot(jnp.concatenate(ws, axis=0), v)
        return jnp.where(first_head, out[:SB_TILE], out[SB_TILE:]), carries

    def window_scores(t):
        start, n_b, _ = window(t)
        return scores(load_q(t), k_ref[pl.ds(start, n_b * BLOCK), :])

    zs, css, wins, masks = {}, {}, {}, {}

    def issue_scores(t):
        if t < n_t:
            wins[t] = window(t)
            masks[t] = [None if off is None else col < row + off for off in wins[t][2]]
            zs[t] = window_scores(t)

    def issue_suffix_sums(t):
        if t < n_t:
            css[t] = suffix_sums(zs[t], masks[t])

    for t in range(SB_SCORE_LOOKAHEAD):
        issue_scores(t)
    for t in range(SB_SUFFIX_LOOKAHEAD):
        issue_suffix_sums(t)
    group_unfinished = {}
    for t in range(n_t):
        issue_scores(t + SB_SCORE_LOOKAHEAD)
        issue_suffix_sums(t + SB_SUFFIX_LOOKAHEAD)
        start, n_b, _ = wins.pop(t)
        out, carry = weighted_values(zs.pop(t), css.pop(t), v_ref[pl.ds(start, n_b * BLOCK), :],
                                     None, masks.pop(t))
        o_ref[tile_rows(t), :] = out.astype(o_ref.dtype)
        if start > 0:
            acc_ref[t] = out
            carry_ref[t, 0] = carry[0]
            carry_ref[t, 1] = carry[1]
            least = jnp.min(jnp.minimum(carry[0], carry[1]))
            least_ref[t] = least
            unfinished = least < -SB_UNDERFLOW_LOG2
            g = t // SB_FLAG_GROUP
            group_unfinished[g] = (jnp.logical_or(group_unfinished[g], unfinished)
                                   if g in group_unfinished else unfinished)
    for g, unfinished in group_unfinished.items():
        unfinished_ref[g] = unfinished.astype(jnp.int32)

    first_open = next((t for t in range(n_t) if window(t)[0] > 0), n_t)

    def finish_tile(t, _):
        def more(state):
            remaining, least = state
            return jnp.logical_and(remaining > 0, least < -SB_UNDERFLOW_LOG2)

        def older_block(state):
            remaining, _ = state
            first = jnp.maximum(remaining - BLOCK, 0)
            rows = pl.ds(pl.multiple_of(first, SB_TILE), BLOCK)
            masks = [col < remaining - first]
            z = scores(load_q(t), k_ref[rows, :])
            out, carry = weighted_values(z, suffix_sums(z, masks), v_ref[rows, :],
                                         [carry_ref[t, 0], carry_ref[t, 1]], masks)
            acc_ref[t] += out
            carry_ref[t, 0] = carry[0]
            carry_ref[t, 1] = carry[1]
            return first, jnp.min(jnp.minimum(carry[0], carry[1]))

        @pl.when(least_ref[t] < -SB_UNDERFLOW_LOG2)
        def _():
            lax.while_loop(more, older_block, ((t + 1) * SB_TILE - window_keys, least_ref[t]))
            o_ref[tile_rows(t), :] = acc_ref[t].astype(o_ref.dtype)
        return 0

    def finish_group(g, _):
        @pl.when(unfinished_ref[g] != 0)
        def _():
            lax.fori_loop(jnp.maximum(g * SB_FLAG_GROUP, first_open),
                          jnp.minimum((g + 1) * SB_FLAG_GROUP, n_t), finish_tile, 0)
        return 0

    if n_t > first_open:
        lax.fori_loop(first_open // SB_FLAG_GROUP, pl.cdiv(n_t, SB_FLAG_GROUP), finish_group, 0)


def _sb_attention(qkv, batch, seq, n_heads, window_keys):
    n_pairs = n_heads // 2
    n_t = seq // SB_TILE
    blk = (None, seq, PAIR)
    return pl.pallas_call(
        functools.partial(_sb_attn_kernel, window_keys=window_keys),
        grid=(batch, n_pairs),
        in_specs=[
            pl.BlockSpec(blk, lambda b, p: (b, 0, p)),
            pl.BlockSpec(blk, lambda b, p: (b, 0, n_pairs + p)),
            pl.BlockSpec(blk, lambda b, p: (b, 0, 2 * n_pairs + p)),
        ],
        out_specs=pl.BlockSpec(blk, lambda b, p: (b, 0, p)),
        out_shape=jax.ShapeDtypeStruct((batch, seq, n_heads * HEAD_DIM), _BF16),
        scratch_shapes=[pltpu.VMEM((n_t, SB_TILE, PAIR), _F32),
                        pltpu.VMEM((n_t, 2, SB_TILE, BLOCK), _F32),
                        pltpu.SMEM((n_t,), _F32),
                        pltpu.SMEM((pl.cdiv(n_t, SB_FLAG_GROUP),), jnp.int32)],
        compiler_params=pltpu.CompilerParams(
            dimension_semantics=("arbitrary", "arbitrary"), vmem_limit_bytes=VMEM_LIMIT_BYTES),
        name="sb_attention",
    )(qkv, qkv, qkv)


def _bias_band_kernel(rel_ref, sink_ref, bucket_ref, o_ref):
    layer = pl.program_id(0)
    head = pl.program_id(1)
    bucket = bucket_ref[...]
    qi = lax.broadcasted_iota(jnp.int32, bucket.shape, 0)
    kj = lax.broadcasted_iota(jnp.int32, bucket.shape, 1)
    dist = qi + BLOCK - kj
    valid = (dist >= 0) & (dist < WINDOW)
    bias = jnp.zeros(bucket.shape, _F32)
    for b in range(NUM_BUCKETS):
        bias = jnp.where(bucket == b, rel_ref[b, head] * LOG2_E, bias)
    sink = sink_ref[layer, head] * LOG2_E
    o_ref[0] = jnp.where(kj == 0, sink, jnp.where(valid & (kj >= BLOCK), bias, NEG_INF))
    o_ref[1] = jnp.where(kj == 0, sink, jnp.where(valid, bias, NEG_INF))


def _bias_band(rel_bias, sinks, bucket):
    n_layers, n_heads = sinks.shape
    return pl.pallas_call(
        _bias_band_kernel,
        grid=(n_layers, n_heads),
        in_specs=[
            pl.BlockSpec(memory_space=pltpu.SMEM),
            pl.BlockSpec(memory_space=pltpu.SMEM),
            pl.BlockSpec(bucket.shape, lambda j, h: (0, 0)),
        ],
        out_specs=pl.BlockSpec((None, 2, None) + bucket.shape, lambda j, h: (j, 0, h, 0, 0)),
        out_shape=jax.ShapeDtypeStruct((n_layers, 2, n_heads) + bucket.shape, _F32),
        name="bias_band",
    )(rel_bias, sinks, bucket)


def _band_buckets():
    qi = jnp.arange(BLOCK)[:, None]
    kj = jnp.arange(2 * BLOCK)[None, :]
    n = jnp.maximum(qi + BLOCK - kj, 0)
    nf = jnp.maximum(n, 1).astype(_F32)
    large = MAX_EXACT + (jnp.log(nf / MAX_EXACT) / math.log(MAX_DISTANCE / MAX_EXACT)
                         * (NUM_BUCKETS - MAX_EXACT)).astype(jnp.int32)
    large = jnp.minimum(large, NUM_BUCKETS - 1)
    return jnp.where(n < MAX_EXACT, n, large).astype(jnp.int32)


def _swa_kernel(q_ref, kvp_ref, kvc_ref, bias_ref, o_ref):
    n_pairs = q_ref.shape[-1] // PAIR
    n_kv = kvc_ref.shape[-1] // (2 * PAIR)
    pairs_per_kv = n_pairs // n_kv
    blocks = q_ref.shape[0] // BLOCK
    first_head = lax.broadcasted_iota(jnp.int32, (BLOCK, PAIR), 1) < HEAD_DIM
    ones = jnp.ones((2 * BLOCK, PAIR), _BF16)
    sink_key = lax.broadcasted_iota(jnp.int32, (2 * BLOCK, PAIR), 0) == 0
    first_table = jnp.where(pl.program_id(1) == 0, 0, 1)

    for t in range(blocks):
        rows = slice(t * BLOCK, (t + 1) * BLOCK)
        table = first_table if t == 0 else 1
        for g in range(n_kv):
            kc = slice(g * PAIR, (g + 1) * PAIR)
            vc = slice((n_kv + g) * PAIR, (n_kv + g + 1) * PAIR)
            if t == 0:
                k = jnp.concatenate([kvp_ref[:, kc], kvc_ref[rows, kc]], axis=0)
                v = jnp.concatenate([kvp_ref[:, vc], kvc_ref[rows, vc]], axis=0)
            else:
                band = slice((t - 1) * BLOCK, (t + 1) * BLOCK)
                k = kvc_ref[band, kc]
                v = kvc_ref[band, vc]
            k = jnp.where(sink_key, 0, k)
            v = jnp.where(sink_key, 0, v)
            v_and_ones = jnp.concatenate([v, ones], axis=1)
            qs = []
            for j in range(pairs_per_kv):
                p = g * pairs_per_kv + j
                q = q_ref[rows, p * PAIR:(p + 1) * PAIR]
                qs += [jnp.where(first_head, q, 0), jnp.where(first_head, 0, q)]
            s = _dot_nt(jnp.concatenate(qs, axis=0), k)
            es = []
            for hh in range(2 * pairs_per_kv):
                head = 2 * g * pairs_per_kv + hh
                sh = s[hh * BLOCK:(hh + 1) * BLOCK] + bias_ref[table, head]
                es.append(jnp.exp2(sh - jnp.max(sh, axis=-1, keepdims=True)).astype(_BF16))
            pv = _dot(jnp.concatenate(es, axis=0), v_and_ones)
            for j in range(pairs_per_kv):
                p = g * pairs_per_kv + j
                halves = []
                for h in range(2):
                    r = pv[(2 * j + h) * BLOCK:(2 * j + h + 1) * BLOCK]
                    halves.append(r[:, :PAIR] / r[:, PAIR:])
                o_ref[rows, p * PAIR:(p + 1) * PAIR] = (
                    jnp.where(first_head, halves[0], halves[1]).astype(o_ref.dtype))


def _swa_attention(q_src, kv_src, kv_col, d, kvw, bias_tabs, layer):
    batch, seq, _ = q_src.shape
    blocks = math.gcd(seq // BLOCK, SWA_BLOCKS_PER_STEP)
    tq = blocks * BLOCK
    return pl.pallas_call(
        _swa_kernel,
        grid=(batch, seq // tq),
        in_specs=[
            pl.BlockSpec((None, tq, d), lambda b, n: (b, n, 0)),
            pl.BlockSpec((None, BLOCK, kvw), lambda b, n: (b, jnp.maximum(n * blocks - 1, 0), kv_col)),
            pl.BlockSpec((None, tq, kvw), lambda b, n: (b, n, kv_col)),
            pl.BlockSpec((None,) + bias_tabs.shape[1:], lambda b, n: (layer, 0, 0, 0, 0),
                         pipeline_mode=pl.Buffered(1)),
        ],
        out_specs=pl.BlockSpec((None, tq, d), lambda b, n: (b, n, 0)),
        out_shape=jax.ShapeDtypeStruct((batch, seq, d), _BF16),
        compiler_params=pltpu.CompilerParams(
            dimension_semantics=("arbitrary", "arbitrary"), vmem_limit_bytes=VMEM_LIMIT_BYTES),
        name="swa_attention",
    )(q_src, kv_src, kv_src, bias_tabs)


def _duplicate_heads(w, n_heads):
    lead = w.shape[:-1]
    w = w.reshape(lead + (n_heads, 1, HEAD_DIM))
    w = jnp.broadcast_to(w, lead + (n_heads, 2, HEAD_DIM))
    return w.reshape(lead + (n_heads * PAIR,))


def kernel(x, a_norm, a_wqkv, a_wo, kv_norm, w_kv, b_kv, b_norm, b_wq, b_bq, b_sinks,
           b_wo, b_bo, rel_bias, mlp_norm, mlp_up, mlp_down, final_norm):
    batch, seq, d = x.shape
    m = batch * seq
    n_a = a_wqkv.shape[0]
    n_b = b_wq.shape[0]
    depth = n_a + n_b
    n_heads = d // HEAD_DIM
    n_kv = w_kv.shape[1] // (2 * HEAD_DIM)
    assert seq % BLOCK == 0 and m % min(ROW_TILE, m) == 0 and n_heads % (2 * n_kv) == 0

    bf = lambda w: w.astype(_BF16)
    kvw = 2 * n_kv * PAIR
    q_scale = SCALE * LOG2_E

    a_w = bf(a_norm[:, :, None] * a_wqkv * jnp.where(jnp.arange(a_wqkv.shape[2]) < d, q_scale, 1.0))

    def attention_projection(layer):
        if layer < n_a:
            return (a_w, layer), jnp.zeros((a_w.shape[2],), _F32)
        j = layer - n_a
        w = b_norm[j][:, None] * b_wq[j] * q_scale
        b = b_bq[j] * q_scale
        if j == 0:
            wk, wv = jnp.split(w_kv, 2, axis=-1)
            bk, bv = jnp.split(b_kv, 2, axis=-1)
            w_dup = jnp.concatenate([_duplicate_heads(wk, n_kv), _duplicate_heads(wv, n_kv)], -1)
            b_dup = jnp.concatenate([_duplicate_heads(bk, n_kv), _duplicate_heads(bv, n_kv)], -1)
            w = jnp.concatenate([w, kv_norm[:, None] * w_dup], axis=-1)
            b = jnp.concatenate([b, b_dup], axis=-1)
        return (bf(w)[None], 0), b

    a_wo_bf, b_wo_bf = bf(a_wo), bf(b_wo)
    bias_tabs = _bias_band(rel_bias, b_sinks, _band_buckets())
    h = x.reshape(m, d)
    y, (up, down) = _norm_proj(h, *attention_projection(0), riders=((mlp_up, 0), (mlp_down, 0)))
    y = y.reshape(batch, seq, -1)
    kv_src = None
    for layer in range(depth):
        if layer < n_a:
            attn = _sb_attention(y, batch, seq, n_heads,
                                 SB_WINDOW_FIRST_LAYER if layer == 0 else SB_WINDOW_LATER_LAYERS)
            wo, bo = (a_wo_bf, layer), jnp.zeros((d,), _F32)
        else:
            j = layer - n_a
            if j == 0:
                kv_src = y
            attn = _swa_attention(y, kv_src, d // kvw, d, kvw, bias_tabs, j)
            wo, bo = (b_wo_bf, j), b_bo[j]
        tail = functools.partial(_layer_tail, h, attn.reshape(m, d), wo, bo, mlp_norm[layer],
                                 (up[None], 0), (down[None], 0))
        if layer == depth - 1:
            return tail(final_gain=final_norm).reshape(batch, seq, d)
        h, y, (up, down) = tail(next_proj=attention_projection(layer + 1),
                                riders=((mlp_up, layer + 1), (mlp_down, layer + 1)))
        y = y.reshape(batch, seq, -1)
```

```python
import functools
import math

import jax
import jax.numpy as jnp
from jax import lax
from jax.experimental import pallas as pl
from jax.experimental.pallas import tpu as pltpu

HEAD_DIM = 64
PAIR = 2 * HEAD_DIM
BLOCK = 128
WINDOW = 128
NUM_BUCKETS = 32
MAX_EXACT = NUM_BUCKETS // 2
MAX_DISTANCE = 128
EPS = 1e-5
NEG_INF = -1e30
SCALE = 1.0 / math.sqrt(HEAD_DIM)

LOG2_E = math.log2(math.e)
SB_UNDERFLOW_LOG2 = -151.0
SB_TILE = 64
SB_WINDOW_FIRST_LAYER = 256
SB_WINDOW_LATER_LAYERS = 384
SB_SCORE_LOOKAHEAD = 6
SB_SUFFIX_LOOKAHEAD = 3
SB_FLAG_GROUP = 8

SWA_BLOCKS_PER_STEP = 16

ROW_TILE = 512
FF_CHUNK = 1024
PROJ_CHUNK = 512
SUB_ROWS = 256
VMEM_LIMIT_BYTES = 56 * 1024 * 1024

_F32 = jnp.float32
_BF16 = jnp.bfloat16
BF16_SUBLANES = 16


def _dot(a, b):
    return jnp.dot(a, b, preferred_element_type=_F32)


def _dot_nt(a, b):
    return lax.dot_general(a, b, (((1,), (1,)), ((), ())), preferred_element_type=_F32)


def _rms_unit(x):
    ms = jnp.mean(x * x, axis=-1, keepdims=True)
    return x * lax.rsqrt(ms + EPS)


def _resident(shape):
    return pl.BlockSpec(shape, lambda *_: (0,) * len(shape), pipeline_mode=pl.Buffered(1))


def _resident_layer(stack, index):
    return pl.BlockSpec((None,) + stack.shape[1:], lambda *_: (index,) + (0,) * (stack.ndim - 1),
                        pipeline_mode=pl.Buffered(1))


def _cast_riders(steps, riders):
    in_specs, inputs, out_specs, out_shapes = [], [], [], []
    for stack, index in riders:
        rows, cols = stack.shape[1:]
        share = rows // steps
        in_specs.append(pl.BlockSpec((None, share, cols), lambda i, index=index: (index, i, 0)))
        inputs.append(stack)
        out_specs.append(pl.BlockSpec((share, cols), lambda i: (i, 0)))
        out_shapes.append(jax.ShapeDtypeStruct((rows, cols), _BF16))
    return in_specs, inputs, out_specs, out_shapes


def _can_ride(steps, riders):
    return all(r[0].shape[1] % (steps * BF16_SUBLANES) == 0 for r in riders)


def _copy_as_bf16(srcs, dsts):
    for src, dst in zip(srcs, dsts):
        dst[...] = src[...].astype(_BF16)


def _row_subtiles(tm):
    step = min(SUB_ROWS, tm)
    return [slice(r, r + step) for r in range(0, tm, step)]


def _project(subs, x_units, w_ref, b_ref, y_ref):
    xb = [x.astype(_BF16) for x in x_units]
    n = y_ref.shape[-1]
    for c in range(0, n, PROJ_CHUNK):
        cols = slice(c, min(c + PROJ_CHUNK, n))
        for s, x in zip(subs, xb):
            y_ref[s, cols] = (_dot(x, w_ref[:, cols]) + b_ref[:, cols]).astype(y_ref.dtype)


def _norm_proj_kernel(x_ref, w_ref, b_ref, *refs):
    n_riders = (len(refs) - 1) // 2
    y_ref = refs[n_riders]
    subs = _row_subtiles(x_ref.shape[0])
    _project(subs, [_rms_unit(x_ref[s, :]) for s in subs], w_ref, b_ref, y_ref)
    _copy_as_bf16(refs[:n_riders], refs[n_riders + 1:])


def _norm_proj(x, w, b, riders=()):
    m, d = x.shape
    n = w[0].shape[-1]
    tm = math.gcd(2 * ROW_TILE, m)
    ride = _can_ride(m // tm, riders)
    r_in, r_args, r_out, r_shapes = _cast_riders(m // tm, riders if ride else ())
    y, *copies = pl.pallas_call(
        _norm_proj_kernel,
        grid=(m // tm,),
        in_specs=[pl.BlockSpec((tm, d), lambda i: (i, 0)), _resident_layer(*w), _resident((1, n))] + r_in,
        out_specs=[pl.BlockSpec((tm, n), lambda i: (i, 0))] + r_out,
        out_shape=[jax.ShapeDtypeStruct((m, n), _BF16)] + r_shapes,
        compiler_params=pltpu.CompilerParams(
            dimension_semantics=("arbitrary",), vmem_limit_bytes=VMEM_LIMIT_BYTES),
        name="norm_proj",
    )(x, w[0], b.reshape(1, n), *r_args)
    return y, copies if ride else [stack[index].astype(_BF16) for stack, index in riders]


def _attn_out_and_mlp(h_ref, a_ref, wo_ref, bo_ref, g_ref, wup_ref, wdn_ref):
    subs = _row_subtiles(h_ref.shape[0])
    h1 = [h_ref[s, :] + _dot(a_ref[s, :], wo_ref[...]) + bo_ref[...] for s in subs]
    xn = [(_rms_unit(x) * g_ref[...]).astype(_BF16) for x in h1]
    acc = h1
    d_ff = wup_ref.shape[1]
    for c in range(0, d_ff, FF_CHUNK):
        u = [jnp.maximum(_dot(x, wup_ref[:, c:c + FF_CHUNK]), 0.0) for x in xn]
        acc = [a + _dot((v * v).astype(_BF16), wdn_ref[c:c + FF_CHUNK, :]) for a, v in zip(acc, u)]
    return subs, acc


def _mid_layer_kernel(h_ref, a_ref, wo_ref, bo_ref, g_ref, wup_ref, wdn_ref, wn_ref, bn_ref,
                      *refs):
    n_riders = (len(refs) - 2) // 2
    o_ref, y_ref = refs[n_riders:n_riders + 2]
    subs, h2 = _attn_out_and_mlp(h_ref, a_ref, wo_ref, bo_ref, g_ref, wup_ref, wdn_ref)
    for s, x in zip(subs, h2):
        o_ref[s, :] = x
    _project(subs, [_rms_unit(x) for x in h2], wn_ref, bn_ref, y_ref)
    _copy_as_bf16(refs[:n_riders], refs[n_riders + 2:])


def _last_layer_kernel(h_ref, a_ref, wo_ref, bo_ref, g_ref, wup_ref, wdn_ref, fg_ref, o_ref):
    subs, h2 = _attn_out_and_mlp(h_ref, a_ref, wo_ref, bo_ref, g_ref, wup_ref, wdn_ref)
    for s, x in zip(subs, h2):
        o_ref[s, :] = _rms_unit(x) * fg_ref[...]


def _layer_tail(h, a, wo, bo, g, wup, wdn, *, next_proj=None, riders=(), final_gain=None):
    m, d = h.shape
    tm = math.gcd(2 * ROW_TILE, m) if next_proj is None else min(ROW_TILE, m)
    row = lambda i: (i, 0)
    in_specs = [
        pl.BlockSpec((tm, d), row),
        pl.BlockSpec((tm, d), row),
        _resident_layer(*wo),
        _resident((1, d)),
        _resident((1, d)),
        _resident_layer(*wup),
        _resident_layer(*wdn),
    ]
    args = [h, a, wo[0], bo.reshape(1, d), g.reshape(1, d), wup[0], wdn[0]]
    params = pltpu.CompilerParams(
        dimension_semantics=("arbitrary",), vmem_limit_bytes=VMEM_LIMIT_BYTES)
    if next_proj is None:
        return pl.pallas_call(
            _last_layer_kernel,
            grid=(m // tm,),
            in_specs=in_specs + [_resident((1, d))],
            out_specs=pl.BlockSpec((tm, d), row),
            out_shape=jax.ShapeDtypeStruct((m, d), _F32),
            compiler_params=params,
            name="last_layer",
        )(*args, final_gain.reshape(1, d))
    wn, bn = next_proj
    n = wn[0].shape[-1]
    ride = _can_ride(m // tm, riders)
    r_in, r_args, r_out, r_shapes = _cast_riders(m // tm, riders if ride else ())
    h2, y, *copies = pl.pallas_call(
        _mid_layer_kernel,
        grid=(m // tm,),
        in_specs=in_specs + [_resident_layer(*wn), _resident((1, n))] + r_in,
        out_specs=[pl.BlockSpec((tm, d), row), pl.BlockSpec((tm, n), row)] + r_out,
        out_shape=[jax.ShapeDtypeStruct((m, d), _F32), jax.ShapeDtypeStruct((m, n), _BF16)] + r_shapes,
        compiler_params=params,
        name="mid_layer",
    )(*args, wn[0], bn.reshape(1, n), *r_args)
    return h2, y, copies if ride else [stack[index].astype(_BF16) for stack, index in riders]


def _sb_attn_kernel(q_ref, k_ref, v_ref, o_ref, acc_ref, carry_ref, least_ref, unfinished_ref,
                    *, window_keys):
    n_t = q_ref.shape[0] // SB_TILE
    row = lax.broadcasted_iota(jnp.int32, (SB_TILE, BLOCK), 0)
    col = lax.broadcasted_iota(jnp.int32, (SB_TILE, BLOCK), 1)
    first_head = col < HEAD_DIM
    key = lax.broadcasted_iota(jnp.int32, (BLOCK, BLOCK), 0)
    pos = lax.broadcasted_iota(jnp.int32, (BLOCK, BLOCK), 1)
    suffix_and_ones = jnp.concatenate(
        [(key >= pos).astype(_BF16), jnp.ones((BLOCK, BLOCK), _BF16)], axis=1)
    suffix_and_ones = jnp.concatenate([suffix_and_ones, suffix_and_ones], axis=0)

    def tile_rows(t):
        start = t * SB_TILE
        return pl.ds(start if isinstance(t, int) else pl.multiple_of(start, SB_TILE), SB_TILE)

    def load_q(t):
        q = q_ref[tile_rows(t), :]
        return jnp.concatenate([jnp.where(first_head, q, 0), jnp.where(first_head, 0, q)], axis=0)

    def window(t):
        end = (t + 1) * SB_TILE
        n_b = min(pl.cdiv(end, BLOCK), window_keys // BLOCK)
        start = max(end - n_b * BLOCK, 0)
        offsets = []
        for b in range(n_b):
            off = t * SB_TILE - start - b * BLOCK
            assert off > -SB_TILE
            offsets.append(None if off >= BLOCK else off)
        return start, n_b, offsets

    def scores(q2, k, masks):
        z = _dot_nt(q2, k)
        if all(m is None for m in masks):
            return z
        hidden = jnp.concatenate(
            [jnp.zeros((SB_TILE, BLOCK), _F32) if m is None else jnp.where(m, 0.0, NEG_INF)
             for m in masks], axis=1)
        return z + jnp.concatenate([hidden, hidden], axis=0)

    def suffix_sums(z):
        neg_abs = lax.bitcast_convert_type(
            lax.bitcast_convert_type(z, jnp.uint32) | jnp.uint32(0x80000000), _F32)
        sp = jnp.maximum(z, 0.0) + jnp.log2(1.0 + jnp.exp2(neg_abs))
        split = []
        for h in range(2):
            for b in range(z.shape[1] // BLOCK):
                s = sp[h * SB_TILE:(h + 1) * SB_TILE, b * BLOCK:(b + 1) * BLOCK]
                hi = s.astype(_BF16)
                lo = (s - hi.astype(_F32)).astype(_BF16)
                split.append(jnp.concatenate([hi, lo], axis=1))
        return _dot(jnp.concatenate(split, axis=0), suffix_and_ones)

    def weighted_values(z, cs, v, carry_in):
        n_b = z.shape[1] // BLOCK
        ws, carries = [], []
        for h in range(2):
            carry = None if carry_in is None else carry_in[h]
            wb = [None] * n_b
            for b in reversed(range(n_b)):
                r = cs[(h * n_b + b) * SB_TILE:(h * n_b + b + 1) * SB_TILE]
                arg = z[h * SB_TILE:(h + 1) * SB_TILE, b * BLOCK:(b + 1) * BLOCK] - r[:, :BLOCK]
                if carry is not None:
                    arg = arg - carry
                wb[b] = jnp.exp2(arg).astype(_BF16)
                carry = r[:, BLOCK:] if carry is None else carry + r[:, BLOCK:]
            ws.append(jnp.concatenate(wb, axis=1))
            carries.append(carry)
        out = _dot(jnp.concatenate(ws, axis=0), v)
        return jnp.where(first_head, out[:SB_TILE], out[SB_TILE:]), carries

    def window_scores(t):
        start, n_b, offsets = window(t)
        return scores(load_q(t), k_ref[pl.ds(start, n_b * BLOCK), :],
                      [None if off is None else col < row + off for off in offsets])

    zs, css = {}, {}

    def issue_scores(t):
        if t < n_t:
            zs[t] = window_scores(t)

    def issue_suffix_sums(t):
        if t < n_t:
            css[t] = suffix_sums(zs[t])

    for t in range(SB_SCORE_LOOKAHEAD):
        issue_scores(t)
    for t in range(SB_SUFFIX_LOOKAHEAD):
        issue_suffix_sums(t)
    group_unfinished = {}
    for t in range(n_t):
        issue_scores(t + SB_SCORE_LOOKAHEAD)
        issue_suffix_sums(t + SB_SUFFIX_LOOKAHEAD)
        start, n_b, _ = window(t)
        out, carry = weighted_values(zs.pop(t), css.pop(t), v_ref[pl.ds(start, n_b * BLOCK), :], None)
        o_ref[tile_rows(t), :] = out.astype(o_ref.dtype)
        if start > 0:
            acc_ref[t] = out
            carry_ref[t, 0] = carry[0]
            carry_ref[t, 1] = carry[1]
            least = jnp.min(jnp.minimum(carry[0], carry[1]))
            least_ref[t] = least
            unfinished = least < -SB_UNDERFLOW_LOG2
            g = t // SB_FLAG_GROUP
            group_unfinished[g] = (jnp.logical_or(group_unfinished[g], unfinished)
                                   if g in group_unfinished else unfinished)
    for g, unfinished in group_unfinished.items():
        unfinished_ref[g] = unfinished.astype(jnp.int32)

    first_open = next((t for t in range(n_t) if window(t)[0] > 0), n_t)

    def finish_tile(t, _):
        def more(state):
            remaining, least = state
            return jnp.logical_and(remaining > 0, least < -SB_UNDERFLOW_LOG2)

        def older_block(state):
            remaining, _ = state
            first = jnp.maximum(remaining - BLOCK, 0)
            rows = pl.ds(pl.multiple_of(first, SB_TILE), BLOCK)
            z = scores(load_q(t), k_ref[rows, :], [col < remaining - first])
            out, carry = weighted_values(z, suffix_sums(z), v_ref[rows, :],
                                         [carry_ref[t, 0], carry_ref[t, 1]])
            acc_ref[t] += out
            carry_ref[t, 0] = carry[0]
            carry_ref[t, 1] = carry[1]
            return first, jnp.min(jnp.minimum(carry[0], carry[1]))

        @pl.when(least_ref[t] < -SB_UNDERFLOW_LOG2)
        def _():
            lax.while_loop(more, older_block, ((t + 1) * SB_TILE - window_keys, least_ref[t]))
            o_ref[tile_rows(t), :] = acc_ref[t].astype(o_ref.dtype)
        return 0

    def finish_group(g, _):
        @pl.when(unfinished_ref[g] != 0)
        def _():
            lax.fori_loop(jnp.maximum(g * SB_FLAG_GROUP, first_open),
                          jnp.minimum((g + 1) * SB_FLAG_GROUP, n_t), finish_tile, 0)
        return 0

    if n_t > first_open:
        lax.fori_loop(first_open // SB_FLAG_GROUP, pl.cdiv(n_t, SB_FLAG_GROUP), finish_group, 0)


def _sb_attention(qkv, batch, seq, n_heads, window_keys):
    n_pairs = n_heads // 2
    n_t = seq // SB_TILE
    blk = (None, seq, PAIR)
    return pl.pallas_call(
        functools.partial(_sb_attn_kernel, window_keys=window_keys),
        grid=(batch, n_pairs),
        in_specs=[
            pl.BlockSpec(blk, lambda b, p: (b, 0, p)),
            pl.BlockSpec(blk, lambda b, p: (b, 0, n_pairs + p)),
            pl.BlockSpec(blk, lambda b, p: (b, 0, 2 * n_pairs + p)),
        ],
        out_specs=pl.BlockSpec(blk, lambda b, p: (b, 0, p)),
        out_shape=jax.ShapeDtypeStruct((batch, seq, n_heads * HEAD_DIM), _BF16),
        scratch_shapes=[pltpu.VMEM((n_t, SB_TILE, PAIR), _F32),
                        pltpu.VMEM((n_t, 2, SB_TILE, BLOCK), _F32),
                        pltpu.SMEM((n_t,), _F32),
                        pltpu.SMEM((pl.cdiv(n_t, SB_FLAG_GROUP),), jnp.int32)],
        compiler_params=pltpu.CompilerParams(
            dimension_semantics=("arbitrary", "arbitrary"), vmem_limit_bytes=VMEM_LIMIT_BYTES),
        name="sb_attention",
    )(qkv, qkv, qkv)


def _bias_band_kernel(rel_ref, sink_ref, bucket_ref, o_ref):
    layer = pl.program_id(0)
    head = pl.program_id(1)
    bucket = bucket_ref[...]
    qi = lax.broadcasted_iota(jnp.int32, bucket.shape, 0)
    kj = lax.broadcasted_iota(jnp.int32, bucket.shape, 1)
    dist = qi + BLOCK - kj
    valid = (dist >= 0) & (dist < WINDOW)
    bias = jnp.zeros(bucket.shape, _F32)
    for b in range(NUM_BUCKETS):
        bias = jnp.where(bucket == b, rel_ref[b, head] * LOG2_E, bias)
    sink = sink_ref[layer, head] * LOG2_E
    o_ref[0] = jnp.where(kj == 0, sink, jnp.where(valid & (kj >= BLOCK), bias, NEG_INF))
    o_ref[1] = jnp.where(kj == 0, sink, jnp.where(valid, bias, NEG_INF))


def _bias_band(rel_bias, sinks, bucket):
    n_layers, n_heads = sinks.shape
    return pl.pallas_call(
        _bias_band_kernel,
        grid=(n_layers, n_heads),
        in_specs=[
            pl.BlockSpec(memory_space=pltpu.SMEM),
            pl.BlockSpec(memory_space=pltpu.SMEM),
            pl.BlockSpec(bucket.shape, lambda j, h: (0, 0)),
        ],
        out_specs=pl.BlockSpec((None, 2, None) + bucket.shape, lambda j, h: (j, 0, h, 0, 0)),
        out_shape=jax.ShapeDtypeStruct((n_layers, 2, n_heads) + bucket.shape, _F32),
        name="bias_band",
    )(rel_bias, sinks, bucket)


def _band_buckets():
    qi = jnp.arange(BLOCK)[:, None]
    kj = jnp.arange(2 * BLOCK)[None, :]
    n = jnp.maximum(qi + BLOCK - kj, 0)
    nf = jnp.maximum(n, 1).astype(_F32)
    large = MAX_EXACT + (jnp.log(nf / MAX_EXACT) / math.log(MAX_DISTANCE / MAX_EXACT)
                         * (NUM_BUCKETS - MAX_EXACT)).astype(jnp.int32)
    large = jnp.minimum(large, NUM_BUCKETS - 1)
    return jnp.where(n < MAX_EXACT, n, large).astype(jnp.int32)


def _swa_kernel(q_ref, kvp_ref, kvc_ref, bias_ref, o_ref):
    n_pairs = q_ref.shape[-1] // PAIR
    n_kv = kvc_ref.shape[-1] // (2 * PAIR)
    pairs_per_kv = n_pairs // n_kv
    blocks = q_ref.shape[0] // BLOCK
    first_head = lax.broadcasted_iota(jnp.int32, (BLOCK, PAIR), 1) < HEAD_DIM
    ones = jnp.ones((2 * BLOCK, PAIR), _BF16)
    sink_key = lax.broadcasted_iota(jnp.int32, (2 * BLOCK, PAIR), 0) == 0
    first_table = jnp.where(pl.program_id(1) == 0, 0, 1)

    for t in range(blocks):
        rows = slice(t * BLOCK, (t + 1) * BLOCK)
        table = first_table if t == 0 else 1
        for g in range(n_kv):
            kc = slice(g * PAIR, (g + 1) * PAIR)
            vc = slice((n_kv + g) * PAIR, (n_kv + g + 1) * PAIR)
            if t == 0:
                k = jnp.concatenate([kvp_ref[:, kc], kvc_ref[rows, kc]], axis=0)
                v = jnp.concatenate([kvp_ref[:, vc], kvc_ref[rows, vc]], axis=0)
            else:
                band = slice((t - 1) * BLOCK, (t + 1) * BLOCK)
                k = kvc_ref[band, kc]
                v = kvc_ref[band, vc]
            k = jnp.where(sink_key, 0, k)
            v = jnp.where(sink_key, 0, v)
            v_and_ones = jnp.concatenate([v, ones], axis=1)
            qs = []
            for j in range(pairs_per_kv):
                p = g * pairs_per_kv + j
                q = q_ref[rows, p * PAIR:(p + 1) * PAIR]
                qs += [jnp.where(first_head, q, 0), jnp.where(first_head, 0, q)]
            s = _dot_nt(jnp.concatenate(qs, axis=0), k)
            es = []
            for hh in range(2 * pairs_per_kv):
                head = 2 * g * pairs_per_kv + hh
                sh = s[hh * BLOCK:(hh + 1) * BLOCK] + bias_ref[table, head]
                es.append(jnp.exp2(sh - jnp.max(sh, axis=-1, keepdims=True)).astype(_BF16))
            pv = _dot(jnp.concatenate(es, axis=0), v_and_ones)
            for j in range(pairs_per_kv):
                p = g * pairs_per_kv + j
                halves = []
                for h in range(2):
                    r = pv[(2 * j + h) * BLOCK:(2 * j + h + 1) * BLOCK]
                    halves.append(r[:, :PAIR] / r[:, PAIR:])
                o_ref[rows, p * PAIR:(p + 1) * PAIR] = (
                    jnp.where(first_head, halves[0], halves[1]).astype(o_ref.dtype))


def _swa_attention(q_src, kv_src, kv_col, d, kvw, bias_tabs, layer):
    batch, seq, _ = q_src.shape
    blocks = math.gcd(seq // BLOCK, SWA_BLOCKS_PER_STEP)
    tq = blocks * BLOCK
    return pl.pallas_call(
        _swa_kernel,
        grid=(batch, seq // tq),
        in_specs=[
            pl.BlockSpec((None, tq, d), lambda b, n: (b, n, 0)),
            pl.BlockSpec((None, BLOCK, kvw), lambda b, n: (b, jnp.maximum(n * blocks - 1, 0), kv_col)),
            pl.BlockSpec((None, tq, kvw), lambda b, n: (b, n, kv_col)),
            pl.BlockSpec((None,) + bias_tabs.shape[1:], lambda b, n: (layer, 0, 0, 0, 0),
                         pipeline_mode=pl.Buffered(1)),
        ],
        out_specs=pl.BlockSpec((None, tq, d), lambda b, n: (b, n, 0)),
        out_shape=jax.ShapeDtypeStruct((batch, seq, d), _BF16),
        compiler_params=pltpu.CompilerParams(
            dimension_semantics=("arbitrary", "arbitrary"), vmem_limit_bytes=VMEM_LIMIT_BYTES),
        name="swa_attention",
    )(q_src, kv_src, kv_src, bias_tabs)


def _duplicate_heads(w, n_heads):
    lead = w.shape[:-1]
    w = w.reshape(lead + (n_heads, 1, HEAD_DIM))
    w = jnp.broadcast_to(w, lead + (n_heads, 2, HEAD_DIM))
    return w.reshape(lead + (n_heads * PAIR,))


def kernel(x, a_norm, a_wqkv, a_wo, kv_norm, w_kv, b_kv, b_norm, b_wq, b_bq, b_sinks,
           b_wo, b_bo, rel_bias, mlp_norm, mlp_up, mlp_down, final_norm):
    batch, seq, d = x.shape
    m = batch * seq
    n_a = a_wqkv.shape[0]
    n_b = b_wq.shape[0]
    depth = n_a + n_b
    n_heads = d // HEAD_DIM
    n_kv = w_kv.shape[1] // (2 * HEAD_DIM)
    assert seq % BLOCK == 0 and m % min(ROW_TILE, m) == 0 and n_heads % (2 * n_kv) == 0

    bf = lambda w: w.astype(_BF16)
    kvw = 2 * n_kv * PAIR
    q_scale = SCALE * LOG2_E

    a_w = bf(a_norm[:, :, None] * a_wqkv * jnp.where(jnp.arange(a_wqkv.shape[2]) < d, q_scale, 1.0))

    def attention_projection(layer):
        if layer < n_a:
            return (a_w, layer), jnp.zeros((a_w.shape[2],), _F32)
        j = layer - n_a
        w = b_norm[j][:, None] * b_wq[j] * q_scale
        b = b_bq[j] * q_scale
        if j == 0:
            wk, wv = jnp.split(w_kv, 2, axis=-1)
            bk, bv = jnp.split(b_kv, 2, axis=-1)
            w_dup = jnp.concatenate([_duplicate_heads(wk, n_kv), _duplicate_heads(wv, n_kv)], -1)
            b_dup = jnp.concatenate([_duplicate_heads(bk, n_kv), _duplicate_heads(bv, n_kv)], -1)
            w = jnp.concatenate([w, kv_norm[:, None] * w_dup], axis=-1)
            b = jnp.concatenate([b, b_dup], axis=-1)
        return (bf(w)[None], 0), b

    a_wo_bf, b_wo_bf = bf(a_wo), bf(b_wo)
    bias_tabs = _bias_band(rel_bias, b_sinks, _band_buckets())
    h = x.reshape(m, d)
    y, (up, down) = _norm_proj(h, *attention_projection(0), riders=((mlp_up, 0), (mlp_down, 0)))
    y = y.reshape(batch, seq, -1)
    kv_src = None
    for layer in range(depth):
        if layer < n_a:
            attn = _sb_attention(y, batch, seq, n_heads,
                                 SB_WINDOW_FIRST_LAYER if layer == 0 else SB_WINDOW_LATER_LAYERS)
            wo, bo = (a_wo_bf, layer), jnp.zeros((d,), _F32)
        else:
            j = layer - n_a
            if j == 0:
                kv_src = y
            attn = _swa_attention(y, kv_src, d // kvw, d, kvw, bias_tabs, j)
            wo, bo = (b_wo_bf, j), b_bo[j]
        tail = functools.partial(_layer_tail, h, attn.reshape(m, d), wo, bo, mlp_norm[layer],
                                 (up[None], 0), (down[None], 0))
        if layer == depth - 1:
            return tail(final_gain=final_norm).reshape(batch, seq, d)
        h, y, (up, down) = tail(next_proj=attention_projection(layer + 1),
                                riders=((mlp_up, layer + 1), (mlp_down, layer + 1)))
        y = y.reshape(batch, seq, -1)
```

```python
import functools
import math

import jax
import jax.numpy as jnp
from jax import lax
from jax.experimental import pallas as pl
from jax.experimental.pallas import tpu as pltpu

HEAD_DIM = 64
PAIR = 2 * HEAD_DIM
BLOCK = 128
WINDOW = 128
NUM_BUCKETS = 32
MAX_EXACT = NUM_BUCKETS // 2
MAX_DISTANCE = 128
EPS = 1e-5
NEG_INF = -1e30
SCALE = 1.0 / math.sqrt(HEAD_DIM)

LOG2_E = math.log2(math.e)
SB_UNDERFLOW_LOG2 = -151.0
SB_TILE = 64
SB_WINDOW_FIRST_LAYER = 256
SB_WINDOW_LATER_LAYERS = 384
SB_SCORE_LOOKAHEAD = 4
SB_SUFFIX_LOOKAHEAD = 3
SB_FLAG_GROUP = 8

SWA_BLOCKS_PER_STEP = 16

ROW_TILE = 512
FF_CHUNK = 1024
PROJ_CHUNK = 512
SUB_ROWS = 256
VMEM_LIMIT_BYTES = 56 * 1024 * 1024

_F32 = jnp.float32
_BF16 = jnp.bfloat16
BF16_SUBLANES = 16


def _dot(a, b):
    return jnp.dot(a, b, preferred_element_type=_F32)


def _dot_nt(a, b):
    return lax.dot_general(a, b, (((1,), (1,)), ((), ())), preferred_element_type=_F32)


def _rms_unit(x):
    ms = jnp.mean(x * x, axis=-1, keepdims=True)
    return x * lax.rsqrt(ms + EPS)


def _resident(shape):
    return pl.BlockSpec(shape, lambda *_: (0,) * len(shape), pipeline_mode=pl.Buffered(1))


def _resident_layer(stack, index):
    return pl.BlockSpec((None,) + stack.shape[1:], lambda *_: (index,) + (0,) * (stack.ndim - 1),
                        pipeline_mode=pl.Buffered(1))


def _cast_riders(steps, riders):
    in_specs, inputs, out_specs, out_shapes = [], [], [], []
    for stack, index in riders:
        rows, cols = stack.shape[1:]
        share = rows // steps
        in_specs.append(pl.BlockSpec((None, share, cols), lambda i, index=index: (index, i, 0)))
        inputs.append(stack)
        out_specs.append(pl.BlockSpec((share, cols), lambda i: (i, 0)))
        out_shapes.append(jax.ShapeDtypeStruct((rows, cols), _BF16))
    return in_specs, inputs, out_specs, out_shapes


def _can_ride(steps, riders):
    return all(r[0].shape[1] % (steps * BF16_SUBLANES) == 0 for r in riders)


def _copy_as_bf16(srcs, dsts):
    for src, dst in zip(srcs, dsts):
        dst[...] = src[...].astype(_BF16)


def _row_subtiles(tm):
    step = min(SUB_ROWS, tm)
    return [slice(r, r + step) for r in range(0, tm, step)]


def _project(subs, x_units, w_ref, b_ref, y_ref):
    xb = [x.astype(_BF16) for x in x_units]
    n = y_ref.shape[-1]
    for c in range(0, n, PROJ_CHUNK):
        cols = slice(c, min(c + PROJ_CHUNK, n))
        for s, x in zip(subs, xb):
            y_ref[s, cols] = (_dot(x, w_ref[:, cols]) + b_ref[:, cols]).astype(y_ref.dtype)


def _norm_proj_kernel(x_ref, w_ref, b_ref, *refs):
    n_riders = (len(refs) - 1) // 2
    y_ref = refs[n_riders]
    subs = _row_subtiles(x_ref.shape[0])
    _project(subs, [_rms_unit(x_ref[s, :]) for s in subs], w_ref, b_ref, y_ref)
    _copy_as_bf16(refs[:n_riders], refs[n_riders + 1:])


def _norm_proj(x, w, b, riders=()):
    m, d = x.shape
    n = w[0].shape[-1]
    tm = math.gcd(2 * ROW_TILE, m)
    ride = _can_ride(m // tm, riders)
    r_in, r_args, r_out, r_shapes = _cast_riders(m // tm, riders if ride else ())
    y, *copies = pl.pallas_call(
        _norm_proj_kernel,
        grid=(m // tm,),
        in_specs=[pl.BlockSpec((tm, d), lambda i: (i, 0)), _resident_layer(*w), _resident((1, n))] + r_in,
        out_specs=[pl.BlockSpec((tm, n), lambda i: (i, 0))] + r_out,
        out_shape=[jax.ShapeDtypeStruct((m, n), _BF16)] + r_shapes,
        compiler_params=pltpu.CompilerParams(
            dimension_semantics=("arbitrary",), vmem_limit_bytes=VMEM_LIMIT_BYTES),
        name="norm_proj",
    )(x, w[0], b.reshape(1, n), *r_args)
    return y, copies if ride else [stack[index].astype(_BF16) for stack, index in riders]


def _attn_out_and_mlp(h_ref, a_ref, wo_ref, bo_ref, g_ref, wup_ref, wdn_ref):
    subs = _row_subtiles(h_ref.shape[0])
    h1 = [h_ref[s, :] + _dot(a_ref[s, :], wo_ref[...]) + bo_ref[...] for s in subs]
    xn = [(_rms_unit(x) * g_ref[...]).astype(_BF16) for x in h1]
    acc = h1
    d_ff = wup_ref.shape[1]
    for c in range(0, d_ff, FF_CHUNK):
        u = [jnp.maximum(_dot(x, wup_ref[:, c:c + FF_CHUNK]), 0.0) for x in xn]
        acc = [a + _dot((v * v).astype(_BF16), wdn_ref[c:c + FF_CHUNK, :]) for a, v in zip(acc, u)]
    return subs, acc


def _mid_layer_kernel(h_ref, a_ref, wo_ref, bo_ref, g_ref, wup_ref, wdn_ref, wn_ref, bn_ref,
                      *refs):
    n_riders = (len(refs) - 2) // 2
    o_ref, y_ref = refs[n_riders:n_riders + 2]
    subs, h2 = _attn_out_and_mlp(h_ref, a_ref, wo_ref, bo_ref, g_ref, wup_ref, wdn_ref)
    for s, x in zip(subs, h2):
        o_ref[s, :] = x
    _project(subs, [_rms_unit(x) for x in h2], wn_ref, bn_ref, y_ref)
    _copy_as_bf16(refs[:n_riders], refs[n_riders + 2:])


def _last_layer_kernel(h_ref, a_ref, wo_ref, bo_ref, g_ref, wup_ref, wdn_ref, fg_ref, o_ref):
    subs, h2 = _attn_out_and_mlp(h_ref, a_ref, wo_ref, bo_ref, g_ref, wup_ref, wdn_ref)
    for s, x in zip(subs, h2):
        o_ref[s, :] = _rms_unit(x) * fg_ref[...]


def _layer_tail(h, a, wo, bo, g, wup, wdn, *, next_proj=None, riders=(), final_gain=None):
    m, d = h.shape
    tm = math.gcd(2 * ROW_TILE, m) if next_proj is None else min(ROW_TILE, m)
    row = lambda i: (i, 0)
    in_specs = [
        pl.BlockSpec((tm, d), row),
        pl.BlockSpec((tm, d), row),
        _resident_layer(*wo),
        _resident((1, d)),
        _resident((1, d)),
        _resident_layer(*wup),
        _resident_layer(*wdn),
    ]
    args = [h, a, wo[0], bo.reshape(1, d), g.reshape(1, d), wup[0], wdn[0]]
    params = pltpu.CompilerParams(
        dimension_semantics=("arbitrary",), vmem_limit_bytes=VMEM_LIMIT_BYTES)
    if next_proj is None:
        return pl.pallas_call(
            _last_layer_kernel,
            grid=(m // tm,),
            in_specs=in_specs + [_resident((1, d))],
            out_specs=pl.BlockSpec((tm, d), row),
            out_shape=jax.ShapeDtypeStruct((m, d), _F32),
            compiler_params=params,
            name="last_layer",
        )(*args, final_gain.reshape(1, d))
    wn, bn = next_proj
    n = wn[0].shape[-1]
    ride = _can_ride(m // tm, riders)
    r_in, r_args, r_out, r_shapes = _cast_riders(m // tm, riders if ride else ())
    h2, y, *copies = pl.pallas_call(
        _mid_layer_kernel,
        grid=(m // tm,),
        in_specs=in_specs + [_resident_layer(*wn), _resident((1, n))] + r_in,
        out_specs=[pl.BlockSpec((tm, d), row), pl.BlockSpec((tm, n), row)] + r_out,
        out_shape=[jax.ShapeDtypeStruct((m, d), _F32), jax.ShapeDtypeStruct((m, n), _BF16)] + r_shapes,
        compiler_params=params,
        name="mid_layer",
    )(*args, wn[0], bn.reshape(1, n), *r_args)
    return h2, y, copies if ride else [stack[index].astype(_BF16) for stack, index in riders]


def _sb_attn_kernel(q_ref, k_ref, v_ref, o_ref, acc_ref, carry_ref, least_ref, unfinished_ref,
                    *, window_keys):
    n_t = q_ref.shape[0] // SB_TILE
    row = lax.broadcasted_iota(jnp.int32, (SB_TILE, BLOCK), 0)
    col = lax.broadcasted_iota(jnp.int32, (SB_TILE, BLOCK), 1)
    first_head = col < HEAD_DIM
    key = lax.broadcasted_iota(jnp.int32, (BLOCK, BLOCK), 0)
    pos = lax.broadcasted_iota(jnp.int32, (BLOCK, BLOCK), 1)
    suffix_and_ones = jnp.concatenate(
        [(key >= pos).astype(_BF16), jnp.ones((BLOCK, BLOCK), _BF16)], axis=1)
    suffix_and_ones = jnp.concatenate([suffix_and_ones, suffix_and_ones], axis=0)

    def tile_rows(t):
        start = t * SB_TILE
        return pl.ds(start if isinstance(t, int) else pl.multiple_of(start, SB_TILE), SB_TILE)

    def load_q(t):
        q = q_ref[tile_rows(t), :]
        return jnp.concatenate([jnp.where(first_head, q, 0), jnp.where(first_head, 0, q)], axis=0)

    def window(t):
        end = (t + 1) * SB_TILE
        n_b = min(pl.cdiv(end, BLOCK), window_keys // BLOCK)
        start = max(end - n_b * BLOCK, 0)
        offsets = []
        for b in range(n_b):
            off = t * SB_TILE - start - b * BLOCK
            assert off > -SB_TILE
            offsets.append(None if off >= BLOCK else off)
        return start, n_b, offsets

    def scores(q2, k, masks):
        z = _dot_nt(q2, k)
        if all(m is None for m in masks):
            return z
        hidden = jnp.concatenate(
            [jnp.zeros((SB_TILE, BLOCK), _F32) if m is None else jnp.where(m, 0.0, NEG_INF)
             for m in masks], axis=1)
        return z + jnp.concatenate([hidden, hidden], axis=0)

    def suffix_sums(z):
        neg_abs = lax.bitcast_convert_type(
            lax.bitcast_convert_type(z, jnp.uint32) | jnp.uint32(0x80000000), _F32)
        sp = jnp.maximum(z, 0.0) + jnp.log2(1.0 + jnp.exp2(neg_abs))
        split = []
        for h in range(2):
            for b in range(z.shape[1] // BLOCK):
                s = sp[h * SB_TILE:(h + 1) * SB_TILE, b * BLOCK:(b + 1) * BLOCK]
                hi = s.astype(_BF16)
                lo = (s - hi.astype(_F32)).astype(_BF16)
                split.append(jnp.concatenate([hi, lo], axis=1))
        return _dot(jnp.concatenate(split, axis=0), suffix_and_ones)

    def weighted_values(z, cs, v, carry_in):
        n_b = z.shape[1] // BLOCK
        ws, carries = [], []
        for h in range(2):
            carry = None if carry_in is None else carry_in[h]
            wb = [None] * n_b
            for b in reversed(range(n_b)):
                r = cs[(h * n_b + b) * SB_TILE:(h * n_b + b + 1) * SB_TILE]
                arg = z[h * SB_TILE:(h + 1) * SB_TILE, b * BLOCK:(b + 1) * BLOCK] - r[:, :BLOCK]
                if carry is not None:
                    arg = arg - carry
                wb[b] = jnp.exp2(arg).astype(_BF16)
                carry = r[:, BLOCK:] if carry is None else carry + r[:, BLOCK:]
            ws.append(jnp.concatenate(wb, axis=1))
            carries.append(carry)
        out = _dot(jnp.concatenate(ws, axis=0), v)
        return jnp.where(first_head, out[:SB_TILE], out[SB_TILE:]), carries

    def window_scores(t):
        start, n_b, offsets = window(t)
        return scores(load_q(t), k_ref[pl.ds(start, n_b * BLOCK), :],
                      [None if off is None else col < row + off for off in offsets])

    zs, css = {}, {}

    def issue_scores(t):
        if t < n_t:
            zs[t] = window_scores(t)

    def issue_suffix_sums(t):
        if t < n_t:
            css[t] = suffix_sums(zs[t])

    for t in range(SB_SCORE_LOOKAHEAD):
        issue_scores(t)
    for t in range(SB_SUFFIX_LOOKAHEAD):
        issue_suffix_sums(t)
    group_unfinished = {}
    for t in range(n_t):
        issue_scores(t + SB_SCORE_LOOKAHEAD)
        issue_suffix_sums(t + SB_SUFFIX_LOOKAHEAD)
        start, n_b, _ = window(t)
        out, carry = weighted_values(zs.pop(t), css.pop(t), v_ref[pl.ds(start, n_b * BLOCK), :], None)
        o_ref[tile_rows(t), :] = out.astype(o_ref.dtype)
        if start > 0:
            acc_ref[t] = out
            carry_ref[t, 0] = carry[0]
            carry_ref[t, 1] = carry[1]
            least = jnp.min(jnp.minimum(carry[0], carry[1]))
            least_ref[t] = least
            unfinished = least < -SB_UNDERFLOW_LOG2
            g = t // SB_FLAG_GROUP
            group_unfinished[g] = (jnp.logical_or(group_unfinished[g], unfinished)
                                   if g in group_unfinished else unfinished)
    for g, unfinished in group_unfinished.items():
        unfinished_ref[g] = unfinished.astype(jnp.int32)

    first_open = next((t for t in range(n_t) if window(t)[0] > 0), n_t)

    def finish_tile(t, _):
        def more(state):
            remaining, least = state
            return jnp.logical_and(remaining > 0, least < -SB_UNDERFLOW_LOG2)

        def older_block(state):
            remaining, _ = state
            first = jnp.maximum(remaining - BLOCK, 0)
            rows = pl.ds(pl.multiple_of(first, SB_TILE), BLOCK)
            z = scores(load_q(t), k_ref[rows, :], [col < remaining - first])
            out, carry = weighted_values(z, suffix_sums(z), v_ref[rows, :],
                                         [carry_ref[t, 0], carry_ref[t, 1]])
            acc_ref[t] += out
            carry_ref[t, 0] = carry[0]
            carry_ref[t, 1] = carry[1]
            return first, jnp.min(jnp.minimum(carry[0], carry[1]))

        @pl.when(least_ref[t] < -SB_UNDERFLOW_LOG2)
        def _():
            lax.while_loop(more, older_block, ((t + 1) * SB_TILE - window_keys, least_ref[t]))
            o_ref[tile_rows(t), :] = acc_ref[t].astype(o_ref.dtype)
        return 0

    def finish_group(g, _):
        @pl.when(unfinished_ref[g] != 0)
        def _():
            lax.fori_loop(jnp.maximum(g * SB_FLAG_GROUP, first_open),
                          jnp.minimum((g + 1) * SB_FLAG_GROUP, n_t), finish_tile, 0)
        return 0

    if n_t > first_open:
        lax.fori_loop(first_open // SB_FLAG_GROUP, pl.cdiv(n_t, SB_FLAG_GROUP), finish_group, 0)


def _sb_attention(qkv, batch, seq, n_heads, window_keys):
    n_pairs = n_heads // 2
    n_t = seq // SB_TILE
    blk = (None, seq, PAIR)
    return pl.pallas_call(
        functools.partial(_sb_attn_kernel, window_keys=window_keys),
        grid=(batch, n_pairs),
        in_specs=[
            pl.BlockSpec(blk, lambda b, p: (b, 0, p)),
            pl.BlockSpec(blk, lambda b, p: (b, 0, n_pairs + p)),
            pl.BlockSpec(blk, lambda b, p: (b, 0, 2 * n_pairs + p)),
        ],
        out_specs=pl.BlockSpec(blk, lambda b, p: (b, 0, p)),
        out_shape=jax.ShapeDtypeStruct((batch, seq, n_heads * HEAD_DIM), _BF16),
        scratch_shapes=[pltpu.VMEM((n_t, SB_TILE, PAIR), _F32),
                        pltpu.VMEM((n_t, 2, SB_TILE, BLOCK), _F32),
                        pltpu.SMEM((n_t,), _F32),
                        pltpu.SMEM((pl.cdiv(n_t, SB_FLAG_GROUP),), jnp.int32)],
        compiler_params=pltpu.CompilerParams(
            dimension_semantics=("arbitrary", "arbitrary"), vmem_limit_bytes=VMEM_LIMIT_BYTES),
        name="sb_attention",
    )(qkv, qkv, qkv)


def _bias_band_kernel(rel_ref, sink_ref, bucket_ref, o_ref):
    layer = pl.program_id(0)
    head = pl.program_id(1)
    bucket = bucket_ref[...]
    qi = lax.broadcasted_iota(jnp.int32, bucket.shape, 0)
    kj = lax.broadcasted_iota(jnp.int32, bucket.shape, 1)
    dist = qi + BLOCK - kj
    valid = (dist >= 0) & (dist < WINDOW)
    bias = jnp.zeros(bucket.shape, _F32)
    for b in range(NUM_BUCKETS):
        bias = jnp.where(bucket == b, rel_ref[b, head] * LOG2_E, bias)
    sink = sink_ref[layer, head] * LOG2_E
    o_ref[0] = jnp.where(kj == 0, sink, jnp.where(valid & (kj >= BLOCK), bias, NEG_INF))
    o_ref[1] = jnp.where(kj == 0, sink, jnp.where(valid, bias, NEG_INF))


def _bias_band(rel_bias, sinks, bucket):
    n_layers, n_heads = sinks.shape
    return pl.pallas_call(
        _bias_band_kernel,
        grid=(n_layers, n_heads),
        in_specs=[
            pl.BlockSpec(memory_space=pltpu.SMEM),
            pl.BlockSpec(memory_space=pltpu.SMEM),
            pl.BlockSpec(bucket.shape, lambda j, h: (0, 0)),
        ],
        out_specs=pl.BlockSpec((None, 2, None) + bucket.shape, lambda j, h: (j, 0, h, 0, 0)),
        out_shape=jax.ShapeDtypeStruct((n_layers, 2, n_heads) + bucket.shape, _F32),
        name="bias_band",
    )(rel_bias, sinks, bucket)


def _band_buckets():
    qi = jnp.arange(BLOCK)[:, None]
    kj = jnp.arange(2 * BLOCK)[None, :]
    n = jnp.maximum(qi + BLOCK - kj, 0)
    nf = jnp.maximum(n, 1).astype(_F32)
    large = MAX_EXACT + (jnp.log(nf / MAX_EXACT) / math.log(MAX_DISTANCE / MAX_EXACT)
                         * (NUM_BUCKETS - MAX_EXACT)).astype(jnp.int32)
    large = jnp.minimum(large, NUM_BUCKETS - 1)
    return jnp.where(n < MAX_EXACT, n, large).astype(jnp.int32)


def _swa_kernel(q_ref, kvp_ref, kvc_ref, bias_ref, o_ref):
    n_pairs = q_ref.shape[-1] // PAIR
    n_kv = kvc_ref.shape[-1] // (2 * PAIR)
    pairs_per_kv = n_pairs // n_kv
    blocks = q_ref.shape[0] // BLOCK
    first_head = lax.broadcasted_iota(jnp.int32, (BLOCK, PAIR), 1) < HEAD_DIM
    ones = jnp.ones((2 * BLOCK, PAIR), _BF16)
    sink_key = lax.broadcasted_iota(jnp.int32, (2 * BLOCK, PAIR), 0) == 0
    first_table = jnp.where(pl.program_id(1) == 0, 0, 1)

    for t in range(blocks):
        rows = slice(t * BLOCK, (t + 1) * BLOCK)
        table = first_table if t == 0 else 1
        for g in range(n_kv):
            kc = slice(g * PAIR, (g + 1) * PAIR)
            vc = slice((n_kv + g) * PAIR, (n_kv + g + 1) * PAIR)
            if t == 0:
                k = jnp.concatenate([kvp_ref[:, kc], kvc_ref[rows, kc]], axis=0)
                v = jnp.concatenate([kvp_ref[:, vc], kvc_ref[rows, vc]], axis=0)
            else:
                band = slice((t - 1) * BLOCK, (t + 1) * BLOCK)
                k = kvc_ref[band, kc]
                v = kvc_ref[band, vc]
            k = jnp.where(sink_key, 0, k)
            v = jnp.where(sink_key, 0, v)
            v_and_ones = jnp.concatenate([v, ones], axis=1)
            qs = []
            for j in range(pairs_per_kv):
                p = g * pairs_per_kv + j
                q = q_ref[rows, p * PAIR:(p + 1) * PAIR]
                qs += [jnp.where(first_head, q, 0), jnp.where(first_head, 0, q)]
            s = _dot_nt(jnp.concatenate(qs, axis=0), k)
            es = []
            for hh in range(2 * pairs_per_kv):
                head = 2 * g * pairs_per_kv + hh
                sh = s[hh * BLOCK:(hh + 1) * BLOCK] + bias_ref[table, head]
                es.append(jnp.exp2(sh - jnp.max(sh, axis=-1, keepdims=True)).astype(_BF16))
            pv = _dot(jnp.concatenate(es, axis=0), v_and_ones)
            for j in range(pairs_per_kv):
                p = g * pairs_per_kv + j
                halves = []
                for h in range(2):
                    r = pv[(2 * j + h) * BLOCK:(2 * j + h + 1) * BLOCK]
                    halves.append(r[:, :PAIR] / r[:, PAIR:])
                o_ref[rows, p * PAIR:(p + 1) * PAIR] = (
                    jnp.where(first_head, halves[0], halves[1]).astype(o_ref.dtype))


def _swa_attention(q_src, kv_src, kv_col, d, kvw, bias_tabs, layer):
    batch, seq, _ = q_src.shape
    blocks = math.gcd(seq // BLOCK, SWA_BLOCKS_PER_STEP)
    tq = blocks * BLOCK
    return pl.pallas_call(
        _swa_kernel,
        grid=(batch, seq // tq),
        in_specs=[
            pl.BlockSpec((None, tq, d), lambda b, n: (b, n, 0)),
            pl.BlockSpec((None, BLOCK, kvw), lambda b, n: (b, jnp.maximum(n * blocks - 1, 0), kv_col)),
            pl.BlockSpec((None, tq, kvw), lambda b, n: (b, n, kv_col)),
            pl.BlockSpec((None,) + bias_tabs.shape[1:], lambda b, n: (layer, 0, 0, 0, 0),
                         pipeline_mode=pl.Buffered(1)),
        ],
        out_specs=pl.BlockSpec((None, tq, d), lambda b, n: (b, n, 0)),
        out_shape=jax.ShapeDtypeStruct((batch, seq, d), _BF16),
        compiler_params=pltpu.CompilerParams(
            dimension_semantics=("arbitrary", "arbitrary"), vmem_limit_bytes=VMEM_LIMIT_BYTES),
        name="swa_attention",
    )(q_src, kv_src, kv_src, bias_tabs)


def _duplicate_heads(w, n_heads):
    lead = w.shape[:-1]
    w = w.reshape(lead + (n_heads, 1, HEAD_DIM))
    w = jnp.broadcast_to(w, lead + (n_heads, 2, HEAD_DIM))
    return w.reshape(lead + (n_heads * PAIR,))


def kernel(x, a_norm, a_wqkv, a_wo, kv_norm, w_kv, b_kv, b_norm, b_wq, b_bq, b_sinks,
           b_wo, b_bo, rel_bias, mlp_norm, mlp_up, mlp_down, final_norm):
    batch, seq, d = x.shape
    m = batch * seq
    n_a = a_wqkv.shape[0]
    n_b = b_wq.shape[0]
    depth = n_a + n_b
    n_heads = d // HEAD_DIM
    n_kv = w_kv.shape[1] // (2 * HEAD_DIM)
    assert seq % BLOCK == 0 and m % min(ROW_TILE, m) == 0 and n_heads % (2 * n_kv) == 0

    bf = lambda w: w.astype(_BF16)
    kvw = 2 * n_kv * PAIR
    q_scale = SCALE * LOG2_E

    a_w = bf(a_norm[:, :, None] * a_wqkv * jnp.where(jnp.arange(a_wqkv.shape[2]) < d, q_scale, 1.0))

    def attention_projection(layer):
        if layer < n_a:
            return (a_w, layer), jnp.zeros((a_w.shape[2],), _F32)
        j = layer - n_a
        w = b_norm[j][:, None] * b_wq[j] * q_scale
        b = b_bq[j] * q_scale
        if j == 0:
            wk, wv = jnp.split(w_kv, 2, axis=-1)
            bk, bv = jnp.split(b_kv, 2, axis=-1)
            w_dup = jnp.concatenate([_duplicate_heads(wk, n_kv), _duplicate_heads(wv, n_kv)], -1)
            b_dup = jnp.concatenate([_duplicate_heads(bk, n_kv), _duplicate_heads(bv, n_kv)], -1)
            w = jnp.concatenate([w, kv_norm[:, None] * w_dup], axis=-1)
            b = jnp.concatenate([b, b_dup], axis=-1)
        return (bf(w)[None], 0), b

    a_wo_bf, b_wo_bf = bf(a_wo), bf(b_wo)
    bias_tabs = _bias_band(rel_bias, b_sinks, _band_buckets())
    h = x.reshape(m, d)
    y, (up, down) = _norm_proj(h, *attention_projection(0), riders=((mlp_up, 0), (mlp_down, 0)))
    y = y.reshape(batch, seq, -1)
    kv_src = None
    for layer in range(depth):
        if layer < n_a:
            attn = _sb_attention(y, batch, seq, n_heads,
                                 SB_WINDOW_FIRST_LAYER if layer == 0 else SB_WINDOW_LATER_LAYERS)
            wo, bo = (a_wo_bf, layer), jnp.zeros((d,), _F32)
        else:
            j = layer - n_a
            if j == 0:
                kv_src = y
            attn = _swa_attention(y, kv_src, d // kvw, d, kvw, bias_tabs, j)
            wo, bo = (b_wo_bf, j), b_bo[j]
        tail = functools.partial(_layer_tail, h, attn.reshape(m, d), wo, bo, mlp_norm[layer],
                                 (up[None], 0), (down[None], 0))
        if layer == depth - 1:
            return tail(final_gain=final_norm).reshape(batch, seq, d)
        h, y, (up, down) = tail(next_proj=attention_projection(layer + 1),
                                riders=((mlp_up, layer + 1), (mlp_down, layer + 1)))
        y = y.reshape(batch, seq, -1)
```

```python
import functools
import math

import jax
import jax.numpy as jnp
from jax import lax
from jax.experimental import pallas as pl
from jax.experimental.pallas import tpu as pltpu

HEAD_DIM = 64
PAIR = 2 * HEAD_DIM
BLOCK = 128
WINDOW = 128
NUM_BUCKETS = 32
MAX_EXACT = NUM_BUCKETS // 2
MAX_DISTANCE = 128
EPS = 1e-5
NEG_INF = -1e30
SCALE = 1.0 / math.sqrt(HEAD_DIM)

LOG2_E = math.log2(math.e)
SB_UNDERFLOW_LOG2 = -151.0
SB_TILE = 64
SB_WINDOW_FIRST_LAYER = 256
SB_WINDOW_LATER_LAYERS = 384
SB_SCORE_LOOKAHEAD_BLOCKS = 12
SB_SUFFIX_LOOKAHEAD = 3
SB_FLAG_GROUP = 8

SWA_BLOCKS_PER_STEP = 16

ROW_TILE = 512
FF_CHUNK = 1024
PROJ_CHUNK = 512
SUB_ROWS = 256
VMEM_LIMIT_BYTES = 56 * 1024 * 1024

_F32 = jnp.float32
_BF16 = jnp.bfloat16
BF16_SUBLANES = 16


def _dot(a, b):
    return jnp.dot(a, b, preferred_element_type=_F32)


def _dot_nt(a, b):
    return lax.dot_general(a, b, (((1,), (1,)), ((), ())), preferred_element_type=_F32)


def _rms_unit(x):
    ms = jnp.mean(x * x, axis=-1, keepdims=True)
    return x * lax.rsqrt(ms + EPS)


def _resident(shape):
    return pl.BlockSpec(shape, lambda *_: (0,) * len(shape), pipeline_mode=pl.Buffered(1))


def _resident_layer(stack, index):
    return pl.BlockSpec((None,) + stack.shape[1:], lambda *_: (index,) + (0,) * (stack.ndim - 1),
                        pipeline_mode=pl.Buffered(1))


def _cast_riders(steps, riders):
    in_specs, inputs, out_specs, out_shapes = [], [], [], []
    for stack, index in riders:
        rows, cols = stack.shape[1:]
        share = rows // steps
        in_specs.append(pl.BlockSpec((None, share, cols), lambda i, index=index: (index, i, 0)))
        inputs.append(stack)
        out_specs.append(pl.BlockSpec((share, cols), lambda i: (i, 0)))
        out_shapes.append(jax.ShapeDtypeStruct((rows, cols), _BF16))
    return in_specs, inputs, out_specs, out_shapes


def _can_ride(steps, riders):
    return all(r[0].shape[1] % (steps * BF16_SUBLANES) == 0 for r in riders)


def _copy_as_bf16(srcs, dsts):
    for src, dst in zip(srcs, dsts):
        dst[...] = src[...].astype(_BF16)


def _row_subtiles(tm):
    step = min(SUB_ROWS, tm)
    return [slice(r, r + step) for r in range(0, tm, step)]


def _project(subs, x_units, w_ref, b_ref, y_ref):
    xb = [x.astype(_BF16) for x in x_units]
    n = y_ref.shape[-1]
    for c in range(0, n, PROJ_CHUNK):
        cols = slice(c, min(c + PROJ_CHUNK, n))
        for s, x in zip(subs, xb):
            y_ref[s, cols] = (_dot(x, w_ref[:, cols]) + b_ref[:, cols]).astype(y_ref.dtype)


def _norm_proj_kernel(x_ref, w_ref, b_ref, *refs):
    n_riders = (len(refs) - 1) // 2
    y_ref = refs[n_riders]
    subs = _row_subtiles(x_ref.shape[0])
    _project(subs, [_rms_unit(x_ref[s, :]) for s in subs], w_ref, b_ref, y_ref)
    _copy_as_bf16(refs[:n_riders], refs[n_riders + 1:])


def _norm_proj(x, w, b, riders=()):
    m, d = x.shape
    n = w[0].shape[-1]
    tm = math.gcd(2 * ROW_TILE, m)
    ride = _can_ride(m // tm, riders)
    r_in, r_args, r_out, r_shapes = _cast_riders(m // tm, riders if ride else ())
    y, *copies = pl.pallas_call(
        _norm_proj_kernel,
        grid=(m // tm,),
        in_specs=[pl.BlockSpec((tm, d), lambda i: (i, 0)), _resident_layer(*w), _resident((1, n))] + r_in,
        out_specs=[pl.BlockSpec((tm, n), lambda i: (i, 0))] + r_out,
        out_shape=[jax.ShapeDtypeStruct((m, n), _BF16)] + r_shapes,
        compiler_params=pltpu.CompilerParams(
            dimension_semantics=("arbitrary",), vmem_limit_bytes=VMEM_LIMIT_BYTES),
        name="norm_proj",
    )(x, w[0], b.reshape(1, n), *r_args)
    return y, copies if ride else [stack[index].astype(_BF16) for stack, index in riders]


def _attn_out_and_mlp(h_ref, a_ref, wo_ref, bo_ref, g_ref, wup_ref, wdn_ref):
    subs = _row_subtiles(h_ref.shape[0])
    h1 = [h_ref[s, :] + _dot(a_ref[s, :], wo_ref[...]) + bo_ref[...] for s in subs]
    xn = [(_rms_unit(x) * g_ref[...]).astype(_BF16) for x in h1]
    acc = h1
    d_ff = wup_ref.shape[1]
    for c in range(0, d_ff, FF_CHUNK):
        u = [jnp.maximum(_dot(x, wup_ref[:, c:c + FF_CHUNK]), 0.0) for x in xn]
        acc = [a + _dot((v * v).astype(_BF16), wdn_ref[c:c + FF_CHUNK, :]) for a, v in zip(acc, u)]
    return subs, acc


def _mid_layer_kernel(h_ref, a_ref, wo_ref, bo_ref, g_ref, wup_ref, wdn_ref, wn_ref, bn_ref,
                      *refs):
    n_riders = (len(refs) - 2) // 2
    o_ref, y_ref = refs[n_riders:n_riders + 2]
    subs, h2 = _attn_out_and_mlp(h_ref, a_ref, wo_ref, bo_ref, g_ref, wup_ref, wdn_ref)
    for s, x in zip(subs, h2):
        o_ref[s, :] = x
    _project(subs, [_rms_unit(x) for x in h2], wn_ref, bn_ref, y_ref)
    _copy_as_bf16(refs[:n_riders], refs[n_riders + 2:])


def _last_layer_kernel(h_ref, a_ref, wo_ref, bo_ref, g_ref, wup_ref, wdn_ref, fg_ref, o_ref):
    subs, h2 = _attn_out_and_mlp(h_ref, a_ref, wo_ref, bo_ref, g_ref, wup_ref, wdn_ref)
    for s, x in zip(subs, h2):
        o_ref[s, :] = _rms_unit(x) * fg_ref[...]


def _layer_tail(h, a, wo, bo, g, wup, wdn, *, next_proj=None, riders=(), final_gain=None):
    m, d = h.shape
    tm = math.gcd(2 * ROW_TILE, m) if next_proj is None else min(ROW_TILE, m)
    row = lambda i: (i, 0)
    in_specs = [
        pl.BlockSpec((tm, d), row),
        pl.BlockSpec((tm, d), row),
        _resident_layer(*wo),
        _resident((1, d)),
        _resident((1, d)),
        _resident_layer(*wup),
        _resident_layer(*wdn),
    ]
    args = [h, a, wo[0], bo.reshape(1, d), g.reshape(1, d), wup[0], wdn[0]]
    params = pltpu.CompilerParams(
        dimension_semantics=("arbitrary",), vmem_limit_bytes=VMEM_LIMIT_BYTES)
    if next_proj is None:
        return pl.pallas_call(
            _last_layer_kernel,
            grid=(m // tm,),
            in_specs=in_specs + [_resident((1, d))],
            out_specs=pl.BlockSpec((tm, d), row),
            out_shape=jax.ShapeDtypeStruct((m, d), _F32),
            compiler_params=params,
            name="last_layer",
        )(*args, final_gain.reshape(1, d))
    wn, bn = next_proj
    n = wn[0].shape[-1]
    ride = _can_ride(m // tm, riders)
    r_in, r_args, r_out, r_shapes = _cast_riders(m // tm, riders if ride else ())
    h2, y, *copies = pl.pallas_call(
        _mid_layer_kernel,
        grid=(m // tm,),
        in_specs=in_specs + [_resident_layer(*wn), _resident((1, n))] + r_in,
        out_specs=[pl.BlockSpec((tm, d), row), pl.BlockSpec((tm, n), row)] + r_out,
        out_shape=[jax.ShapeDtypeStruct((m, d), _F32), jax.ShapeDtypeStruct((m, n), _BF16)] + r_shapes,
        compiler_params=params,
        name="mid_layer",
    )(*args, wn[0], bn.reshape(1, n), *r_args)
    return h2, y, copies if ride else [stack[index].astype(_BF16) for stack, index in riders]


def _sb_attn_kernel(q_ref, k_ref, v_ref, o_ref, acc_ref, carry_ref, least_ref, unfinished_ref,
                    *, window_keys):
    n_t = q_ref.shape[0] // SB_TILE
    row = lax.broadcasted_iota(jnp.int32, (SB_TILE, BLOCK), 0)
    col = lax.broadcasted_iota(jnp.int32, (SB_TILE, BLOCK), 1)
    first_head = col < HEAD_DIM
    key = lax.broadcasted_iota(jnp.int32, (BLOCK, BLOCK), 0)
    pos = lax.broadcasted_iota(jnp.int32, (BLOCK, BLOCK), 1)
    suffix_and_ones = jnp.concatenate(
        [(key >= pos).astype(_BF16), jnp.ones((BLOCK, BLOCK), _BF16)], axis=1)
    suffix_and_ones = jnp.concatenate([suffix_and_ones, suffix_and_ones], axis=0)

    def tile_rows(t):
        start = t * SB_TILE
        return pl.ds(start if isinstance(t, int) else pl.multiple_of(start, SB_TILE), SB_TILE)

    def load_q(t):
        q = q_ref[tile_rows(t), :]
        return jnp.concatenate([jnp.where(first_head, q, 0), jnp.where(first_head, 0, q)], axis=0)

    def window(t):
        end = (t + 1) * SB_TILE
        n_b = min(pl.cdiv(end, BLOCK), window_keys // BLOCK)
        start = max(end - n_b * BLOCK, 0)
        offsets = []
        for b in range(n_b):
            off = t * SB_TILE - start - b * BLOCK
            assert off > -SB_TILE
            offsets.append(None if off >= BLOCK else off)
        return start, n_b, offsets

    def scores(q2, k, masks):
        z = _dot_nt(q2, k)
        if all(m is None for m in masks):
            return z
        hidden = jnp.concatenate(
            [jnp.zeros((SB_TILE, BLOCK), _F32) if m is None else jnp.where(m, 0.0, NEG_INF)
             for m in masks], axis=1)
        return z + jnp.concatenate([hidden, hidden], axis=0)

    def suffix_sums(z):
        neg_abs = lax.bitcast_convert_type(
            lax.bitcast_convert_type(z, jnp.uint32) | jnp.uint32(0x80000000), _F32)
        sp = jnp.maximum(z, 0.0) + jnp.log2(1.0 + jnp.exp2(neg_abs))
        split = []
        for h in range(2):
            for b in range(z.shape[1] // BLOCK):
                s = sp[h * SB_TILE:(h + 1) * SB_TILE, b * BLOCK:(b + 1) * BLOCK]
                hi = s.astype(_BF16)
                lo = (s - hi.astype(_F32)).astype(_BF16)
                split.append(jnp.concatenate([hi, lo], axis=1))
        return _dot(jnp.concatenate(split, axis=0), suffix_and_ones)

    def weighted_values(z, cs, v, carry_in):
        n_b = z.shape[1] // BLOCK
        ws, carries = [], []
        for h in range(2):
            carry = None if carry_in is None else carry_in[h]
            wb = [None] * n_b
            for b in reversed(range(n_b)):
                r = cs[(h * n_b + b) * SB_TILE:(h * n_b + b + 1) * SB_TILE]
                arg = z[h * SB_TILE:(h + 1) * SB_TILE, b * BLOCK:(b + 1) * BLOCK] - r[:, :BLOCK]
                if carry is not None:
                    arg = arg - carry
                wb[b] = jnp.exp2(arg).astype(_BF16)
                carry = r[:, BLOCK:] if carry is None else carry + r[:, BLOCK:]
            ws.append(jnp.concatenate(wb, axis=1))
            carries.append(carry)
        out = _dot(jnp.concatenate(ws, axis=0), v)
        return jnp.where(first_head, out[:SB_TILE], out[SB_TILE:]), carries

    def window_scores(t):
        start, n_b, offsets = window(t)
        return scores(load_q(t), k_ref[pl.ds(start, n_b * BLOCK), :],
                      [None if off is None else col < row + off for off in offsets])

    zs, css = {}, {}

    def issue_scores(t):
        if t < n_t:
            zs[t] = window_scores(t)

    def issue_suffix_sums(t):
        if t < n_t:
            css[t] = suffix_sums(zs[t])

    score_lookahead = max(SB_SCORE_LOOKAHEAD_BLOCKS // (window_keys // BLOCK), SB_SUFFIX_LOOKAHEAD)
    for t in range(score_lookahead):
        issue_scores(t)
    for t in range(SB_SUFFIX_LOOKAHEAD):
        issue_suffix_sums(t)
    group_unfinished = {}
    for t in range(n_t):
        issue_scores(t + score_lookahead)
        issue_suffix_sums(t + SB_SUFFIX_LOOKAHEAD)
        start, n_b, _ = window(t)
        out, carry = weighted_values(zs.pop(t), css.pop(t), v_ref[pl.ds(start, n_b * BLOCK), :], None)
        o_ref[tile_rows(t), :] = out.astype(o_ref.dtype)
        if start > 0:
            acc_ref[t] = out
            carry_ref[t, 0] = carry[0]
            carry_ref[t, 1] = carry[1]
            least = jnp.min(jnp.minimum(carry[0], carry[1]))
            least_ref[t] = least
            unfinished = least < -SB_UNDERFLOW_LOG2
            g = t // SB_FLAG_GROUP
            group_unfinished[g] = (jnp.logical_or(group_unfinished[g], unfinished)
                                   if g in group_unfinished else unfinished)
    for g, unfinished in group_unfinished.items():
        unfinished_ref[g] = unfinished.astype(jnp.int32)

    first_open = next((t for t in range(n_t) if window(t)[0] > 0), n_t)

    def finish_tile(t, _):
        def more(state):
            remaining, least = state
            return jnp.logical_and(remaining > 0, least < -SB_UNDERFLOW_LOG2)

        def older_block(state):
            remaining, _ = state
            first = jnp.maximum(remaining - BLOCK, 0)
            rows = pl.ds(pl.multiple_of(first, SB_TILE), BLOCK)
            z = scores(load_q(t), k_ref[rows, :], [col < remaining - first])
            out, carry = weighted_values(z, suffix_sums(z), v_ref[rows, :],
                                         [carry_ref[t, 0], carry_ref[t, 1]])
            acc_ref[t] += out
            carry_ref[t, 0] = carry[0]
            carry_ref[t, 1] = carry[1]
            return first, jnp.min(jnp.minimum(carry[0], carry[1]))

        @pl.when(least_ref[t] < -SB_UNDERFLOW_LOG2)
        def _():
            lax.while_loop(more, older_block, ((t + 1) * SB_TILE - window_keys, least_ref[t]))
            o_ref[tile_rows(t), :] = acc_ref[t].astype(o_ref.dtype)
        return 0

    def finish_group(g, _):
        @pl.when(unfinished_ref[g] != 0)
        def _():
            lax.fori_loop(jnp.maximum(g * SB_FLAG_GROUP, first_open),
                          jnp.minimum((g + 1) * SB_FLAG_GROUP, n_t), finish_tile, 0)
        return 0

    if n_t > first_open:
        lax.fori_loop(first_open // SB_FLAG_GROUP, pl.cdiv(n_t, SB_FLAG_GROUP), finish_group, 0)


def _sb_attention(qkv, batch, seq, n_heads, window_keys):
    n_pairs = n_heads // 2
    n_t = seq // SB_TILE
    blk = (None, seq, PAIR)
    return pl.pallas_call(
        functools.partial(_sb_attn_kernel, window_keys=window_keys),
        grid=(batch, n_pairs),
        in_specs=[
            pl.BlockSpec(blk, lambda b, p: (b, 0, p)),
            pl.BlockSpec(blk, lambda b, p: (b, 0, n_pairs + p)),
            pl.BlockSpec(blk, lambda b, p: (b, 0, 2 * n_pairs + p)),
        ],
        out_specs=pl.BlockSpec(blk, lambda b, p: (b, 0, p)),
        out_shape=jax.ShapeDtypeStruct((batch, seq, n_heads * HEAD_DIM), _BF16),
        scratch_shapes=[pltpu.VMEM((n_t, SB_TILE, PAIR), _F32),
                        pltpu.VMEM((n_t, 2, SB_TILE, BLOCK), _F32),
                        pltpu.SMEM((n_t,), _F32),
                        pltpu.SMEM((pl.cdiv(n_t, SB_FLAG_GROUP),), jnp.int32)],
        compiler_params=pltpu.CompilerParams(
            dimension_semantics=("arbitrary", "arbitrary"), vmem_limit_bytes=VMEM_LIMIT_BYTES),
        name="sb_attention",
    )(qkv, qkv, qkv)


def _bias_band_kernel(rel_ref, sink_ref, bucket_ref, o_ref):
    layer = pl.program_id(0)
    head = pl.program_id(1)
    bucket = bucket_ref[...]
    qi = lax.broadcasted_iota(jnp.int32, bucket.shape, 0)
    kj = lax.broadcasted_iota(jnp.int32, bucket.shape, 1)
    dist = qi + BLOCK - kj
    valid = (dist >= 0) & (dist < WINDOW)
    bias = jnp.zeros(bucket.shape, _F32)
    for b in range(NUM_BUCKETS):
        bias = jnp.where(bucket == b, rel_ref[b, head] * LOG2_E, bias)
    sink = sink_ref[layer, head] * LOG2_E
    o_ref[0] = jnp.where(kj == 0, sink, jnp.where(valid & (kj >= BLOCK), bias, NEG_INF))
    o_ref[1] = jnp.where(kj == 0, sink, jnp.where(valid, bias, NEG_INF))


def _bias_band(rel_bias, sinks, bucket):
    n_layers, n_heads = sinks.shape
    return pl.pallas_call(
        _bias_band_kernel,
        grid=(n_layers, n_heads),
        in_specs=[
            pl.BlockSpec(memory_space=pltpu.SMEM),
            pl.BlockSpec(memory_space=pltpu.SMEM),
            pl.BlockSpec(bucket.shape, lambda j, h: (0, 0)),
        ],
        out_specs=pl.BlockSpec((None, 2, None) + bucket.shape, lambda j, h: (j, 0, h, 0, 0)),
        out_shape=jax.ShapeDtypeStruct((n_layers, 2, n_heads) + bucket.shape, _F32),
        name="bias_band",
    )(rel_bias, sinks, bucket)


def _band_buckets():
    qi = jnp.arange(BLOCK)[:, None]
    kj = jnp.arange(2 * BLOCK)[None, :]
    n = jnp.maximum(qi + BLOCK - kj, 0)
    nf = jnp.maximum(n, 1).astype(_F32)
    large = MAX_EXACT + (jnp.log(nf / MAX_EXACT) / math.log(MAX_DISTANCE / MAX_EXACT)
                         * (NUM_BUCKETS - MAX_EXACT)).astype(jnp.int32)
    large = jnp.minimum(large, NUM_BUCKETS - 1)
    return jnp.where(n < MAX_EXACT, n, large).astype(jnp.int32)


def _swa_kernel(q_ref, kvp_ref, kvc_ref, bias_ref, o_ref):
    n_pairs = q_ref.shape[-1] // PAIR
    n_kv = kvc_ref.shape[-1] // (2 * PAIR)
    pairs_per_kv = n_pairs // n_kv
    blocks = q_ref.shape[0] // BLOCK
    first_head = lax.broadcasted_iota(jnp.int32, (BLOCK, PAIR), 1) < HEAD_DIM
    ones = jnp.ones((2 * BLOCK, PAIR), _BF16)
    sink_key = lax.broadcasted_iota(jnp.int32, (2 * BLOCK, PAIR), 0) == 0
    first_table = jnp.where(pl.program_id(1) == 0, 0, 1)

    for t in range(blocks):
        rows = slice(t * BLOCK, (t + 1) * BLOCK)
        table = first_table if t == 0 else 1
        for g in range(n_kv):
            kc = slice(g * PAIR, (g + 1) * PAIR)
            vc = slice((n_kv + g) * PAIR, (n_kv + g + 1) * PAIR)
            if t == 0:
                k = jnp.concatenate([kvp_ref[:, kc], kvc_ref[rows, kc]], axis=0)
                v = jnp.concatenate([kvp_ref[:, vc], kvc_ref[rows, vc]], axis=0)
            else:
                band = slice((t - 1) * BLOCK, (t + 1) * BLOCK)
                k = kvc_ref[band, kc]
                v = kvc_ref[band, vc]
            k = jnp.where(sink_key, 0, k)
            v = jnp.where(sink_key, 0, v)
            v_and_ones = jnp.concatenate([v, ones], axis=1)
            qs = []
            for j in range(pairs_per_kv):
                p = g * pairs_per_kv + j
                q = q_ref[rows, p * PAIR:(p + 1) * PAIR]
                qs += [jnp.where(first_head, q, 0), jnp.where(first_head, 0, q)]
            s = _dot_nt(jnp.concatenate(qs, axis=0), k)
            es = []
            for hh in range(2 * pairs_per_kv):
                head = 2 * g * pairs_per_kv + hh
                sh = s[hh * BLOCK:(hh + 1) * BLOCK] + bias_ref[table, head]
                es.append(jnp.exp2(sh - jnp.max(sh, axis=-1, keepdims=True)).astype(_BF16))
            pv = _dot(jnp.concatenate(es, axis=0), v_and_ones)
            for j in range(pairs_per_kv):
                p = g * pairs_per_kv + j
                halves = []
                for h in range(2):
                    r = pv[(2 * j + h) * BLOCK:(2 * j + h + 1) * BLOCK]
                    halves.append(r[:, :PAIR] / r[:, PAIR:])
                o_ref[rows, p * PAIR:(p + 1) * PAIR] = (
                    jnp.where(first_head, halves[0], halves[1]).astype(o_ref.dtype))


def _swa_attention(q_src, kv_src, kv_col, d, kvw, bias_tabs, layer):
    batch, seq, _ = q_src.shape
    blocks = math.gcd(seq // BLOCK, SWA_BLOCKS_PER_STEP)
    tq = blocks * BLOCK
    return pl.pallas_call(
        _swa_kernel,
        grid=(batch, seq // tq),
        in_specs=[
            pl.BlockSpec((None, tq, d), lambda b, n: (b, n, 0)),
            pl.BlockSpec((None, BLOCK, kvw), lambda b, n: (b, jnp.maximum(n * blocks - 1, 0), kv_col)),
            pl.BlockSpec((None, tq, kvw), lambda b, n: (b, n, kv_col)),
            pl.BlockSpec((None,) + bias_tabs.shape[1:], lambda b, n: (layer, 0, 0, 0, 0),
                         pipeline_mode=pl.Buffered(1)),
        ],
        out_specs=pl.BlockSpec((None, tq, d), lambda b, n: (b, n, 0)),
        out_shape=jax.ShapeDtypeStruct((batch, seq, d), _BF16),
        compiler_params=pltpu.CompilerParams(
            dimension_semantics=("arbitrary", "arbitrary"), vmem_limit_bytes=VMEM_LIMIT_BYTES),
        name="swa_attention",
    )(q_src, kv_src, kv_src, bias_tabs)


def _duplicate_heads(w, n_heads):
    lead = w.shape[:-1]
    w = w.reshape(lead + (n_heads, 1, HEAD_DIM))
    w = jnp.broadcast_to(w, lead + (n_heads, 2, HEAD_DIM))
    return w.reshape(lead + (n_heads * PAIR,))


def kernel(x, a_norm, a_wqkv, a_wo, kv_norm, w_kv, b_kv, b_norm, b_wq, b_bq, b_sinks,
           b_wo, b_bo, rel_bias, mlp_norm, mlp_up, mlp_down, final_norm):
    batch, seq, d = x.shape
    m = batch * seq
    n_a = a_wqkv.shape[0]
    n_b = b_wq.shape[0]
    depth = n_a + n_b
    n_heads = d // HEAD_DIM
    n_kv = w_kv.shape[1] // (2 * HEAD_DIM)
    assert seq % BLOCK == 0 and m % min(ROW_TILE, m) == 0 and n_heads % (2 * n_kv) == 0

    bf = lambda w: w.astype(_BF16)
    kvw = 2 * n_kv * PAIR
    q_scale = SCALE * LOG2_E

    a_w = bf(a_norm[:, :, None] * a_wqkv * jnp.where(jnp.arange(a_wqkv.shape[2]) < d, q_scale, 1.0))

    def attention_projection(layer):
        if layer < n_a:
            return (a_w, layer), jnp.zeros((a_w.shape[2],), _F32)
        j = layer - n_a
        w = b_norm[j][:, None] * b_wq[j] * q_scale
        b = b_bq[j] * q_scale
        if j == 0:
            wk, wv = jnp.split(w_kv, 2, axis=-1)
            bk, bv = jnp.split(b_kv, 2, axis=-1)
            w_dup = jnp.concatenate([_duplicate_heads(wk, n_kv), _duplicate_heads(wv, n_kv)], -1)
            b_dup = jnp.concatenate([_duplicate_heads(bk, n_kv), _duplicate_heads(bv, n_kv)], -1)
            w = jnp.concatenate([w, kv_norm[:, None] * w_dup], axis=-1)
            b = jnp.concatenate([b, b_dup], axis=-1)
        return (bf(w)[None], 0), b

    a_wo_bf, b_wo_bf = bf(a_wo), bf(b_wo)
    bias_tabs = _bias_band(rel_bias, b_sinks, _band_buckets())
    h = x.reshape(m, d)
    y, (up, down) = _norm_proj(h, *attention_projection(0), riders=((mlp_up, 0), (mlp_down, 0)))
    y = y.reshape(batch, seq, -1)
    kv_src = None
    for layer in range(depth):
        if layer < n_a:
            attn = _sb_attention(y, batch, seq, n_heads,
                                 SB_WINDOW_FIRST_LAYER if layer == 0 else SB_WINDOW_LATER_LAYERS)
            wo, bo = (a_wo_bf, layer), jnp.zeros((d,), _F32)
        else:
            j = layer - n_a
            if j == 0:
                kv_src = y
            attn = _swa_attention(y, kv_src, d // kvw, d, kvw, bias_tabs, j)
            wo, bo = (b_wo_bf, j), b_bo[j]
        tail = functools.partial(_layer_tail, h, attn.reshape(m, d), wo, bo, mlp_norm[layer],
                                 (up[None], 0), (down[None], 0))
        if layer == depth - 1:
            return tail(final_gain=final_norm).reshape(batch, seq, d)
        h, y, (up, down) = tail(next_proj=attention_projection(layer + 1),
                                riders=((mlp_up, layer + 1), (mlp_down, layer + 1)))
        y = y.reshape(batch, seq, -1)
```

```python
import functools
import math

import jax
import jax.numpy as jnp
from jax import lax
from jax.experimental import pallas as pl
from jax.experimental.pallas import tpu as pltpu

HEAD_DIM = 64
PAIR = 2 * HEAD_DIM
BLOCK = 128
WINDOW = 128
NUM_BUCKETS = 32
MAX_EXACT = NUM_BUCKETS // 2
MAX_DISTANCE = 128
EPS = 1e-5
NEG_INF = -1e30
SCALE = 1.0 / math.sqrt(HEAD_DIM)

LOG2_E = math.log2(math.e)
SB_UNDERFLOW_LOG2 = -151.0
SB_TILE = 64
SB_WINDOW_FIRST_LAYER = 256
SB_WINDOW_LATER_LAYERS = 384
SB_SCORE_LOOKAHEAD_BLOCKS = 12
SB_SUFFIX_LOOKAHEAD = 3
SB_FLAG_GROUP = 8

SWA_BLOCKS_PER_STEP = 16

ROW_TILE = 512
FF_CHUNK = 1024
PROJ_CHUNK = 512
SUB_ROWS = 256
VMEM_LIMIT_BYTES = 56 * 1024 * 1024

_F32 = jnp.float32
_BF16 = jnp.bfloat16
BF16_SUBLANES = 16


def _dot(a, b):
    return jnp.dot(a, b, preferred_element_type=_F32)


def _dot_nt(a, b):
    return lax.dot_general(a, b, (((1,), (1,)), ((), ())), preferred_element_type=_F32)


def _rms_unit(x):
    ms = jnp.mean(x * x, axis=-1, keepdims=True)
    return x * lax.rsqrt(ms + EPS)


def _resident(shape):
    return pl.BlockSpec(shape, lambda *_: (0,) * len(shape), pipeline_mode=pl.Buffered(1))


def _resident_layer(stack, index):
    return pl.BlockSpec((None,) + stack.shape[1:], lambda *_: (index,) + (0,) * (stack.ndim - 1),
                        pipeline_mode=pl.Buffered(1))


def _cast_riders(steps, riders):
    in_specs, inputs, out_specs, out_shapes = [], [], [], []
    for stack, index in riders:
        rows, cols = stack.shape[1:]
        share = rows // steps
        in_specs.append(pl.BlockSpec((None, share, cols), lambda i, index=index: (index, i, 0)))
        inputs.append(stack)
        out_specs.append(pl.BlockSpec((share, cols), lambda i: (i, 0)))
        out_shapes.append(jax.ShapeDtypeStruct((rows, cols), _BF16))
    return in_specs, inputs, out_specs, out_shapes


def _can_ride(steps, riders):
    return all(r[0].shape[1] % (steps * BF16_SUBLANES) == 0 for r in riders)


def _copy_as_bf16(srcs, dsts):
    for src, dst in zip(srcs, dsts):
        dst[...] = src[...].astype(_BF16)


def _row_subtiles(tm):
    step = min(SUB_ROWS, tm)
    return [slice(r, r + step) for r in range(0, tm, step)]


def _project(subs, x_units, w_ref, b_ref, y_ref):
    xb = [x.astype(_BF16) for x in x_units]
    n = y_ref.shape[-1]
    for c in range(0, n, PROJ_CHUNK):
        cols = slice(c, min(c + PROJ_CHUNK, n))
        for s, x in zip(subs, xb):
            y_ref[s, cols] = (_dot(x, w_ref[:, cols]) + b_ref[:, cols]).astype(y_ref.dtype)


def _norm_proj_kernel(x_ref, w_ref, b_ref, *refs):
    n_riders = (len(refs) - 1) // 2
    y_ref = refs[n_riders]
    subs = _row_subtiles(x_ref.shape[0])
    _project(subs, [_rms_unit(x_ref[s, :]) for s in subs], w_ref, b_ref, y_ref)
    _copy_as_bf16(refs[:n_riders], refs[n_riders + 1:])


def _norm_proj(x, w, b, riders=()):
    m, d = x.shape
    n = w[0].shape[-1]
    tm = math.gcd(2 * ROW_TILE, m)
    ride = _can_ride(m // tm, riders)
    r_in, r_args, r_out, r_shapes = _cast_riders(m // tm, riders if ride else ())
    y, *copies = pl.pallas_call(
        _norm_proj_kernel,
        grid=(m // tm,),
        in_specs=[pl.BlockSpec((tm, d), lambda i: (i, 0)), _resident_layer(*w), _resident((1, n))] + r_in,
        out_specs=[pl.BlockSpec((tm, n), lambda i: (i, 0))] + r_out,
        out_shape=[jax.ShapeDtypeStruct((m, n), _BF16)] + r_shapes,
        compiler_params=pltpu.CompilerParams(
            dimension_semantics=("arbitrary",), vmem_limit_bytes=VMEM_LIMIT_BYTES,
            allow_input_fusion=[False, True, False] + [False] * len(r_args)),
        name="norm_proj",
    )(x, w[0], b.reshape(1, n), *r_args)
    return y, copies if ride else [stack[index].astype(_BF16) for stack, index in riders]


def _attn_out_and_mlp(h_ref, a_ref, wo_ref, bo_ref, g_ref, wup_ref, wdn_ref):
    subs = _row_subtiles(h_ref.shape[0])
    h1 = [h_ref[s, :] + _dot(a_ref[s, :], wo_ref[...]) + bo_ref[...] for s in subs]
    xn = [(_rms_unit(x) * g_ref[...]).astype(_BF16) for x in h1]
    acc = h1
    d_ff = wup_ref.shape[1]
    for c in range(0, d_ff, FF_CHUNK):
        u = [jnp.maximum(_dot(x, wup_ref[:, c:c + FF_CHUNK]), 0.0) for x in xn]
        acc = [a + _dot((v * v).astype(_BF16), wdn_ref[c:c + FF_CHUNK, :]) for a, v in zip(acc, u)]
    return subs, acc


def _mid_layer_kernel(h_ref, a_ref, wo_ref, bo_ref, g_ref, wup_ref, wdn_ref, wn_ref, bn_ref,
                      *refs):
    n_riders = (len(refs) - 2) // 2
    o_ref, y_ref = refs[n_riders:n_riders + 2]
    subs, h2 = _attn_out_and_mlp(h_ref, a_ref, wo_ref, bo_ref, g_ref, wup_ref, wdn_ref)
    for s, x in zip(subs, h2):
        o_ref[s, :] = x
    _project(subs, [_rms_unit(x) for x in h2], wn_ref, bn_ref, y_ref)
    _copy_as_bf16(refs[:n_riders], refs[n_riders + 2:])


def _last_layer_kernel(h_ref, a_ref, wo_ref, bo_ref, g_ref, wup_ref, wdn_ref, fg_ref, o_ref):
    subs, h2 = _attn_out_and_mlp(h_ref, a_ref, wo_ref, bo_ref, g_ref, wup_ref, wdn_ref)
    for s, x in zip(subs, h2):
        o_ref[s, :] = _rms_unit(x) * fg_ref[...]


def _layer_tail(h, a, wo, bo, g, wup, wdn, *, next_proj=None, riders=(), final_gain=None):
    m, d = h.shape
    tm = math.gcd(2 * ROW_TILE, m) if next_proj is None else min(ROW_TILE, m)
    row = lambda i: (i, 0)
    in_specs = [
        pl.BlockSpec((tm, d), row),
        pl.BlockSpec((tm, d), row),
        _resident_layer(*wo),
        _resident((1, d)),
        _resident((1, d)),
        _resident_layer(*wup),
        _resident_layer(*wdn),
    ]
    args = [h, a, wo[0], bo.reshape(1, d), g.reshape(1, d), wup[0], wdn[0]]
    params = pltpu.CompilerParams(
        dimension_semantics=("arbitrary",), vmem_limit_bytes=VMEM_LIMIT_BYTES)
    if next_proj is None:
        return pl.pallas_call(
            _last_layer_kernel,
            grid=(m // tm,),
            in_specs=in_specs + [_resident((1, d))],
            out_specs=pl.BlockSpec((tm, d), row),
            out_shape=jax.ShapeDtypeStruct((m, d), _F32),
            compiler_params=params,
            name="last_layer",
        )(*args, final_gain.reshape(1, d))
    wn, bn = next_proj
    n = wn[0].shape[-1]
    ride = _can_ride(m // tm, riders)
    r_in, r_args, r_out, r_shapes = _cast_riders(m // tm, riders if ride else ())
    h2, y, *copies = pl.pallas_call(
        _mid_layer_kernel,
        grid=(m // tm,),
        in_specs=in_specs + [_resident_layer(*wn), _resident((1, n))] + r_in,
        out_specs=[pl.BlockSpec((tm, d), row), pl.BlockSpec((tm, n), row)] + r_out,
        out_shape=[jax.ShapeDtypeStruct((m, d), _F32), jax.ShapeDtypeStruct((m, n), _BF16)] + r_shapes,
        compiler_params=params,
        name="mid_layer",
    )(*args, wn[0], bn.reshape(1, n), *r_args)
    return h2, y, copies if ride else [stack[index].astype(_BF16) for stack, index in riders]


def _sb_attn_kernel(q_ref, k_ref, v_ref, o_ref, acc_ref, carry_ref, least_ref, unfinished_ref,
                    *, window_keys):
    n_t = q_ref.shape[0] // SB_TILE
    row = lax.broadcasted_iota(jnp.int32, (SB_TILE, BLOCK), 0)
    col = lax.broadcasted_iota(jnp.int32, (SB_TILE, BLOCK), 1)
    first_head = col < HEAD_DIM
    key = lax.broadcasted_iota(jnp.int32, (BLOCK, BLOCK), 0)
    pos = lax.broadcasted_iota(jnp.int32, (BLOCK, BLOCK), 1)
    suffix_and_ones = jnp.concatenate(
        [(key >= pos).astype(_BF16), jnp.ones((BLOCK, BLOCK), _BF16)], axis=1)
    suffix_and_ones = jnp.concatenate([suffix_and_ones, suffix_and_ones], axis=0)

    def tile_rows(t):
        start = t * SB_TILE
        return pl.ds(start if isinstance(t, int) else pl.multiple_of(start, SB_TILE), SB_TILE)

    def load_q(t):
        q = q_ref[tile_rows(t), :]
        return jnp.concatenate([jnp.where(first_head, q, 0), jnp.where(first_head, 0, q)], axis=0)

    def window(t):
        end = (t + 1) * SB_TILE
        n_b = min(pl.cdiv(end, BLOCK), window_keys // BLOCK)
        start = max(end - n_b * BLOCK, 0)
        offsets = []
        for b in range(n_b):
            off = t * SB_TILE - start - b * BLOCK
            assert off > -SB_TILE
            offsets.append(None if off >= BLOCK else off)
        return start, n_b, offsets

    def scores(q2, k, masks):
        z = _dot_nt(q2, k)
        if all(m is None for m in masks):
            return z
        hidden = jnp.concatenate(
            [jnp.zeros((SB_TILE, BLOCK), _F32) if m is None else jnp.where(m, 0.0, NEG_INF)
             for m in masks], axis=1)
        return z + jnp.concatenate([hidden, hidden], axis=0)

    def suffix_sums(z):
        neg_abs = lax.bitcast_convert_type(
            lax.bitcast_convert_type(z, jnp.uint32) | jnp.uint32(0x80000000), _F32)
        sp = jnp.maximum(z, 0.0) + jnp.log2(1.0 + jnp.exp2(neg_abs))
        split = []
        for h in range(2):
            for b in range(z.shape[1] // BLOCK):
                s = sp[h * SB_TILE:(h + 1) * SB_TILE, b * BLOCK:(b + 1) * BLOCK]
                hi = s.astype(_BF16)
                lo = (s - hi.astype(_F32)).astype(_BF16)
                split.append(jnp.concatenate([hi, lo], axis=1))
        return _dot(jnp.concatenate(split, axis=0), suffix_and_ones)

    def weighted_values(z, cs, v, carry_in):
        n_b = z.shape[1] // BLOCK
        ws, carries = [], []
        for h in range(2):
            carry = None if carry_in is None else carry_in[h]
            wb = [None] * n_b
            for b in reversed(range(n_b)):
                r = cs[(h * n_b + b) * SB_TILE:(h * n_b + b + 1) * SB_TILE]
                arg = z[h * SB_TILE:(h + 1) * SB_TILE, b * BLOCK:(b + 1) * BLOCK] - r[:, :BLOCK]
                if carry is not None:
                    arg = arg - carry
                wb[b] = jnp.exp2(arg).astype(_BF16)
                carry = r[:, BLOCK:] if carry is None else carry + r[:, BLOCK:]
            ws.append(jnp.concatenate(wb, axis=1))
            carries.append(carry)
        out = _dot(jnp.concatenate(ws, axis=0), v)
        return jnp.where(first_head, out[:SB_TILE], out[SB_TILE:]), carries

    def window_scores(t):
        start, n_b, offsets = window(t)
        return scores(load_q(t), k_ref[pl.ds(start, n_b * BLOCK), :],
                      [None if off is None else col < row + off for off in offsets])

    zs, css = {}, {}

    def issue_scores(t):
        if t < n_t:
            zs[t] = window_scores(t)

    def issue_suffix_sums(t):
        if t < n_t:
            css[t] = suffix_sums(zs[t])

    score_lookahead = max(SB_SCORE_LOOKAHEAD_BLOCKS // (window_keys // BLOCK), SB_SUFFIX_LOOKAHEAD)
    for t in range(score_lookahead):
        issue_scores(t)
    for t in range(SB_SUFFIX_LOOKAHEAD):
        issue_suffix_sums(t)
    group_unfinished = {}
    for t in range(n_t):
        issue_scores(t + score_lookahead)
        issue_suffix_sums(t + SB_SUFFIX_LOOKAHEAD)
        start, n_b, _ = window(t)
        out, carry = weighted_values(zs.pop(t), css.pop(t), v_ref[pl.ds(start, n_b * BLOCK), :], None)
        o_ref[tile_rows(t), :] = out.astype(o_ref.dtype)
        if start > 0:
            acc_ref[t] = out
            carry_ref[t, 0] = carry[0]
            carry_ref[t, 1] = carry[1]
            least = jnp.min(jnp.minimum(carry[0], carry[1]))
            least_ref[t] = least
            unfinished = least < -SB_UNDERFLOW_LOG2
            g = t // SB_FLAG_GROUP
            group_unfinished[g] = (jnp.logical_or(group_unfinished[g], unfinished)
                                   if g in group_unfinished else unfinished)
    for g, unfinished in group_unfinished.items():
        unfinished_ref[g] = unfinished.astype(jnp.int32)

    first_open = next((t for t in range(n_t) if window(t)[0] > 0), n_t)

    def finish_tile(t, _):
        def more(state):
            remaining, least = state
            return jnp.logical_and(remaining > 0, least < -SB_UNDERFLOW_LOG2)

        def older_block(state):
            remaining, _ = state
            first = jnp.maximum(remaining - BLOCK, 0)
            rows = pl.ds(pl.multiple_of(first, SB_TILE), BLOCK)
            z = scores(load_q(t), k_ref[rows, :], [col < remaining - first])
            out, carry = weighted_values(z, suffix_sums(z), v_ref[rows, :],
                                         [carry_ref[t, 0], carry_ref[t, 1]])
            acc_ref[t] += out
            carry_ref[t, 0] = carry[0]
            carry_ref[t, 1] = carry[1]
            return first, jnp.min(jnp.minimum(carry[0], carry[1]))

        @pl.when(least_ref[t] < -SB_UNDERFLOW_LOG2)
        def _():
            lax.while_loop(more, older_block, ((t + 1) * SB_TILE - window_keys, least_ref[t]))
            o_ref[tile_rows(t), :] = acc_ref[t].astype(o_ref.dtype)
        return 0

    def finish_group(g, _):
        @pl.when(unfinished_ref[g] != 0)
        def _():
            lax.fori_loop(jnp.maximum(g * SB_FLAG_GROUP, first_open),
                          jnp.minimum((g + 1) * SB_FLAG_GROUP, n_t), finish_tile, 0)
        return 0

    if n_t > first_open:
        lax.fori_loop(first_open // SB_FLAG_GROUP, pl.cdiv(n_t, SB_FLAG_GROUP), finish_group, 0)


def _sb_attention(qkv, batch, seq, n_heads, window_keys):
    n_pairs = n_heads // 2
    n_t = seq // SB_TILE
    blk = (None, seq, PAIR)
    return pl.pallas_call(
        functools.partial(_sb_attn_kernel, window_keys=window_keys),
        grid=(batch, n_pairs),
        in_specs=[
            pl.BlockSpec(blk, lambda b, p: (b, 0, p)),
            pl.BlockSpec(blk, lambda b, p: (b, 0, n_pairs + p)),
            pl.BlockSpec(blk, lambda b, p: (b, 0, 2 * n_pairs + p)),
        ],
        out_specs=pl.BlockSpec(blk, lambda b, p: (b, 0, p)),
        out_shape=jax.ShapeDtypeStruct((batch, seq, n_heads * HEAD_DIM), _BF16),
        scratch_shapes=[pltpu.VMEM((n_t, SB_TILE, PAIR), _F32),
                        pltpu.VMEM((n_t, 2, SB_TILE, BLOCK), _F32),
                        pltpu.SMEM((n_t,), _F32),
                        pltpu.SMEM((pl.cdiv(n_t, SB_FLAG_GROUP),), jnp.int32)],
        compiler_params=pltpu.CompilerParams(
            dimension_semantics=("arbitrary", "arbitrary"), vmem_limit_bytes=VMEM_LIMIT_BYTES),
        name="sb_attention",
    )(qkv, qkv, qkv)


def _bias_band_kernel(rel_ref, sink_ref, bucket_ref, o_ref):
    layer = pl.program_id(0)
    head = pl.program_id(1)
    bucket = bucket_ref[...]
    qi = lax.broadcasted_iota(jnp.int32, bucket.shape, 0)
    kj = lax.broadcasted_iota(jnp.int32, bucket.shape, 1)
    dist = qi + BLOCK - kj
    valid = (dist >= 0) & (dist < WINDOW)
    bias = jnp.zeros(bucket.shape, _F32)
    for b in range(NUM_BUCKETS):
        bias = jnp.where(bucket == b, rel_ref[b, head] * LOG2_E, bias)
    sink = sink_ref[layer, head] * LOG2_E
    o_ref[0] = jnp.where(kj == 0, sink, jnp.where(valid & (kj >= BLOCK), bias, NEG_INF))
    o_ref[1] = jnp.where(kj == 0, sink, jnp.where(valid, bias, NEG_INF))


def _bias_band(rel_bias, sinks, bucket):
    n_layers, n_heads = sinks.shape
    return pl.pallas_call(
        _bias_band_kernel,
        grid=(n_layers, n_heads),
        in_specs=[
            pl.BlockSpec(memory_space=pltpu.SMEM),
            pl.BlockSpec(memory_space=pltpu.SMEM),
            pl.BlockSpec(bucket.shape, lambda j, h: (0, 0)),
        ],
        out_specs=pl.BlockSpec((None, 2, None) + bucket.shape, lambda j, h: (j, 0, h, 0, 0)),
        out_shape=jax.ShapeDtypeStruct((n_layers, 2, n_heads) + bucket.shape, _F32),
        name="bias_band",
    )(rel_bias, sinks, bucket)


def _band_buckets():
    qi = jnp.arange(BLOCK)[:, None]
    kj = jnp.arange(2 * BLOCK)[None, :]
    n = jnp.maximum(qi + BLOCK - kj, 0)
    nf = jnp.maximum(n, 1).astype(_F32)
    large = MAX_EXACT + (jnp.log(nf / MAX_EXACT) / math.log(MAX_DISTANCE / MAX_EXACT)
                         * (NUM_BUCKETS - MAX_EXACT)).astype(jnp.int32)
    large = jnp.minimum(large, NUM_BUCKETS - 1)
    return jnp.where(n < MAX_EXACT, n, large).astype(jnp.int32)


def _swa_kernel(q_ref, kvp_ref, kvc_ref, bias_ref, o_ref):
    n_pairs = q_ref.shape[-1] // PAIR
    n_kv = kvc_ref.shape[-1] // (2 * PAIR)
    pairs_per_kv = n_pairs // n_kv
    blocks = q_ref.shape[0] // BLOCK
    first_head = lax.broadcasted_iota(jnp.int32, (BLOCK, PAIR), 1) < HEAD_DIM
    ones = jnp.ones((2 * BLOCK, PAIR), _BF16)
    sink_key = lax.broadcasted_iota(jnp.int32, (2 * BLOCK, PAIR), 0) == 0
    first_table = jnp.where(pl.program_id(1) == 0, 0, 1)

    for t in range(blocks):
        rows = slice(t * BLOCK, (t + 1) * BLOCK)
        table = first_table if t == 0 else 1
        for g in range(n_kv):
            kc = slice(g * PAIR, (g + 1) * PAIR)
            vc = slice((n_kv + g) * PAIR, (n_kv + g + 1) * PAIR)
            if t == 0:
                k = jnp.concatenate([kvp_ref[:, kc], kvc_ref[rows, kc]], axis=0)
                v = jnp.concatenate([kvp_ref[:, vc], kvc_ref[rows, vc]], axis=0)
            else:
                band = slice((t - 1) * BLOCK, (t + 1) * BLOCK)
                k = kvc_ref[band, kc]
                v = kvc_ref[band, vc]
            k = jnp.where(sink_key, 0, k)
            v = jnp.where(sink_key, 0, v)
            v_and_ones = jnp.concatenate([v, ones], axis=1)
            qs = []
            for j in range(pairs_per_kv):
                p = g * pairs_per_kv + j
                q = q_ref[rows, p * PAIR:(p + 1) * PAIR]
                qs += [jnp.where(first_head, q, 0), jnp.where(first_head, 0, q)]
            s = _dot_nt(jnp.concatenate(qs, axis=0), k)
            es = []
            for hh in range(2 * pairs_per_kv):
                head = 2 * g * pairs_per_kv + hh
                sh = s[hh * BLOCK:(hh + 1) * BLOCK] + bias_ref[table, head]
                es.append(jnp.exp2(sh - jnp.max(sh, axis=-1, keepdims=True)).astype(_BF16))
            pv = _dot(jnp.concatenate(es, axis=0), v_and_ones)
            for j in range(pairs_per_kv):
                p = g * pairs_per_kv + j
                halves = []
                for h in range(2):
                    r = pv[(2 * j + h) * BLOCK:(2 * j + h + 1) * BLOCK]
                    halves.append(r[:, :PAIR] / r[:, PAIR:])
                o_ref[rows, p * PAIR:(p + 1) * PAIR] = (
                    jnp.where(first_head, halves[0], halves[1]).astype(o_ref.dtype))


def _swa_attention(q_src, kv_src, kv_col, d, kvw, bias_tabs, layer):
    batch, seq, _ = q_src.shape
    blocks = math.gcd(seq // BLOCK, SWA_BLOCKS_PER_STEP)
    tq = blocks * BLOCK
    return pl.pallas_call(
        _swa_kernel,
        grid=(batch, seq // tq),
        in_specs=[
            pl.BlockSpec((None, tq, d), lambda b, n: (b, n, 0)),
            pl.BlockSpec((None, BLOCK, kvw), lambda b, n: (b, jnp.maximum(n * blocks - 1, 0), kv_col)),
            pl.BlockSpec((None, tq, kvw), lambda b, n: (b, n, kv_col)),
            pl.BlockSpec((None,) + bias_tabs.shape[1:], lambda b, n: (layer, 0, 0, 0, 0),
                         pipeline_mode=pl.Buffered(1)),
        ],
        out_specs=pl.BlockSpec((None, tq, d), lambda b, n: (b, n, 0)),
        out_shape=jax.ShapeDtypeStruct((batch, seq, d), _BF16),
        compiler_params=pltpu.CompilerParams(
            dimension_semantics=("arbitrary", "arbitrary"), vmem_limit_bytes=VMEM_LIMIT_BYTES),
        name="swa_attention",
    )(q_src, kv_src, kv_src, bias_tabs)


def _duplicate_heads(w, n_heads):
    lead = w.shape[:-1]
    w = w.reshape(lead + (n_heads, 1, HEAD_DIM))
    w = jnp.broadcast_to(w, lead + (n_heads, 2, HEAD_DIM))
    return w.reshape(lead + (n_heads * PAIR,))


def kernel(x, a_norm, a_wqkv, a_wo, kv_norm, w_kv, b_kv, b_norm, b_wq, b_bq, b_sinks,
           b_wo, b_bo, rel_bias, mlp_norm, mlp_up, mlp_down, final_norm):
    batch, seq, d = x.shape
    m = batch * seq
    n_a = a_wqkv.shape[0]
    n_b = b_wq.shape[0]
    depth = n_a + n_b
    n_heads = d // HEAD_DIM
    n_kv = w_kv.shape[1] // (2 * HEAD_DIM)
    assert seq % BLOCK == 0 and m % min(ROW_TILE, m) == 0 and n_heads % (2 * n_kv) == 0

    bf = lambda w: w.astype(_BF16)
    kvw = 2 * n_kv * PAIR
    q_scale = SCALE * LOG2_E

    a_w = bf(a_norm[:, :, None] * a_wqkv * jnp.where(jnp.arange(a_wqkv.shape[2]) < d, q_scale, 1.0))

    def attention_projection(layer):
        if layer < n_a:
            return (a_w, layer), jnp.zeros((a_w.shape[2],), _F32)
        j = layer - n_a
        w = b_norm[j][:, None] * b_wq[j] * q_scale
        b = b_bq[j] * q_scale
        if j == 0:
            wk, wv = jnp.split(w_kv, 2, axis=-1)
            bk, bv = jnp.split(b_kv, 2, axis=-1)
            w_dup = jnp.concatenate([_duplicate_heads(wk, n_kv), _duplicate_heads(wv, n_kv)], -1)
            b_dup = jnp.concatenate([_duplicate_heads(bk, n_kv), _duplicate_heads(bv, n_kv)], -1)
            w = jnp.concatenate([w, kv_norm[:, None] * w_dup], axis=-1)
            b = jnp.concatenate([b, b_dup], axis=-1)
        return (bf(w)[None], 0), b

    a_wo_bf, b_wo_bf = bf(a_wo), bf(b_wo)
    bias_tabs = _bias_band(rel_bias, b_sinks, _band_buckets())
    h = x.reshape(m, d)
    y, (up, down) = _norm_proj(h, *attention_projection(0), riders=((mlp_up, 0), (mlp_down, 0)))
    y = y.reshape(batch, seq, -1)
    kv_src = None
    for layer in range(depth):
        if layer < n_a:
            attn = _sb_attention(y, batch, seq, n_heads,
                                 SB_WINDOW_FIRST_LAYER if layer == 0 else SB_WINDOW_LATER_LAYERS)
            wo, bo = (a_wo_bf, layer), jnp.zeros((d,), _F32)
        else:
            j = layer - n_a
            if j == 0:
                kv_src = y
            attn = _swa_attention(y, kv_src, d // kvw, d, kvw, bias_tabs, j)
            wo, bo = (b_wo_bf, j), b_bo[j]
        tail = functools.partial(_layer_tail, h, attn.reshape(m, d), wo, bo, mlp_norm[layer],
                                 (up[None], 0), (down[None], 0))
        if layer == depth - 1:
            return tail(final_gain=final_norm).reshape(batch, seq, d)
        h, y, (up, down) = tail(next_proj=attention_projection(layer + 1),
                                riders=((mlp_up, layer + 1), (mlp_down, layer + 1)))
        y = y.reshape(batch, seq, -1)
```
